```python
import math
import jax, jax.numpy as jnp
from jax import lax
import numpy as np

D_MODEL = 2048
BATCH = 4
SEQ = 2048
DEPTH = 4
DEC_BATCH = 128
DEC_SEQ = 8
PAST_LEN = 16384
PAGE_SIZE = 128

D_MIX = D_MODEL
S5_WIDTH = D_MIX // 4
S5_GROUP = 16
S5_GROUPS = S5_WIDTH // S5_GROUP
S5_STATE = 64
RWKV_WIDTH = (3 * D_MIX) // 8
RWKV_HEAD = 64
RWKV_HEADS = RWKV_WIDTH // RWKV_HEAD
RWKV_DECAY_LORA = 64
RWKV_A_LORA = 64
RWKV_GATE_LORA = 128
RWKV_SHIFT_WIDTH = 3 * RWKV_WIDTH + RWKV_DECAY_LORA + RWKV_A_LORA + RWKV_GATE_LORA
RWKV_GN_EPS = 64e-5
HGRN_WIDTH = D_MIX - S5_WIDTH - RWKV_WIDTH
HGRN_EXPAND = 128
HGRN_HEADS = HGRN_WIDTH // HGRN_EXPAND
HGRN_HEAD_V = HGRN_WIDTH // HGRN_HEADS
HGRN_CHUNK = 64
HGRN_EPS = 1e-5
N_IN = S5_WIDTH + RWKV_SHIFT_WIDTH + 4 * HGRN_WIDTH
D_FF = ((8 * D_MODEL // 3 + 255) // 256) * 256
NORM_EPS = 1e-6

kernel_name = 'hybrid_s5_rwkv7_hgrn2_step'


def rms_norm(x, w, eps=NORM_EPS):
    x32 = x.astype(jnp.float32)
    y = x32 * lax.rsqrt(jnp.mean(x32 * x32, axis=-1, keepdims=True) + eps)
    return (y * w.astype(jnp.float32)).astype(x.dtype)


def swiglu_ffn(x, w_gate, w_up, w_down):
    return (jax.nn.silu(x @ w_gate) * (x @ w_up)) @ w_down


def s5_mixer(u, x0_re, x0_im, a_re, a_im, log_dt, b_re, b_im, c_re, c_im, d, w_glu, b_glu):
    n, t, _ = u.shape
    f32 = jnp.float32
    lam = lax.complex(a_re.astype(f32), a_im.astype(f32))
    dt = jnp.exp(log_dt.astype(f32))[:, None]
    a_bar = jnp.exp(lam * dt)
    b = lax.complex(b_re.astype(f32), b_im.astype(f32))
    b_bar = ((a_bar - 1.0) / lam)[..., None] * b
    c = lax.complex(c_re.astype(f32), c_im.astype(f32))
    u32 = u.astype(f32)
    ug = u32.reshape(n, t, S5_GROUPS, S5_GROUP).astype(jnp.complex64)
    bu = jnp.einsum('gph,ntgh->ntgp', b_bar, ug)
    x0 = lax.complex(x0_re.astype(f32), x0_im.astype(f32))
    bu = bu.at[:, 0].add(a_bar * x0)
    a_seq = jnp.broadcast_to(a_bar, bu.shape)

    def combine(e1, e2):
        a1, b1 = e1
        a2, b2 = e2
        return a2 * a1, a2 * b1 + b2

    _, xs = lax.associative_scan(combine, (a_seq, bu), axis=1)
    y = jnp.real(jnp.einsum('ghp,ntgp->ntgh', c, xs)).reshape(n, t, S5_WIDTH)
    y = jax.nn.gelu(y + d.astype(f32) * u32)
    out = y * jax.nn.sigmoid(y @ w_glu.astype(f32) + b_glu.astype(f32))
    x_last = xs[:, -1]
    return out, jnp.real(x_last), jnp.imag(x_last)


def rwkv7_mixer(z, shift0, s0, mu, w0, w2, a0, a2, g2, k_k, k_a, r_k, ln_w, ln_b):
    n, t, _ = z.shape
    f32 = jnp.float32
    z = z.astype(f32)
    prev = jnp.concatenate([shift0.astype(f32)[:, None], z[:, :-1]], axis=1)
    zm = z + (prev - z) * mu.astype(f32)
    o1 = 3 * RWKV_WIDTH
    o2 = o1 + RWKV_DECAY_LORA
    o3 = o2 + RWKV_A_LORA
    r = zm[..., :RWKV_WIDTH]
    k = zm[..., RWKV_WIDTH:2 * RWKV_WIDTH]
    v = zm[..., 2 * RWKV_WIDTH:o1]
    wi, ai, gi = zm[..., o1:o2], zm[..., o2:o3], zm[..., o3:]
    w = -jax.nn.softplus(-(w0 + jnp.tanh(wi) @ w2)) - 0.5
    decay = jnp.exp(-jnp.exp(w))
    a = jax.nn.sigmoid(a0 + ai @ a2)
    g = jax.nn.sigmoid(gi) @ g2

    def heads(y):
        return y.reshape(n, t, RWKV_HEADS, RWKV_HEAD)

    kk = heads(k * k_k)
    kk = kk * lax.rsqrt(jnp.maximum(jnp.sum(kk * kk, axis=-1, keepdims=True), 1e-24))
    k = k * (1.0 + (a - 1.0) * k_a)
    rh, kh, vh, dh, ah = heads(r), heads(k), heads(v), heads(decay), heads(a)
    bh = kk * ah

    def step(S, inp):
        r_t, k_t, v_t, d_t, kk_t, b_t = inp
        sa = jnp.einsum('nhvk,nhk->nhv', S, -kk_t)
        S = S * d_t[:, :, None, :] + sa[..., None] * b_t[:, :, None, :] + v_t[..., None] * k_t[:, :, None, :]
        return S, jnp.einsum('nhvk,nhk->nhv', S, r_t)

    seq = tuple(jnp.moveaxis(y_, 1, 0) for y_ in (rh, kh, vh, dh, kk, bh))
    s_last, ys = lax.scan(step, s0.astype(f32), seq)
    y = jnp.moveaxis(ys, 0, 1)
    mean = jnp.mean(y, axis=-1, keepdims=True)
    var = jnp.mean(jnp.square(y - mean), axis=-1, keepdims=True)
    y = ((y - mean) * lax.rsqrt(var + RWKV_GN_EPS)).reshape(n, t, RWKV_WIDTH) * ln_w + ln_b
    bonus = jnp.sum(rh * kh * r_k, axis=-1, keepdims=True) * vh
    y = (y + bonus.reshape(n, t, RWKV_WIDTH)) * g
    return y, z[:, -1], s_last


def hgrn2_mixer(q, f, i, g, s0, lb, norm_w):
    n, t, _ = q.shape
    f32 = jnp.float32

    def heads(y):
        return y.astype(f32).reshape(n, t, HGRN_HEADS, -1)

    fg = lb + (1.0 - lb) * jax.nn.sigmoid(f.astype(f32))
    qh = jax.nn.silu(heads(q))
    kh = heads(1.0 - fg)
    lfh = heads(jnp.log(fg))
    vh = heads(i)
    c = math.gcd(t, HGRN_CHUNK)
    nc = t // c

    def chunks(y):
        return jnp.transpose(y.reshape(n, nc, c, HGRN_HEADS, -1), (1, 0, 3, 2, 4))

    mask = jnp.tril(jnp.ones((c, c), bool))[:, :, None]

    def step(S, inp):
        q_c, k_c, v_c, lf_c = inp
        b = jnp.cumsum(lf_c, axis=2)
        o_inter = jnp.einsum('nhtk,nhkv->nhtv', q_c * jnp.exp(b), S)
        diff = b[:, :, :, None, :] - b[:, :, None, :, :]
        dec = jnp.where(mask, jnp.exp(jnp.where(mask, diff, 0.0)), 0.0)
        att = jnp.einsum('nhtk,nhtsk,nhsk->nhts', q_c, dec, k_c)
        o = o_inter + jnp.einsum('nhts,nhsv->nhtv', att, v_c)
        b_last = b[:, :, -1]
        S = jnp.exp(b_last)[..., None] * S + jnp.einsum('nhsk,nhsv->nhkv', k_c * jnp.exp(b_last[:, :, None] - b), v_c)
        return S, o

    s_last, os_ = lax.scan(step, s0.astype(f32), tuple(chunks(y) for y in (qh, kh, vh, lfh)))
    o = jnp.transpose(os_, (1, 0, 3, 2, 4)).reshape(n, t, HGRN_HEADS, HGRN_HEAD_V)
    o = o * lax.rsqrt(jnp.mean(o * o, axis=-1, keepdims=True) + HGRN_EPS)
    o = o.reshape(n, t, HGRN_WIDTH) * norm_w * jax.nn.silu(g.astype(f32))
    return o, s_last


def _normal(k, shape, scale):
    return jax.random.normal(k, shape, jnp.float32) * scale


def setup_inputs(seed: int = 0) -> dict:
    key = jax.random.key(seed)
    ks = iter(jax.random.split(key, 48))
    L = DEPTH
    G, P, H = S5_GROUPS, S5_STATE, S5_GROUP
    n_idx = jnp.arange(P, dtype=jnp.float32)
    inp = {}
    inp['x_prompt'] = _normal(next(ks), (BATCH, SEQ, D_MODEL), 1.0)
    inp['x_sample'] = _normal(next(ks), (DEC_BATCH, DEC_SEQ, D_MODEL), 1.0)
    inp['state_s5_re'] = _normal(next(ks), (L, DEC_BATCH, G, P), 0.5)
    inp['state_s5_im'] = _normal(next(ks), (L, DEC_BATCH, G, P), 0.5)
    inp['state_rwkv_shift'] = _normal(next(ks), (L, DEC_BATCH, RWKV_SHIFT_WIDTH), 1.0)
    inp['state_rwkv_wkv'] = _normal(next(ks), (L, DEC_BATCH, RWKV_HEADS, RWKV_HEAD, RWKV_HEAD), 1.0)
    inp['state_hgrn'] = _normal(next(ks), (L, DEC_BATCH, HGRN_HEADS, HGRN_EXPAND, HGRN_HEAD_V), 0.5)
    inp['norm_ffn1'] = 1.0 + _normal(next(ks), (L, D_MODEL), 0.02)
    inp['ffn1_w_gate'] = _normal(next(ks), (L, D_MODEL, D_FF), D_MODEL ** -0.5)
    inp['ffn1_w_up'] = _normal(next(ks), (L, D_MODEL, D_FF), D_MODEL ** -0.5)
    inp['ffn1_w_down'] = _normal(next(ks), (L, D_FF, D_MODEL), D_FF ** -0.5)
    inp['norm_mix'] = 1.0 + _normal(next(ks), (L, D_MODEL), 0.02)
    inp['w_in'] = _normal(next(ks), (L, D_MODEL, N_IN), D_MODEL ** -0.5)
    inp['s5_a_re'] = -0.5 + _normal(next(ks), (L, G, P), 0.01)
    inp['s5_a_im'] = math.pi * n_idx + _normal(next(ks), (L, G, P), 0.01)
    inp['s5_log_dt'] = jax.random.uniform(next(ks), (L, G), jnp.float32, math.log(1e-3), math.log(1e-1))
    inp['s5_b_re'] = _normal(next(ks), (L, G, P, H), (2.0 * H) ** -0.5)
    inp['s5_b_im'] = _normal(next(ks), (L, G, P, H), (2.0 * H) ** -0.5)
    inp['s5_c_re'] = _normal(next(ks), (L, G, H, P), (2.0 * P) ** -0.5)
    inp['s5_c_im'] = _normal(next(ks), (L, G, H, P), (2.0 * P) ** -0.5)
    inp['s5_d'] = _normal(next(ks), (L, S5_WIDTH), 1.0)
    inp['s5_w_glu'] = _normal(next(ks), (L, S5_WIDTH, S5_WIDTH), S5_WIDTH ** -0.5)
    inp['s5_b_glu'] = _normal(next(ks), (L, S5_WIDTH), 0.01)
    inp['rwkv_mu'] = jax.random.uniform(next(ks), (L, RWKV_SHIFT_WIDTH), jnp.float32, 0.0, 1.0)
    inp['rwkv_w0'] = jax.random.uniform(next(ks), (L, RWKV_WIDTH), jnp.float32, -6.0, 0.0)
    inp['rwkv_w2'] = _normal(next(ks), (L, RWKV_DECAY_LORA, RWKV_WIDTH), RWKV_DECAY_LORA ** -0.5)
    inp['rwkv_a0'] = _normal(next(ks), (L, RWKV_WIDTH), 0.5)
    inp['rwkv_a2'] = _normal(next(ks), (L, RWKV_A_LORA, RWKV_WIDTH), RWKV_A_LORA ** -0.5)
    inp['rwkv_g2'] = _normal(next(ks), (L, RWKV_GATE_LORA, RWKV_WIDTH), RWKV_GATE_LORA ** -0.5)
    inp['rwkv_k_k'] = 0.85 + _normal(next(ks), (L, RWKV_WIDTH), 0.05)
    inp['rwkv_k_a'] = 1.0 + _normal(next(ks), (L, RWKV_WIDTH), 0.05)
    inp['rwkv_r_k'] = _normal(next(ks), (L, RWKV_HEADS, RWKV_HEAD), 0.1)
    inp['rwkv_ln_w'] = 1.0 + _normal(next(ks), (L, RWKV_WIDTH), 0.02)
    inp['rwkv_ln_b'] = _normal(next(ks), (L, RWKV_WIDTH), 0.01)
    inp['hgrn_lb_raw'] = _normal(next(ks), (L, HGRN_WIDTH), 0.1)
    inp['hgrn_norm_w'] = 1.0 + _normal(next(ks), (L, HGRN_WIDTH), 0.02)
    inp['w_out'] = _normal(next(ks), (L, D_MIX, D_MODEL), D_MIX ** -0.5)
    inp['norm_ffn2'] = 1.0 + _normal(next(ks), (L, D_MODEL), 0.02)
    inp['ffn2_w_gate'] = _normal(next(ks), (L, D_MODEL, D_FF), D_MODEL ** -0.5)
    inp['ffn2_w_up'] = _normal(next(ks), (L, D_MODEL, D_FF), D_MODEL ** -0.5)
    inp['ffn2_w_down'] = _normal(next(ks), (L, D_FF, D_MODEL), D_FF ** -0.5)
    inp['norm_final'] = 1.0 + _normal(next(ks), (D_MODEL,), 0.02)
    return inp


def reference(x_prompt, x_sample, state_s5_re, state_s5_im, state_rwkv_shift, state_rwkv_wkv, state_hgrn,
              norm_ffn1, ffn1_w_gate, ffn1_w_up, ffn1_w_down, norm_mix, w_in,
              s5_a_re, s5_a_im, s5_log_dt, s5_b_re, s5_b_im, s5_c_re, s5_c_im, s5_d, s5_w_glu, s5_b_glu,
              rwkv_mu, rwkv_w0, rwkv_w2, rwkv_a0, rwkv_a2, rwkv_g2, rwkv_k_k, rwkv_k_a, rwkv_r_k,
              rwkv_ln_w, rwkv_ln_b, hgrn_lb_raw, hgrn_norm_w, w_out,
              norm_ffn2, ffn2_w_gate, ffn2_w_up, ffn2_w_down, norm_final):
    p_lb = jax.nn.softmax(hgrn_lb_raw.astype(jnp.float32), axis=0)
    lower_bounds = jnp.cumsum(p_lb, axis=0) - p_lb[0]
    o_rw = S5_WIDTH
    o_hg = S5_WIDTH + RWKV_SHIFT_WIDTH

    def run(x, s5_re0, s5_im0, shift0, wkv0, hgrn0):
        s5_re_l, s5_im_l, shift_l, wkv_l, hgrn_l = [], [], [], [], []
        for l in range(DEPTH):
            h = rms_norm(x, norm_ffn1[l])
            x = x + 0.5 * swiglu_ffn(h, ffn1_w_gate[l], ffn1_w_up[l], ffn1_w_down[l])
            h = rms_norm(x, norm_mix[l])
            p = h @ w_in[l]
            y_s5, s_re, s_im = s5_mixer(p[..., :o_rw], s5_re0[l], s5_im0[l], s5_a_re[l], s5_a_im[l],
                                        s5_log_dt[l], s5_b_re[l], s5_b_im[l], s5_c_re[l], s5_c_im[l],
                                        s5_d[l], s5_w_glu[l], s5_b_glu[l])
            y_rw, sh, wkv = rwkv7_mixer(p[..., o_rw:o_hg], shift0[l], wkv0[l], rwkv_mu[l], rwkv_w0[l],
                                        rwkv_w2[l], rwkv_a0[l], rwkv_a2[l], rwkv_g2[l], rwkv_k_k[l],
                                        rwkv_k_a[l], rwkv_r_k[l], rwkv_ln_w[l], rwkv_ln_b[l])
            hq = p[..., o_hg:o_hg + HGRN_WIDTH]
            hf = p[..., o_hg + HGRN_WIDTH:o_hg + 2 * HGRN_WIDTH]
            hi = p[..., o_hg + 2 * HGRN_WIDTH:o_hg + 3 * HGRN_WIDTH]
            hgt = p[..., o_hg + 3 * HGRN_WIDTH:]
            y_hg, hs = hgrn2_mixer(hq, hf, hi, hgt, hgrn0[l], lower_bounds[l], hgrn_norm_w[l])
            mix = jnp.concatenate([y_s5, y_rw, y_hg], axis=-1).astype(x.dtype)
            x = x + mix @ w_out[l]
            h = rms_norm(x, norm_ffn2[l])
            x = x + 0.5 * swiglu_ffn(h, ffn2_w_gate[l], ffn2_w_up[l], ffn2_w_down[l])
            s5_re_l.append(s_re)
            s5_im_l.append(s_im)
            shift_l.append(sh)
            wkv_l.append(wkv)
            hgrn_l.append(hs)
        y = rms_norm(x, norm_final)
        return y, jnp.stack(s5_re_l), jnp.stack(s5_im_l), jnp.stack(shift_l), jnp.stack(wkv_l), jnp.stack(hgrn_l)

    nb = x_prompt.shape[0]

    def zeros_like_state(s):
        return jnp.zeros((DEPTH, nb) + s.shape[2:], jnp.float32)

    y_p, s5re_p, s5im_p, shift_p, wkv_p, hgrn_p = run(
        x_prompt, zeros_like_state(state_s5_re), zeros_like_state(state_s5_im),
        zeros_like_state(state_rwkv_shift), zeros_like_state(state_rwkv_wkv), zeros_like_state(state_hgrn))
    y_s, s5re_s, s5im_s, shift_s, wkv_s, hgrn_s = run(
        x_sample, state_s5_re, state_s5_im, state_rwkv_shift, state_rwkv_wkv, state_hgrn)
    return (y_p, y_s, s5re_p, s5im_p, shift_p, wkv_p, hgrn_p, s5re_s, s5im_s, shift_s, wkv_s, hgrn_s)
```

```python
import functools
import math

import numpy as np
import jax
import jax.numpy as jnp
from jax import lax
from jax.experimental import pallas as pl
from jax.experimental.pallas import tpu as pltpu

F32 = jnp.float32
BF16 = jnp.bfloat16

NORM_EPS = 1e-6
RWKV_GN_EPS = 64e-5
HGRN_EPS = 1e-5

D_MODEL = 2048
S5_WIDTH = 512
S5_GROUP = 16
S5_GROUPS = 32
S5_STATE = 64
S5_LANES = S5_GROUPS * S5_STATE
S5_GB = 4
RWKV_WIDTH = 768
RWKV_HEAD = 64
RWKV_HEADS = 12
RWKV_PAIRS = 6
RWKV_SHIFT_WIDTH = 2560
HGRN_WIDTH = 768
HGRN_HEADS = 6
HGRN_DIM = 128
N_IN = 6144

SUBLANES = 8
LANES = 128
VMEM_LIMIT = 56 * 1024 * 1024


def _cparams(sem):
    return pltpu.CompilerParams(dimension_semantics=sem, vmem_limit_bytes=VMEM_LIMIT)


def _split2(x):
    hi = x.astype(BF16)
    lo = (x - hi.astype(F32)).astype(BF16)
    return hi, lo


def _split3(x):
    p1 = x.astype(BF16)
    r1 = x - p1.astype(F32)
    p2 = r1.astype(BF16)
    p3 = (r1 - p2.astype(F32)).astype(BF16)
    return p1, p2, p3


def _dot(a, b):
    return jnp.dot(a, b, preferred_element_type=F32)


def _dot_nt(a, b):
    return lax.dot_general(a, b, (((1,), (1,)), ((), ())), preferred_element_type=F32)


def _dot_tn(a, b):
    return lax.dot_general(a, b, (((0,), (0,)), ((), ())), preferred_element_type=F32)


def _sigmoid(x):
    return 1.0 / (1.0 + jnp.exp(-x))


def _ffn_kernel(x_ref, nw_ref, wg_ref, wu_ref, wd_ref, o_ref, h_ref, acc_ref):
    j = pl.program_id(1)

    @pl.when(j == 0)
    def _():
        x = x_ref[...]
        ms = jnp.mean(x * x, axis=-1, keepdims=True)
        h_ref[...] = (x * lax.rsqrt(ms + NORM_EPS) * nw_ref[...]).astype(BF16)
        acc_ref[...] = jnp.zeros_like(acc_ref)

    h = h_ref[...]
    g = _dot(h, wg_ref[...])
    u = _dot(h, wu_ref[...])
    a = (g * _sigmoid(g) * u).astype(BF16)
    acc_ref[...] += _dot(a, wd_ref[...])

    @pl.when(j == pl.num_programs(1) - 1)
    def _():
        o_ref[...] = x_ref[...] + 0.5 * acc_ref[...]


def _ffn(x, nw, wg, wu, wd, tm=512, tf=512):
    m, d = x.shape
    ff = wg.shape[1]
    return pl.pallas_call(
        _ffn_kernel,
        grid=(m // tm, ff // tf),
        in_specs=[
            pl.BlockSpec((tm, d), lambda i, j: (i, 0)),
            pl.BlockSpec((1, d), lambda i, j: (0, 0)),
            pl.BlockSpec((d, tf), lambda i, j: (0, j)),
            pl.BlockSpec((d, tf), lambda i, j: (0, j)),
            pl.BlockSpec((tf, d), lambda i, j: (j, 0)),
        ],
        out_specs=pl.BlockSpec((tm, d), lambda i, j: (i, 0)),
        out_shape=jax.ShapeDtypeStruct((m, d), F32),
        scratch_shapes=[pltpu.VMEM((tm, d), BF16), pltpu.VMEM((tm, d), F32)],
        compiler_params=_cparams(("parallel", "arbitrary")),
        name="ffn",
    )(x, nw, wg, wu, wd)


def _inproj_kernel(x_ref, nw_ref, w_ref, o_ref, h_ref):
    @pl.when(pl.program_id(1) == 0)
    def _():
        x = x_ref[...]
        ms = jnp.mean(x * x, axis=-1, keepdims=True)
        h_ref[...] = (x * lax.rsqrt(ms + NORM_EPS) * nw_ref[...]).astype(BF16)

    o_ref[...] = _dot(h_ref[...], w_ref[...])


def _inproj(x, nw, w, tm=512, tn=512):
    m, d = x.shape
    n = w.shape[1]
    return pl.pallas_call(
        _inproj_kernel,
        grid=(m // tm, n // tn),
        in_specs=[
            pl.BlockSpec((tm, d), lambda i, j: (i, 0)),
            pl.BlockSpec((1, d), lambda i, j: (0, 0)),
            pl.BlockSpec((d, tn), lambda i, j: (0, j)),
        ],
        out_specs=pl.BlockSpec((tm, tn), lambda i, j: (i, j)),
        out_shape=jax.ShapeDtypeStruct((m, n), F32),
        scratch_shapes=[pltpu.VMEM((tm, d), BF16)],
        compiler_params=_cparams(("parallel", "arbitrary")),
        name="inproj",
    )(x, nw, w)


def _outproj_kernel(x_ref, a_ref, b_ref, c_ref, wa_ref, wb_ref, wc_ref, o_ref):
    acc = _dot(a_ref[...].astype(BF16), wa_ref[...])
    acc += _dot(b_ref[...].astype(BF16), wb_ref[...])
    acc += _dot(c_ref[...].astype(BF16), wc_ref[...])
    o_ref[...] = x_ref[...] + acc


def _outproj(x, ya, yb, yc, wa, wb, wc, tm=512):
    m, d = x.shape
    row = lambda i: (i, 0)
    whole = lambda i: (0, 0)
    return pl.pallas_call(
        _outproj_kernel,
        grid=(m // tm,),
        in_specs=[
            pl.BlockSpec((tm, d), row),
            pl.BlockSpec((tm, ya.shape[1]), row),
            pl.BlockSpec((tm, yb.shape[1]), row),
            pl.BlockSpec((tm, yc.shape[1]), row),
            pl.BlockSpec(wa.shape, whole),
            pl.BlockSpec(wb.shape, whole),
            pl.BlockSpec(wc.shape, whole),
        ],
        out_specs=pl.BlockSpec((tm, d), row),
        out_shape=jax.ShapeDtypeStruct((m, d), F32),
        compiler_params=_cparams(("parallel",)),
        name="outproj",
    )(x, ya, yb, yc, wa, wb, wc)


def _rmsnorm_kernel(x_ref, nw_ref, o_ref):
    x = x_ref[...]
    ms = jnp.mean(x * x, axis=-1, keepdims=True)
    o_ref[...] = x * lax.rsqrt(ms + NORM_EPS) * nw_ref[...]


def _rmsnorm(x, nw, tm=512):
    m, d = x.shape
    return pl.pallas_call(
        _rmsnorm_kernel,
        grid=(m // tm,),
        in_specs=[pl.BlockSpec((tm, d), lambda i: (i, 0)), pl.BlockSpec((1, d), lambda i: (0, 0))],
        out_specs=pl.BlockSpec((tm, d), lambda i: (i, 0)),
        out_shape=jax.ShapeDtypeStruct((m, d), F32),
        compiler_params=_cparams(("parallel",)),
        name="final_norm",
    )(x, nw)


def _s5_kernel(u_ref, x0r_ref, x0i_ref, b3r_ref, b3i_ref, cr_ref, ci_ref, cst_ref, d_ref, wglu_ref,
               bglu_ref, y_ref, xlr_ref, xli_ref, xr_ref, xi_ref, cr_s, ci_s, *, tr, chained):
    c = pl.program_id(1)
    u = u_ref[...]
    uh, ul = _split2(u)
    for gb in range(S5_GB):
        sl = slice(gb * LANES, (gb + 1) * LANES)
        lhs = jnp.concatenate([uh[:, sl], ul[:, sl], uh[:, sl]], axis=1)
        osl = slice(gb * 512, (gb + 1) * 512)
        xr_ref[:, osl] = _dot(lhs, b3r_ref[gb])
        xi_ref[:, osl] = _dot(lhs, b3i_ref[gb])

    if chained:
        @pl.when(c == 0)
        def _():
            cr_s[...] = x0r_ref[0]
            ci_s[...] = x0i_ref[0]

    a1r, a1i, a2r, a2i, a4r, a4i, pr, pi = [cst_ref[i] for i in range(8)]

    def tile(j, carry):
        rows = pl.ds(pl.multiple_of(j * SUBLANES, SUBLANES), SUBLANES)
        xr = xr_ref[rows, :]
        xi = xi_ref[rows, :]
        for k, ar, ai in ((1, a1r, a1i), (2, a2r, a2i), (4, a4r, a4i)):
            sr = pltpu.roll(xr, k, 0)
            si = pltpu.roll(xi, k, 0)
            xr, xi = xr + ar * sr - ai * si, xi + ar * si + ai * sr
        if chained:
            x0r = cr_s[...]
            x0i = ci_s[...]
        else:
            x0r = x0r_ref[pl.ds(j, 1), :]
            x0i = x0i_ref[pl.ds(j, 1), :]
        x0r = jnp.broadcast_to(x0r, xr.shape)
        x0i = jnp.broadcast_to(x0i, xr.shape)
        xr, xi = xr + pr * x0r - pi * x0i, xi + pr * x0i + pi * x0r
        xr_ref[rows, :] = xr
        xi_ref[rows, :] = xi
        if chained:
            cr_s[...] = xr[SUBLANES - 1:SUBLANES, :]
            ci_s[...] = xi[SUBLANES - 1:SUBLANES, :]
        else:
            xlr_ref[pl.ds(j, 1), :] = xr[SUBLANES - 1:SUBLANES, :]
            xli_ref[pl.ds(j, 1), :] = xi[SUBLANES - 1:SUBLANES, :]
        return carry

    lax.fori_loop(0, tr // SUBLANES, tile, 0)

    if chained:
        @pl.when(c == pl.num_programs(1) - 1)
        def _():
            xlr_ref[0] = cr_s[...]
            xli_ref[0] = ci_s[...]

    ys = []
    for gb in range(S5_GB):
        osl = slice(gb * 512, (gb + 1) * 512)
        ys.append(_dot(xr_ref[:, osl].astype(BF16), cr_ref[gb]) + _dot(xi_ref[:, osl].astype(BF16), ci_ref[gb]))
    y = jnp.concatenate(ys, axis=1) + d_ref[...] * u
    y = 0.5 * y * (1.0 + jnp.tanh(math.sqrt(2.0 / math.pi) * (y + 0.044715 * (y * y * y))))
    z = _dot(y.astype(BF16), wglu_ref[...]) + bglu_ref[...]
    y_ref[...] = y * _sigmoid(z)


def _s5(p3, x0r, x0i, cs, nseq, t, tr=512):
    rows = nseq * t
    chained = t > SUBLANES
    if chained:
        grid = (nseq, t // tr)
        row_map = lambda n, c: (n * (t // tr) + c, 5)
        out_map = lambda n, c: (n * (t // tr) + c, 0)
        st_spec = pl.BlockSpec((1, 1, S5_LANES), lambda n, c: (n, 0, 0))
        x0r = x0r.reshape(nseq, 1, S5_LANES)
        x0i = x0i.reshape(nseq, 1, S5_LANES)
        st_shape = jax.ShapeDtypeStruct((nseq, 1, S5_LANES), F32)
    else:
        assert t == SUBLANES
        grid = (rows // tr, 1)
        row_map = lambda n, c: (n, 5)
        out_map = lambda n, c: (n, 0)
        st_spec = pl.BlockSpec((tr // SUBLANES, S5_LANES), lambda n, c: (n, 0))
        st_shape = jax.ShapeDtypeStruct((nseq, S5_LANES), F32)
    whole3 = lambda n, c: (0, 0, 0)
    whole2 = lambda n, c: (0, 0)
    y, xlr, xli = pl.pallas_call(
        functools.partial(_s5_kernel, tr=tr, chained=chained),
        grid=grid,
        in_specs=[
            pl.BlockSpec((tr, S5_WIDTH), row_map),
            st_spec, st_spec,
            pl.BlockSpec(cs["b3r"].shape, whole3),
            pl.BlockSpec(cs["b3i"].shape, whole3),
            pl.BlockSpec(cs["cr"].shape, whole3),
            pl.BlockSpec(cs["ci"].shape, whole3),
            pl.BlockSpec(cs["scan"].shape, whole3),
            pl.BlockSpec((1, S5_WIDTH), whole2),
            pl.BlockSpec((S5_WIDTH, S5_WIDTH), whole2),
            pl.BlockSpec((1, S5_WIDTH), whole2),
        ],
        out_specs=[pl.BlockSpec((tr, S5_WIDTH), out_map), st_spec, st_spec],
        out_shape=[jax.ShapeDtypeStruct((rows, S5_WIDTH), F32), st_shape, st_shape],
        scratch_shapes=[pltpu.VMEM((tr, S5_LANES), F32), pltpu.VMEM((tr, S5_LANES), F32),
                        pltpu.VMEM((1, S5_LANES), F32), pltpu.VMEM((1, S5_LANES), F32)],
        compiler_params=_cparams(("parallel", "arbitrary")),
        name="s5_chain" if chained else "s5_tile",
    )(p3, x0r, x0i, cs["b3r"], cs["b3i"], cs["cr"], cs["ci"], cs["scan"], cs["d"], cs["wglu"], cs["bglu"])
    return y, xlr.reshape(nseq, S5_GROUPS, S5_STATE), xli.reshape(nseq, S5_GROUPS, S5_STATE)


def _s5_consts(a_re, a_im, log_dt, b_re, b_im, c_re, c_im, d, w_glu, b_glu):
    g, p, h = S5_GROUPS, S5_STATE, S5_GROUP
    lam = lax.complex(a_re, a_im)
    dt = jnp.exp(log_dt)[:, None]
    a_bar = jnp.exp(lam * dt)
    b_bar = ((a_bar - 1.0) / lam)[..., None] * lax.complex(b_re, b_im)
    pw = [a_bar]
    for _ in range(SUBLANES - 1):
        pw.append(pw[-1] * a_bar)
    pw = jnp.stack(pw).reshape(SUBLANES, g * p)
    row = jnp.arange(SUBLANES)[:, None]

    def lvl(k):
        ak = jnp.where(row >= k, pw[k - 1][None, :], 0.0)
        return [jnp.real(ak), jnp.imag(ak)]

    scan = jnp.stack(lvl(1) + lvl(2) + lvl(4) + [jnp.real(pw), jnp.imag(pw)]).astype(F32)
    eye = jnp.eye(SUBLANES, dtype=F32)
    bb = b_bar.reshape(S5_GB, SUBLANES, p, h).transpose(0, 1, 3, 2)

    def in_blocks(x):
        blk = (x[:, :, :, None, :] * eye[None, :, None, :, None]).reshape(S5_GB, SUBLANES * h, SUBLANES * p)
        hi = blk.astype(BF16)
        lo = (blk - hi.astype(F32)).astype(BF16)
        return jnp.concatenate([hi, hi, lo], axis=1)

    def out_blocks(x):
        xx = x.reshape(S5_GB, SUBLANES, h, p).transpose(0, 1, 3, 2)
        return (xx[:, :, :, None, :] * eye[None, :, None, :, None]).reshape(
            S5_GB, SUBLANES * p, SUBLANES * h).astype(BF16)

    return dict(b3r=in_blocks(jnp.real(bb)), b3i=in_blocks(jnp.imag(bb)),
                cr=out_blocks(c_re), ci=out_blocks(-c_im), scan=scan,
                d=d.reshape(1, -1), wglu=w_glu.astype(BF16), bglu=b_glu.reshape(1, -1))


def _seg_sum(x, e2):
    hi, lo = _split2(x)
    return _dot(jnp.concatenate([hi, lo], axis=1), e2)


def _rwkv_prep_kernel(z_ref, zp_ref, mu_ref, w0_ref, a0_ref, w2_ref, a2_ref, g2_ref, kk_ref, ka_ref, rk_ref,
                      e2_ref, r_o, k_o, v_o, d_o, kk_o, b_o, g_o, bonus_o):
    z = z_ref[...]
    zm = z + (zp_ref[...] - z) * mu_ref[...]
    w = RWKV_WIDTH
    r = zm[:, 0:w]
    k = zm[:, w:2 * w]
    v = zm[:, 2 * w:3 * w]
    wa = zm[:, 3 * w:3 * w + LANES]
    gi = zm[:, 3 * w + LANES:3 * w + 2 * LANES]
    e2 = e2_ref[...]
    wl = w0_ref[...] + _dot(jnp.tanh(wa).astype(BF16), w2_ref[...])
    sp = jnp.maximum(-wl, 0.0) + jnp.log(1.0 + jnp.exp(-jnp.abs(wl)))
    dec = jnp.exp(-jnp.exp(-sp - 0.5))
    a = _sigmoid(a0_ref[...] + _dot(wa.astype(BF16), a2_ref[...]))
    g = _dot(_sigmoid(gi).astype(BF16), g2_ref[...])
    kk = k * kk_ref[...]
    kk = kk * lax.rsqrt(jnp.maximum(_seg_sum(kk * kk, e2), 1e-24))
    k2 = k * (1.0 + (a - 1.0) * ka_ref[...])
    r_o[...] = r
    k_o[...] = k2
    v_o[...] = v
    d_o[...] = dec
    kk_o[...] = kk
    b_o[...] = kk * a
    g_o[...] = g
    bonus_o[...] = _seg_sum(r * k2 * rk_ref[...], e2) * v


def _rwkv_prep(p3, zprev, cs, tm=256):
    m = p3.shape[0]
    w = RWKV_WIDTH
    row = lambda i: (i, 0)
    whole = lambda i: (0, 0)
    vec = pl.BlockSpec((1, w), whole)
    outs = pl.pallas_call(
        _rwkv_prep_kernel,
        grid=(m // tm,),
        in_specs=[
            pl.BlockSpec((tm, RWKV_SHIFT_WIDTH), row),
            pl.BlockSpec((tm, RWKV_SHIFT_WIDTH), row),
            pl.BlockSpec((1, RWKV_SHIFT_WIDTH), whole),
            vec, vec,
            pl.BlockSpec((LANES, w), whole), pl.BlockSpec((LANES, w), whole), pl.BlockSpec((LANES, w), whole),
            vec, vec, vec,
            pl.BlockSpec((2 * w, w), whole),
        ],
        out_specs=[pl.BlockSpec((tm, w), row)] * 8,
        out_shape=[jax.ShapeDtypeStruct((m, w), F32)] * 8,
        compiler_params=_cparams(("parallel",)),
        name="rwkv_prep",
    )(p3, zprev, cs["mu"], cs["w0"], cs["a0"], cs["w2"], cs["a2"], cs["g2"], cs["k_k"], cs["k_a"], cs["r_k"],
      cs["e2w"])
    return outs


def _rwkv_rec_kernel(r_ref, k_ref, v_ref, d_ref, kk_ref, b_ref, s0_ref, e2_ref, jm_ref, y_ref, so_ref, s_ref,
                     *, nb, tc):
    c = pl.program_id(1)
    npair = nb * RWKV_PAIRS
    hd = RWKV_HEAD

    @pl.when(c == 0)
    def _():
        for n in range(nb):
            for j in range(RWKV_PAIRS):
                s_ref[n * RWKV_PAIRS + j] = jnp.concatenate([s0_ref[n, 2 * j], s0_ref[n, 2 * j + 1]], axis=1)

    e2 = e2_ref[...]
    jm = jm_ref[...]

    def seg(x):
        hi, lo = _split2(x)
        return _dot(jnp.concatenate([hi, lo], axis=1), e2)

    pairs = [(n, j) for n in range(nb) for j in range(RWKV_PAIRS)]

    def group(gi, carry):
        rows = pl.ds(pl.multiple_of(gi * SUBLANES, SUBLANES), SUBLANES)

        def tiles(ref):
            return [ref[n, rows, j * LANES:(j + 1) * LANES] for n, j in pairs]

        r8, k8, v8, d8, kk8, b8 = (tiles(ref) for ref in (r_ref, k_ref, v_ref, d_ref, kk_ref, b_ref))
        ys = [[] for _ in pairs]
        for i in range(SUBLANES):
            def row(tl, idx):
                return jnp.broadcast_to(tl[idx][i:i + 1, :], (hd, LANES))

            prod = [s_ref[idx] * row(kk8, idx) for idx in range(npair)]
            sa = seg(jnp.concatenate(prod, axis=0))
            vm = [row(v8, idx) * jm for idx in range(npair)]
            vb = seg(jnp.concatenate(vm, axis=0))
            zs = []
            for idx in range(npair):
                rs = slice(idx * hd, (idx + 1) * hd)
                s = s_ref[idx] * row(d8, idx) - sa[rs] * row(b8, idx) + vb[rs] * row(k8, idx)
                s_ref[idx] = s
                zs.append(s * row(r8, idx))
            yb = seg(jnp.concatenate(zs, axis=0))
            for idx in range(npair):
                rs = slice(idx * hd, (idx + 1) * hd)
                ys[idx].append(jnp.sum(yb[rs] * jm, axis=0, keepdims=True))
        for idx, (n, j) in enumerate(pairs):
            y_ref[n, rows, j * LANES:(j + 1) * LANES] = jnp.concatenate(ys[idx], axis=0)
        return carry

    lax.fori_loop(0, tc // SUBLANES, group, 0)

    @pl.when(c == pl.num_programs(1) - 1)
    def _():
        for n in range(nb):
            for j in range(RWKV_PAIRS):
                s = s_ref[n * RWKV_PAIRS + j]
                so_ref[n, 2 * j] = s[:, :hd]
                so_ref[n, 2 * j + 1] = s[:, hd:]


def _rwkv_rec(r, k, v, d, kk, b, s0, cs, nb=2, tc=256):
    nseq, t, w = r.shape
    tc = min(tc, t)
    seq_map = lambda i, c: (i, c, 0)
    st_map = lambda i, c: (i, 0, 0, 0)
    whole = lambda i, c: (0, 0)
    blk = pl.BlockSpec((nb, tc, w), seq_map)
    st = pl.BlockSpec((nb, RWKV_HEADS, RWKV_HEAD, RWKV_HEAD), st_map)
    return pl.pallas_call(
        functools.partial(_rwkv_rec_kernel, nb=nb, tc=tc),
        grid=(nseq // nb, t // tc),
        in_specs=[blk] * 6 + [st, pl.BlockSpec((2 * LANES, LANES), whole), pl.BlockSpec((RWKV_HEAD, LANES), whole)],
        out_specs=[blk, st],
        out_shape=[jax.ShapeDtypeStruct((nseq, t, w), F32),
                   jax.ShapeDtypeStruct((nseq, RWKV_HEADS, RWKV_HEAD, RWKV_HEAD), F32)],
        scratch_shapes=[pltpu.VMEM((nb * RWKV_PAIRS, RWKV_HEAD, LANES), F32)],
        compiler_params=_cparams(("parallel", "arbitrary")),
        name="rwkv_rec",
    )(r, k, v, d, kk, b, s0, cs["e2"], cs["jm"])


def _rwkv_post_kernel(y_ref, bonus_ref, g_ref, lnw_ref, lnb_ref, e2_ref, o_ref):
    y = y_ref[...]
    e2 = e2_ref[...]
    mean = _seg_sum(y, e2) * (1.0 / RWKV_HEAD)
    yc = y - mean
    var = _seg_sum(yc * yc, e2) * (1.0 / RWKV_HEAD)
    yn = yc * lax.rsqrt(var + RWKV_GN_EPS) * lnw_ref[...] + lnb_ref[...]
    o_ref[...] = (yn + bonus_ref[...]) * g_ref[...]


def _rwkv_post(y, bonus, g, cs, tm=512):
    m, w = y.shape
    row = lambda i: (i, 0)
    whole = lambda i: (0, 0)
    return pl.pallas_call(
        _rwkv_post_kernel,
        grid=(m // tm,),
        in_specs=[pl.BlockSpec((tm, w), row)] * 3 + [pl.BlockSpec((1, w), whole)] * 2
        + [pl.BlockSpec((2 * w, w), whole)],
        out_specs=pl.BlockSpec((tm, w), row),
        out_shape=jax.ShapeDtypeStruct((m, w), F32),
        compiler_params=_cparams(("parallel",)),
        name="rwkv_post",
    )(y, bonus, g, cs["ln_w"], cs["ln_b"], cs["e2w"])


def _rwkv_consts(mu, w0, w2, a0, a2, g2, k_k, k_a, r_k, ln_w, ln_b):
    w = RWKV_WIDTH
    seg = np.arange(w) // RWKV_HEAD
    e_w = (seg[:, None] == seg[None, :]).astype(np.float32)
    seg = np.arange(LANES) // RWKV_HEAD
    e_p = (seg[:, None] == seg[None, :]).astype(np.float32)
    jm = (np.arange(LANES)[None, :] % RWKV_HEAD == np.arange(RWKV_HEAD)[:, None]).astype(np.float32)
    z64 = jnp.zeros((RWKV_HEAD, w), F32)
    return dict(
        mu=mu.reshape(1, -1), w0=w0.reshape(1, -1), a0=a0.reshape(1, -1),
        w2=jnp.concatenate([w2, z64], axis=0).astype(BF16),
        a2=jnp.concatenate([z64, a2], axis=0).astype(BF16),
        g2=g2.astype(BF16), k_k=k_k.reshape(1, -1), k_a=k_a.reshape(1, -1), r_k=r_k.reshape(1, -1),
        ln_w=ln_w.reshape(1, -1), ln_b=ln_b.reshape(1, -1),
        e2w=jnp.asarray(np.concatenate([e_w, e_w], axis=0), BF16),
        e2=jnp.asarray(np.concatenate([e_p, e_p], axis=0), BF16),
        jm=jnp.asarray(jm, F32))


def _hgrn_tables(c):
    t = np.arange(c)
    j = np.arange(c)
    mats = [(j[None, :] <= t[:, None]), (j[None, :] > t[:, None])]
    masks = []
    n = 2
    while n <= c:
        m = n // 2
        bs = (t // n) * n
        hi_half = (t % n) >= m
        mats.append(hi_half[:, None] & (j[None, :] >= (bs + m)[:, None]) & (j[None, :] <= t[:, None]))
        mats.append((~hi_half)[:, None] & (j[None, :] > t[:, None]) & (j[None, :] <= (bs + m - 1)[:, None]))
        masks.append(((t[:, None] // n) == (t[None, :] // n)) & hi_half[:, None] & (~hi_half)[None, :])
        n *= 2
    w = np.concatenate(mats, axis=0).astype(np.float32)
    w3 = np.concatenate([w, w, w], axis=1)
    return jnp.asarray(w3, BF16), jnp.asarray(np.stack(masks).astype(np.float32)), len(masks)


def _hgrn_kernel(q_ref, f_ref, i_ref, g_ref, s0_ref, lb_ref, nw_ref, w3_ref, mk_ref, o_ref, so_ref, st_ref,
                 *, rb, c, nlev, chained):
    cidx = pl.program_id(1)
    hdim = HGRN_DIM

    if chained:
        @pl.when(cidx == 0)
        def _():
            for h in range(HGRN_HEADS):
                st_ref[h] = s0_ref[0, h].T

    w3 = w3_ref[...]
    row_i = lax.broadcasted_iota(jnp.int32, (c, c), 0)
    col_i = lax.broadcasted_iota(jnp.int32, (c, c), 1)
    eye = (row_i == col_i).astype(F32)

    def block(i, carry):
        rows = pl.ds(pl.multiple_of(i * c, c), c)
        for h in range(HGRN_HEADS):
            ls = slice(h * hdim, (h + 1) * hdim)
            lb = lb_ref[:, ls]
            fg = lb + (1.0 - lb) * _sigmoid(f_ref[rows, ls])
            lf = jnp.log(fg)
            kk = 1.0 - fg
            q = q_ref[rows, ls]
            qs = q * _sigmoid(q)
            v = i_ref[rows, ls]
            p1, p2, p3 = _split3(lf)
            ex = jnp.exp(_dot(w3, jnp.concatenate([p1, p2, p3], axis=0)))
            eb = ex[0:c]
            ek = ex[c:2 * c]
            if chained:
                st = st_ref[h]
            else:
                st = s0_ref[i, h].T
            o = _dot_nt((qs * eb).astype(BF16), st.astype(BF16))
            att = eye * jnp.sum(qs * kk, axis=-1, keepdims=True)
            for l in range(nlev):
                qe = (qs * ex[(2 + 2 * l) * c:(3 + 2 * l) * c]).astype(BF16)
                ke = (kk * ex[(3 + 2 * l) * c:(4 + 2 * l) * c]).astype(BF16)
                att = att + jnp.where(mk_ref[l] > 0.0, _dot_nt(qe, ke), 0.0)
            vb = v.astype(BF16)
            o = o + _dot(att.astype(BF16), vb)
            st = st * eb[c - 1:c, :] + _dot_tn(vb, (kk * ek).astype(BF16))
            if chained:
                st_ref[h] = st
            else:
                so_ref[i, h] = st.T
            o = o * lax.rsqrt(jnp.mean(o * o, axis=-1, keepdims=True) + HGRN_EPS)
            gg = g_ref[rows, ls]
            o_ref[rows, ls] = o * nw_ref[:, ls] * (gg * _sigmoid(gg))
        return carry

    lax.fori_loop(0, rb // c, block, 0)

    if chained:
        @pl.when(cidx == pl.num_programs(1) - 1)
        def _():
            for h in range(HGRN_HEADS):
                so_ref[0, h] = st_ref[h].T


def _hgrn(p3, s0, lb, nw, nseq, t):
    rows = nseq * t
    c = math.gcd(t, 64)
    chained = t > c
    w3, masks, nlev = _hgrn_tables(c)
    w = HGRN_WIDTH
    if chained:
        rb = 256
        nblk = t // rb
        grid = (nseq, nblk)
        col = lambda cb: (lambda n, k: (n * nblk + k, cb))
        out_map = lambda n, k: (n * nblk + k, 0)
        st_spec = pl.BlockSpec((1, HGRN_HEADS, HGRN_DIM, HGRN_DIM), lambda n, k: (n, 0, 0, 0))
    else:
        nsb = 16
        rb = nsb * c
        grid = (rows // rb, 1)
        col = lambda cb: (lambda n, k: (n, cb))
        out_map = lambda n, k: (n, 0)
        st_spec = pl.BlockSpec((nsb, HGRN_HEADS, HGRN_DIM, HGRN_DIM), lambda n, k: (n, 0, 0, 0))
    whole2 = lambda n, k: (0, 0)
    whole3 = lambda n, k: (0, 0, 0)
    return pl.pallas_call(
        functools.partial(_hgrn_kernel, rb=rb, c=c, nlev=nlev, chained=chained),
        grid=grid,
        in_specs=[pl.BlockSpec((rb, w), col(4)), pl.BlockSpec((rb, w), col(5)), pl.BlockSpec((rb, w), col(6)),
                  pl.BlockSpec((rb, w), col(7)), st_spec,
                  pl.BlockSpec((1, w), whole2), pl.BlockSpec((1, w), whole2),
                  pl.BlockSpec(w3.shape, whole2), pl.BlockSpec(masks.shape, whole3)],
        out_specs=[pl.BlockSpec((rb, w), out_map), st_spec],
        out_shape=[jax.ShapeDtypeStruct((rows, w), F32),
                   jax.ShapeDtypeStruct((nseq, HGRN_HEADS, HGRN_DIM, HGRN_DIM), F32)],
        scratch_shapes=[pltpu.VMEM((HGRN_HEADS, HGRN_DIM, HGRN_DIM), F32)],
        compiler_params=_cparams(("parallel", "arbitrary")),
        name="hgrn_chain" if chained else "hgrn_tile",
    )(p3, p3, p3, p3, s0, lb, nw, w3, masks)


def _run(x, nseq, t, states, lw, norm_final):
    s5_re0, s5_im0, shift0, wkv0, hgrn0 = states
    outs = ([], [], [], [], [])
    depth = len(lw)
    for l in range(depth):
        w = lw[l]
        x = _ffn(x, w["norm_ffn1"], w["f1g"], w["f1u"], w["f1d"])
        p3 = _inproj(x, w["norm_mix"], w["w_in"])
        y_s5, s_re, s_im = _s5(p3, s5_re0[l].reshape(nseq, S5_LANES), s5_im0[l].reshape(nseq, S5_LANES),
                               w["s5"], nseq, t)
        z3 = p3.reshape(nseq, t, N_IN)[:, :, :RWKV_SHIFT_WIDTH]
        zprev = jnp.concatenate([shift0[l][:, None, :], z3[:, :-1]], axis=1).reshape(nseq * t, RWKV_SHIFT_WIDTH)
        r, k2, v, dec, kk, b, g, bonus = _rwkv_prep(p3, zprev, w["rwkv"])
        sh3 = lambda a: a.reshape(nseq, t, RWKV_WIDTH)
        y_rw, wkv = _rwkv_rec(sh3(r), sh3(k2), sh3(v), sh3(dec), sh3(kk), sh3(b), wkv0[l], w["rwkv"])
        y_rw = _rwkv_post(y_rw.reshape(nseq * t, RWKV_WIDTH), bonus, g, w["rwkv"])
        y_hg, hs = _hgrn(p3, hgrn0[l], w["hgrn_lb"], w["hgrn_nw"], nseq, t)
        x = _outproj(x, y_s5, y_rw, y_hg, w["wo_s5"], w["wo_rw"], w["wo_hg"])
        x = _ffn(x, w["norm_ffn2"], w["f2g"], w["f2u"], w["f2d"])
        outs[0].append(s_re)
        outs[1].append(s_im)
        outs[2].append(z3[:, -1])
        outs[3].append(wkv)
        outs[4].append(hs)
    y = _rmsnorm(x, norm_final.reshape(1, -1))
    return (y,) + tuple(jnp.stack(o) for o in outs)


def kernel(x_prompt, x_sample, state_s5_re, state_s5_im, state_rwkv_shift, state_rwkv_wkv, state_hgrn, norm_ffn1, ffn1_w_gate, ffn1_w_up, ffn1_w_down, norm_mix, w_in, s5_a_re, s5_a_im, s5_log_dt, s5_b_re, s5_b_im, s5_c_re, s5_c_im, s5_d, s5_w_glu, s5_b_glu, rwkv_mu, rwkv_w0, rwkv_w2, rwkv_a0, rwkv_a2, rwkv_g2, rwkv_k_k, rwkv_k_a, rwkv_r_k, rwkv_ln_w, rwkv_ln_b, hgrn_lb_raw, hgrn_norm_w, w_out, norm_ffn2, ffn2_w_gate, ffn2_w_up, ffn2_w_down, norm_final):
    depth = w_in.shape[0]
    nb, seq, d = x_prompt.shape
    ns, dseq, _ = x_sample.shape

    p_lb = jax.nn.softmax(hgrn_lb_raw.astype(F32), axis=0)
    lower_bounds = jnp.cumsum(p_lb, axis=0) - p_lb[0]

    o_rw = S5_WIDTH
    o_hg = S5_WIDTH + RWKV_SHIFT_WIDTH
    lw = []
    for l in range(depth):
        wi = w_in[l]
        wo = w_out[l].astype(BF16)
        lw.append(dict(
            norm_ffn1=norm_ffn1[l].reshape(1, -1), norm_mix=norm_mix[l].reshape(1, -1),
            norm_ffn2=norm_ffn2[l].reshape(1, -1),
            f1g=ffn1_w_gate[l].astype(BF16), f1u=ffn1_w_up[l].astype(BF16), f1d=ffn1_w_down[l].astype(BF16),
            f2g=ffn2_w_gate[l].astype(BF16), f2u=ffn2_w_up[l].astype(BF16), f2d=ffn2_w_down[l].astype(BF16),
            w_in=jnp.concatenate([wi[:, o_rw:o_hg], wi[:, :o_rw], wi[:, o_hg:]], axis=1).astype(BF16),
            wo_s5=wo[:S5_WIDTH], wo_rw=wo[S5_WIDTH:S5_WIDTH + RWKV_WIDTH], wo_hg=wo[S5_WIDTH + RWKV_WIDTH:],
            s5=_s5_consts(s5_a_re[l], s5_a_im[l], s5_log_dt[l], s5_b_re[l], s5_b_im[l], s5_c_re[l], s5_c_im[l],
                          s5_d[l], s5_w_glu[l], s5_b_glu[l]),
            rwkv=_rwkv_consts(rwkv_mu[l], rwkv_w0[l], rwkv_w2[l], rwkv_a0[l], rwkv_a2[l], rwkv_g2[l],
                              rwkv_k_k[l], rwkv_k_a[l], rwkv_r_k[l], rwkv_ln_w[l], rwkv_ln_b[l]),
            hgrn_lb=lower_bounds[l].reshape(1, -1), hgrn_nw=hgrn_norm_w[l].reshape(1, -1),
        ))

    def zeros_like_state(s):
        return jnp.zeros((depth, nb) + s.shape[2:], F32)

    p_states = tuple(zeros_like_state(s) for s in
                     (state_s5_re, state_s5_im, state_rwkv_shift, state_rwkv_wkv, state_hgrn))
    s_states = (state_s5_re, state_s5_im, state_rwkv_shift, state_rwkv_wkv, state_hgrn)
    y_p, s5re_p, s5im_p, shift_p, wkv_p, hgrn_p = _run(x_prompt.reshape(nb * seq, d), nb, seq, p_states, lw, norm_final)
    y_s, s5re_s, s5im_s, shift_s, wkv_s, hgrn_s = _run(x_sample.reshape(ns * dseq, d), ns, dseq, s_states, lw, norm_final)
    return (y_p.reshape(nb, seq, d), y_s.reshape(ns, dseq, d), s5re_p, s5im_p, shift_p, wkv_p, hgrn_p,
            s5re_s, s5im_s, shift_s, wkv_s, hgrn_s)
```

```python
import functools
import math

import numpy as np
import jax
import jax.numpy as jnp
from jax import lax
from jax.experimental import pallas as pl
from jax.experimental.pallas import tpu as pltpu

F32 = jnp.float32
BF16 = jnp.bfloat16

NORM_EPS = 1e-6
RWKV_GN_EPS = 64e-5
HGRN_EPS = 1e-5

D_MODEL = 2048
S5_WIDTH = 512
S5_GROUP = 16
S5_GROUPS = 32
S5_STATE = 64
S5_LANES = S5_GROUPS * S5_STATE
S5_GB = 4
RWKV_WIDTH = 768
RWKV_HEAD = 64
RWKV_HEADS = 12
RWKV_PAIRS = 6
RWKV_SHIFT_WIDTH = 2560
RWKV_BLOCK = 16
RWKV_TM = 128
HGRN_WIDTH = 768
HGRN_HEADS = 6
HGRN_DIM = 128
N_IN = 6144

SUBLANES = 8
LANES = 128
VMEM_LIMIT = 56 * 1024 * 1024


def _cparams(sem):
    return pltpu.CompilerParams(dimension_semantics=sem, vmem_limit_bytes=VMEM_LIMIT)


def _split2(x):
    hi = x.astype(BF16)
    lo = (x - hi.astype(F32)).astype(BF16)
    return hi, lo


def _split3(x):
    p1 = x.astype(BF16)
    r1 = x - p1.astype(F32)
    p2 = r1.astype(BF16)
    p3 = (r1 - p2.astype(F32)).astype(BF16)
    return p1, p2, p3


def _dot(a, b):
    return jnp.dot(a, b, preferred_element_type=F32)


def _dot_nt(a, b):
    return lax.dot_general(a, b, (((1,), (1,)), ((), ())), preferred_element_type=F32)


def _dot_tn(a, b):
    return lax.dot_general(a, b, (((0,), (0,)), ((), ())), preferred_element_type=F32)


def _dot3(a, b):
    ah, al = _split2(a)
    bh, bl = _split2(b)
    return _dot(jnp.concatenate([ah, ah, al], axis=1), jnp.concatenate([bh, bl, bh], axis=0))


def _dot3s(asp, bsp):
    ah, al = asp
    bh, bl = bsp
    return _dot(jnp.concatenate([ah, ah, al], axis=1), jnp.concatenate([bh, bl, bh], axis=0))


def _dot3_nt(a, b):
    ah, al = _split2(a)
    bh, bl = _split2(b)
    return _dot_nt(jnp.concatenate([ah, ah, al], axis=1), jnp.concatenate([bh, bl, bh], axis=1))


def _sigmoid(x):
    return 1.0 / (1.0 + jnp.exp(-x))


def _ffn_kernel(x_ref, nw_ref, wg_ref, wu_ref, wd_ref, o_ref, h_ref, acc_ref):
    j = pl.program_id(1)

    @pl.when(j == 0)
    def _():
        x = x_ref[...]
        ms = jnp.mean(x * x, axis=-1, keepdims=True)
        h_ref[...] = (x * lax.rsqrt(ms + NORM_EPS) * nw_ref[...]).astype(BF16)
        acc_ref[...] = jnp.zeros_like(acc_ref)

    h = h_ref[...]
    g = _dot(h, wg_ref[...])
    u = _dot(h, wu_ref[...])
    a = (g * _sigmoid(g) * u).astype(BF16)
    acc_ref[...] += _dot(a, wd_ref[...])

    @pl.when(j == pl.num_programs(1) - 1)
    def _():
        o_ref[...] = x_ref[...] + 0.5 * acc_ref[...]


def _ffn(x, nw, wg, wu, wd, tm=512, tf=512):
    m, d = x.shape
    ff = wg.shape[1]
    return pl.pallas_call(
        _ffn_kernel,
        grid=(m // tm, ff // tf),
        in_specs=[
            pl.BlockSpec((tm, d), lambda i, j: (i, 0)),
            pl.BlockSpec((1, d), lambda i, j: (0, 0)),
            pl.BlockSpec((d, tf), lambda i, j: (0, j)),
            pl.BlockSpec((d, tf), lambda i, j: (0, j)),
            pl.BlockSpec((tf, d), lambda i, j: (j, 0)),
        ],
        out_specs=pl.BlockSpec((tm, d), lambda i, j: (i, 0)),
        out_shape=jax.ShapeDtypeStruct((m, d), F32),
        scratch_shapes=[pltpu.VMEM((tm, d), BF16), pltpu.VMEM((tm, d), F32)],
        compiler_params=_cparams(("parallel", "arbitrary")),
        name="ffn",
    )(x, nw, wg, wu, wd)


def _inproj_kernel(x_ref, nw_ref, w_ref, o_ref, h_ref):
    @pl.when(pl.program_id(1) == 0)
    def _():
        x = x_ref[...]
        ms = jnp.mean(x * x, axis=-1, keepdims=True)
        h_ref[...] = (x * lax.rsqrt(ms + NORM_EPS) * nw_ref[...]).astype(BF16)

    o_ref[...] = _dot(h_ref[...], w_ref[...])


def _inproj(x, nw, w, tm=512, tn=1024):
    m, d = x.shape
    n = w.shape[1]
    return pl.pallas_call(
        _inproj_kernel,
        grid=(m // tm, n // tn),
        in_specs=[
            pl.BlockSpec((tm, d), lambda i, j: (i, 0)),
            pl.BlockSpec((1, d), lambda i, j: (0, 0)),
            pl.BlockSpec((d, tn), lambda i, j: (0, j)),
        ],
        out_specs=pl.BlockSpec((tm, tn), lambda i, j: (i, j)),
        out_shape=jax.ShapeDtypeStruct((m, n), F32),
        scratch_shapes=[pltpu.VMEM((tm, d), BF16)],
        compiler_params=_cparams(("parallel", "arbitrary")),
        name="inproj",
    )(x, nw, w)


def _outproj_kernel(x_ref, a_ref, b_ref, c_ref, wa_ref, wb_ref, wc_ref, o_ref):
    acc = _dot(a_ref[...].astype(BF16), wa_ref[...])
    acc += _dot(b_ref[...].astype(BF16), wb_ref[...])
    acc += _dot(c_ref[...].astype(BF16), wc_ref[...])
    o_ref[...] = x_ref[...] + acc


def _outproj(x, ya, yb, yc, wa, wb, wc, tm=512):
    m, d = x.shape
    row = lambda i: (i, 0)
    whole = lambda i: (0, 0)
    return pl.pallas_call(
        _outproj_kernel,
        grid=(m // tm,),
        in_specs=[
            pl.BlockSpec((tm, d), row),
            pl.BlockSpec((tm, ya.shape[1]), row),
            pl.BlockSpec((tm, yb.shape[1]), row),
            pl.BlockSpec((tm, yc.shape[1]), row),
            pl.BlockSpec(wa.shape, whole),
            pl.BlockSpec(wb.shape, whole),
            pl.BlockSpec(wc.shape, whole),
        ],
        out_specs=pl.BlockSpec((tm, d), row),
        out_shape=jax.ShapeDtypeStruct((m, d), F32),
        compiler_params=_cparams(("parallel",)),
        name="outproj",
    )(x, ya, yb, yc, wa, wb, wc)


def _rmsnorm_kernel(x_ref, nw_ref, o_ref):
    x = x_ref[...]
    ms = jnp.mean(x * x, axis=-1, keepdims=True)
    o_ref[...] = x * lax.rsqrt(ms + NORM_EPS) * nw_ref[...]


def _rmsnorm(x, nw, tm=512):
    m, d = x.shape
    return pl.pallas_call(
        _rmsnorm_kernel,
        grid=(m // tm,),
        in_specs=[pl.BlockSpec((tm, d), lambda i: (i, 0)), pl.BlockSpec((1, d), lambda i: (0, 0))],
        out_specs=pl.BlockSpec((tm, d), lambda i: (i, 0)),
        out_shape=jax.ShapeDtypeStruct((m, d), F32),
        compiler_params=_cparams(("parallel",)),
        name="final_norm",
    )(x, nw)


def _s5_kernel(u_ref, x0r_ref, x0i_ref, b3r_ref, b3i_ref, cr_ref, ci_ref, cst_ref, d_ref, wglu_ref,
               bglu_ref, y_ref, xlr_ref, xli_ref, xr_ref, xi_ref, cr_s, ci_s, *, tr, chained):
    c = pl.program_id(1)
    u = u_ref[...]
    uh, ul = _split2(u)
    for gb in range(S5_GB):
        sl = slice(gb * LANES, (gb + 1) * LANES)
        lhs = jnp.concatenate([uh[:, sl], ul[:, sl], uh[:, sl]], axis=1)
        osl = slice(gb * 512, (gb + 1) * 512)
        xr_ref[:, osl] = _dot(lhs, b3r_ref[gb])
        xi_ref[:, osl] = _dot(lhs, b3i_ref[gb])

    if chained:
        @pl.when(c == 0)
        def _():
            cr_s[...] = x0r_ref[0]
            ci_s[...] = x0i_ref[0]

    a1r, a1i, a2r, a2i, a4r, a4i, pr, pi = [cst_ref[i] for i in range(8)]

    def tile(j, carry):
        rows = pl.ds(pl.multiple_of(j * SUBLANES, SUBLANES), SUBLANES)
        xr = xr_ref[rows, :]
        xi = xi_ref[rows, :]
        for k, ar, ai in ((1, a1r, a1i), (2, a2r, a2i), (4, a4r, a4i)):
            sr = pltpu.roll(xr, k, 0)
            si = pltpu.roll(xi, k, 0)
            xr, xi = xr + ar * sr - ai * si, xi + ar * si + ai * sr
        if chained:
            x0r = cr_s[...]
            x0i = ci_s[...]
        else:
            x0r = x0r_ref[pl.ds(j, 1), :]
            x0i = x0i_ref[pl.ds(j, 1), :]
        x0r = jnp.broadcast_to(x0r, xr.shape)
        x0i = jnp.broadcast_to(x0i, xr.shape)
        xr, xi = xr + pr * x0r - pi * x0i, xi + pr * x0i + pi * x0r
        xr_ref[rows, :] = xr
        xi_ref[rows, :] = xi
        if chained:
            cr_s[...] = xr[SUBLANES - 1:SUBLANES, :]
            ci_s[...] = xi[SUBLANES - 1:SUBLANES, :]
        else:
            xlr_ref[pl.ds(j, 1), :] = xr[SUBLANES - 1:SUBLANES, :]
            xli_ref[pl.ds(j, 1), :] = xi[SUBLANES - 1:SUBLANES, :]
        return carry

    lax.fori_loop(0, tr // SUBLANES, tile, 0)

    if chained:
        @pl.when(c == pl.num_programs(1) - 1)
        def _():
            xlr_ref[0] = cr_s[...]
            xli_ref[0] = ci_s[...]

    ys = []
    for gb in range(S5_GB):
        osl = slice(gb * 512, (gb + 1) * 512)
        ys.append(_dot(xr_ref[:, osl].astype(BF16), cr_ref[gb]) + _dot(xi_ref[:, osl].astype(BF16), ci_ref[gb]))
    y = jnp.concatenate(ys, axis=1) + d_ref[...] * u
    y = 0.5 * y * (1.0 + jnp.tanh(math.sqrt(2.0 / math.pi) * (y + 0.044715 * (y * y * y))))
    z = _dot(y.astype(BF16), wglu_ref[...]) + bglu_ref[...]
    y_ref[...] = y * _sigmoid(z)


def _s5(p3, x0r, x0i, l, cs, nseq, t, tr=512):
    rows = nseq * t
    depth = x0r.shape[0]
    chained = t > SUBLANES
    if chained:
        grid = (nseq, t // tr)
        row_map = lambda n, c: (n * (t // tr) + c, 5)
        out_map = lambda n, c: (n * (t // tr) + c, 0)
        x0r = x0r.reshape(depth, nseq, 1, S5_LANES)
        x0i = x0i.reshape(depth, nseq, 1, S5_LANES)
        in_st = pl.BlockSpec((None, 1, 1, S5_LANES), lambda n, c: (l, n, 0, 0))
        st_spec = pl.BlockSpec((1, 1, S5_LANES), lambda n, c: (n, 0, 0))
        st_shape = jax.ShapeDtypeStruct((nseq, 1, S5_LANES), F32)
    else:
        assert t == SUBLANES
        grid = (rows // tr, 1)
        row_map = lambda n, c: (n, 5)
        out_map = lambda n, c: (n, 0)
        in_st = pl.BlockSpec((None, tr // SUBLANES, S5_LANES), lambda n, c: (l, n, 0))
        st_spec = pl.BlockSpec((tr // SUBLANES, S5_LANES), lambda n, c: (n, 0))
        st_shape = jax.ShapeDtypeStruct((nseq, S5_LANES), F32)
    whole3 = lambda n, c: (0, 0, 0)
    whole2 = lambda n, c: (0, 0)
    y, xlr, xli = pl.pallas_call(
        functools.partial(_s5_kernel, tr=tr, chained=chained),
        grid=grid,
        in_specs=[
            pl.BlockSpec((tr, S5_WIDTH), row_map),
            in_st, in_st,
            pl.BlockSpec(cs["b3r"].shape, whole3),
            pl.BlockSpec(cs["b3i"].shape, whole3),
            pl.BlockSpec(cs["cr"].shape, whole3),
            pl.BlockSpec(cs["ci"].shape, whole3),
            pl.BlockSpec(cs["scan"].shape, whole3),
            pl.BlockSpec((1, S5_WIDTH), whole2),
            pl.BlockSpec((S5_WIDTH, S5_WIDTH), whole2),
            pl.BlockSpec((1, S5_WIDTH), whole2),
        ],
        out_specs=[pl.BlockSpec((tr, S5_WIDTH), out_map), st_spec, st_spec],
        out_shape=[jax.ShapeDtypeStruct((rows, S5_WIDTH), F32), st_shape, st_shape],
        scratch_shapes=[pltpu.VMEM((tr, S5_LANES), F32), pltpu.VMEM((tr, S5_LANES), F32),
                        pltpu.VMEM((1, S5_LANES), F32), pltpu.VMEM((1, S5_LANES), F32)],
        compiler_params=_cparams(("parallel", "arbitrary")),
        name="s5_chain" if chained else "s5_tile",
    )(p3, x0r, x0i, cs["b3r"], cs["b3i"], cs["cr"], cs["ci"], cs["scan"], cs["d"], cs["wglu"], cs["bglu"])
    return y, xlr.reshape(nseq, S5_GROUPS, S5_STATE), xli.reshape(nseq, S5_GROUPS, S5_STATE)


def _s5_consts(a_re, a_im, log_dt, b_re, b_im, c_re, c_im, d, w_glu, b_glu):
    g, p, h = S5_GROUPS, S5_STATE, S5_GROUP
    dt = jnp.exp(log_dt)[:, None]
    mag = jnp.exp(a_re * dt)
    abr = mag * jnp.cos(a_im * dt)
    abi = mag * jnp.sin(a_im * dt)
    den = a_re * a_re + a_im * a_im
    qr = ((abr - 1.0) * a_re + abi * a_im) / den
    qi = (abi * a_re - (abr - 1.0) * a_im) / den
    bbr = qr[..., None] * b_re - qi[..., None] * b_im
    bbi = qr[..., None] * b_im + qi[..., None] * b_re
    pr, pi = [abr], [abi]
    for _ in range(SUBLANES - 1):
        pr, pi = pr + [pr[-1] * abr - pi[-1] * abi], pi + [pr[-1] * abi + pi[-1] * abr]
    pr = jnp.stack(pr).reshape(SUBLANES, g * p)
    pi = jnp.stack(pi).reshape(SUBLANES, g * p)
    row = jnp.arange(SUBLANES)[:, None]

    def lvl(k):
        return [jnp.where(row >= k, pr[k - 1][None, :], 0.0), jnp.where(row >= k, pi[k - 1][None, :], 0.0)]

    scan = jnp.stack(lvl(1) + lvl(2) + lvl(4) + [pr, pi]).astype(F32)
    eye = jnp.eye(SUBLANES, dtype=F32)

    def in_blocks(x):
        xx = x.reshape(S5_GB, SUBLANES, p, h).transpose(0, 1, 3, 2)
        blk = (xx[:, :, :, None, :] * eye[None, :, None, :, None]).reshape(S5_GB, SUBLANES * h, SUBLANES * p)
        hi = blk.astype(BF16)
        lo = (blk - hi.astype(F32)).astype(BF16)
        return jnp.concatenate([hi, hi, lo], axis=1)

    def out_blocks(x):
        xx = x.reshape(S5_GB, SUBLANES, h, p).transpose(0, 1, 3, 2)
        return (xx[:, :, :, None, :] * eye[None, :, None, :, None]).reshape(
            S5_GB, SUBLANES * p, SUBLANES * h).astype(BF16)

    return dict(b3r=in_blocks(bbr), b3i=in_blocks(bbi),
                cr=out_blocks(c_re), ci=out_blocks(-c_im), scan=scan,
                d=d.reshape(1, -1), wglu=w_glu.astype(BF16), bglu=b_glu.reshape(1, -1))


def _seg_sum(x, e2):
    hi, lo = _split2(x)
    return _dot(jnp.concatenate([hi, lo], axis=1), e2)


def _rwkv_prep_kernel(*refs, nlev, chained, tiles_per_seq):
    if chained:
        z_ref, prev_ref, sh_ref = refs[:3]
        refs = refs[3:]
    else:
        z_ref, zp_ref = refs[:2]
        refs = refs[2:]
    (mu_ref, w0_ref, a0_ref, w2_ref, a2_ref, g2_ref, kk_ref, ka_ref, rk_ref, e2_ref, tril_ref, mk_ref, hm_ref,
     ap_o, rp_o, bt_o, kp_o, cc_o, y0_o, v_o, gm_o, g_o, bonus_o) = refs
    tm = RWKV_TM
    z = z_ref[...]
    if chained:
        first = (pl.program_id(0) % tiles_per_seq) == 0
        prev_row = jnp.where(first, sh_ref[0], prev_ref[SUBLANES - 1:SUBLANES, :])
        rid = lax.broadcasted_iota(jnp.int32, z.shape, 0)
        zp = jnp.where(rid == 0, prev_row, pltpu.roll(z, 1, 0))
    else:
        zp = zp_ref[...]
    zm = z + (zp - z) * mu_ref[...]
    w = RWKV_WIDTH
    r = zm[:, 0:w]
    k = zm[:, w:2 * w]
    v = zm[:, 2 * w:3 * w]
    wa = zm[:, 3 * w:3 * w + LANES]
    gi = zm[:, 3 * w + LANES:3 * w + 2 * LANES]
    e2 = e2_ref[...]
    wl = w0_ref[...] + _dot(jnp.tanh(wa).astype(BF16), w2_ref[...])
    sp = jnp.maximum(-wl, 0.0) + jnp.log(1.0 + jnp.exp(-jnp.abs(wl)))
    ld = -jnp.exp(-sp - 0.5)
    a = _sigmoid(a0_ref[...] + _dot(wa.astype(BF16), a2_ref[...]))
    g_o[...] = _dot(_sigmoid(gi).astype(BF16), g2_ref[...])
    kk = k * kk_ref[...]
    kk = kk * lax.rsqrt(jnp.maximum(_seg_sum(kk * kk, e2), 1e-24))
    k2 = k * (1.0 + (a - 1.0) * ka_ref[...])
    b = kk * a
    bonus_o[...] = _seg_sum(r * k2 * rk_ref[...], e2) * v
    v_o[...] = v

    p1, p2, p3 = _split3(ld)
    cum = _dot(tril_ref[...], jnp.concatenate([p1, p2, p3], axis=0))
    gam = jnp.exp(cum)
    ginv = jnp.exp(-cum)
    gm_o[...] = gam
    alpha = kk * jnp.exp(cum - ld)
    beta = b * ginv
    kap = k2 * ginv
    rho = r * gam
    bt_o[...] = beta
    kp_o[...] = kap

    strict = mk_ref[0]
    incl = mk_ref[1]
    eye = mk_ref[2]
    heads = [(j, e) for j in range(RWKV_PAIRS) for e in range(2)]
    nh = len(heads)

    def lanes(j):
        return slice(j * LANES, (j + 1) * LANES)

    al = [alpha[:, lanes(j)] * hm_ref[e] for j, e in heads]
    rh = [rho[:, lanes(j)] * hm_ref[e] for j, e in heads]
    vh = [_split2(v[:, lanes(j)] * hm_ref[e]) for j, e in heads]
    wm = [_dot3_nt(jnp.concatenate([al[i], rh[i]], axis=0),
                   jnp.concatenate([beta[:, lanes(j)], kap[:, lanes(j)]], axis=0)) for i, (j, e) in enumerate(heads)]
    mm = [wm[i][:tm, :tm] * strict for i in range(nh)]
    nn = [_split2(wm[i][:tm, tm:] * strict) for i in range(nh)]
    pp = [_split2(wm[i][tm:, :tm] * incl) for i in range(nh)]
    qq = [_split2(wm[i][tm:, tm:] * incl) for i in range(nh)]
    tinv = [eye - mm[i] * mk_ref[3] for i in range(nh)]
    for lv in range(1, nlev):
        ts = [_split2(tinv[i]) for i in range(nh)]
        tmp = [_dot3s(ts[i], _split2(mm[i] * mk_ref[3 + lv])) for i in range(nh)]
        tinv = [tinv[i] - _dot3s(_split2(tmp[i]), ts[i]) for i in range(nh)]
    ts = [_split2(tinv[i]) for i in range(nh)]
    ap = [_dot3s(ts[i], _split2(al[i])) for i in range(nh)]
    nv = [_dot3s(nn[i], vh[i]) for i in range(nh)]
    cc = [_dot3s(ts[i], _split2(nv[i])) for i in range(nh)]
    qv = [_dot3s(qq[i], vh[i]) for i in range(nh)]
    rp = [rh[i] - _dot3s(pp[i], _split2(ap[i])) for i in range(nh)]
    y0 = [qv[i] - _dot3s(pp[i], _split2(cc[i])) for i in range(nh)]
    for j in range(RWKV_PAIRS):
        ap_o[:, lanes(j)] = ap[2 * j] + ap[2 * j + 1]
        rp_o[:, lanes(j)] = rp[2 * j] + rp[2 * j + 1]
        cc_o[:, lanes(j)] = cc[2 * j] + cc[2 * j + 1]
        y0_o[:, lanes(j)] = y0[2 * j] + y0[2 * j + 1]


def _rwkv_tables(tm, blk):
    t = np.arange(tm)
    same = (t[:, None] // blk) == (t[None, :] // blk)
    tril = (same & (t[None, :] <= t[:, None])).astype(np.float32)
    masks = [same & (t[None, :] < t[:, None]), same & (t[None, :] <= t[:, None]), np.eye(tm, dtype=bool)]
    n = 2
    while n <= blk:
        m = n // 2
        masks.append(((t[:, None] // n) == (t[None, :] // n)) & ((t[:, None] % n) >= m) & ((t[None, :] % n) < m))
        n *= 2
    nlev = len(masks) - 3
    hm = np.stack([(np.arange(LANES) < RWKV_HEAD), (np.arange(LANES) >= RWKV_HEAD)]).astype(np.float32)
    return (jnp.asarray(np.concatenate([tril, tril, tril], axis=1), BF16),
            jnp.asarray(np.stack(masks).astype(np.float32)), jnp.asarray(hm.reshape(2, 1, LANES)), nlev)


def _rwkv_prep(p3, prev, cs, nseq, t):
    m = nseq * t
    tm = RWKV_TM
    blk = min(RWKV_BLOCK, t)
    chained = t >= tm
    tril3, masks, hm, nlev = _rwkv_tables(tm, blk)
    w = RWKV_WIDTH
    row = lambda i: (i, 0)
    whole = lambda i: (0, 0)
    whole3 = lambda i: (0, 0, 0)
    vec = pl.BlockSpec((1, w), whole)
    if chained:
        tps = t // tm
        lead = [pl.BlockSpec((tm, RWKV_SHIFT_WIDTH), row),
                pl.BlockSpec((SUBLANES, RWKV_SHIFT_WIDTH),
                             lambda i: (jnp.maximum(i * (tm // SUBLANES) - 1, 0), 0)),
                pl.BlockSpec((1, 1, RWKV_SHIFT_WIDTH), lambda i: (i // tps, 0, 0))]
        args = (p3, p3, prev)
    else:
        tps = 1
        lead = [pl.BlockSpec((tm, RWKV_SHIFT_WIDTH), row), pl.BlockSpec((tm, RWKV_SHIFT_WIDTH), row)]
        args = (p3, prev)
    return pl.pallas_call(
        functools.partial(_rwkv_prep_kernel, nlev=nlev, chained=chained, tiles_per_seq=tps),
        grid=(m // tm,),
        in_specs=lead + [
            pl.BlockSpec((1, RWKV_SHIFT_WIDTH), whole),
            vec, vec,
            pl.BlockSpec((LANES, w), whole), pl.BlockSpec((LANES, w), whole), pl.BlockSpec((LANES, w), whole),
            vec, vec, vec,
            pl.BlockSpec((2 * w, w), whole),
            pl.BlockSpec(tril3.shape, whole), pl.BlockSpec(masks.shape, whole3), pl.BlockSpec(hm.shape, whole3),
        ],
        out_specs=[pl.BlockSpec((tm, w), row)] * 10,
        out_shape=[jax.ShapeDtypeStruct((m, w), F32)] * 10,
        compiler_params=_cparams(("parallel",)),
        name="rwkv_prep",
    )(*args, cs["mu"], cs["w0"], cs["a0"], cs["w2"], cs["a2"], cs["g2"], cs["k_k"], cs["k_a"], cs["r_k"],
      cs["e2w"], tril3, masks, hm)


def _rwkv_rec_kernel(*refs, nbp, tc, blk, aliased):
    ap_ref, rp_ref, bt_ref, kp_ref, cc_ref, y0_ref, v_ref, gm_ref, s0_ref, bd_ref = refs[:10]
    y_ref, so_ref, s_ref = refs[10 + (1 if aliased else 0):]
    c = pl.program_id(1)
    hd = RWKV_HEAD
    pairs = [(n, j) for n in range(nbp) for j in range(RWKV_PAIRS)]

    @pl.when(c == 0)
    def _():
        zero = jnp.zeros((hd, hd), F32)
        for idx, (n, j) in enumerate(pairs):
            top = jnp.concatenate([s0_ref[n, 2 * j], zero], axis=1)
            bot = jnp.concatenate([zero, s0_ref[n, 2 * j + 1]], axis=1)
            s_ref[idx] = jnp.concatenate([top, bot], axis=0)

    bd = bd_ref[...]

    def block(bi, carry):
        rows = pl.ds(pl.multiple_of(bi * blk, blk), blk)
        gmats = []
        for idx, (n, j) in enumerate(pairs):
            ls = slice(j * LANES, (j + 1) * LANES)
            sh, sl = _split2(s_ref[idx])
            xh, xl = _split2(jnp.concatenate([ap_ref[n, rows, ls], rp_ref[n, rows, ls]], axis=0))
            gmats.append(_dot_nt(jnp.concatenate([xh, xh, xl], axis=1), jnp.concatenate([sh, sl, sh], axis=1)))
        upds = []
        for idx, (n, j) in enumerate(pairs):
            ls = slice(j * LANES, (j + 1) * LANES)
            gmat = gmats[idx]
            e = -gmat[:blk] - cc_ref[n, rows, ls]
            y_ref[n, rows, ls] = gmat[blk:] + y0_ref[n, rows, ls]
            evh, evl = _split2(jnp.concatenate([e, v_ref[n, rows, ls]], axis=0))
            bkh, bkl = _split2(jnp.concatenate([bt_ref[n, rows, ls], kp_ref[n, rows, ls]], axis=0))
            upds.append(_dot_tn(jnp.concatenate([evh, evh, evl], axis=0), jnp.concatenate([bkh, bkl, bkh], axis=0)))
        for idx, (n, j) in enumerate(pairs):
            ls = slice(j * LANES, (j + 1) * LANES)
            gl = gm_ref[n, rows, ls][blk - 1:blk, :]
            s_ref[idx] = (s_ref[idx] + upds[idx] * bd) * gl
        return carry

    lax.fori_loop(0, tc // blk, block, 0)

    @pl.when(c == pl.num_programs(1) - 1)
    def _():
        for idx, (n, j) in enumerate(pairs):
            s = s_ref[idx]
            so_ref[n, 2 * j] = s[:hd, :hd]
            so_ref[n, 2 * j + 1] = s[hd:, hd:]


def _rwkv_rec(arrs, s0_all, l, so_prev, cs, nseq, t, nbp=4, tc=128):
    w = RWKV_WIDTH
    tc = min(tc, t)
    blk = min(RWKV_BLOCK, t)
    depth = s0_all.shape[0]
    seq_map = lambda i, c: (i, c, 0)
    st_map = lambda i, c: (l, i, 0, 0, 0)
    blkspec = pl.BlockSpec((nbp, tc, w), seq_map)
    st = pl.BlockSpec((None, nbp, RWKV_HEADS, RWKV_HEAD, RWKV_HEAD), st_map)
    aliased = so_prev is not None
    in_specs = [blkspec] * 8 + [st, pl.BlockSpec((LANES, LANES), lambda i, c: (0, 0))]
    args = list(arrs) + [s0_all, cs["bd"]]
    aliases = {}
    if aliased:
        in_specs.append(pl.BlockSpec(memory_space=pl.ANY))
        args.append(so_prev)
        aliases = {10: 1}
    return pl.pallas_call(
        functools.partial(_rwkv_rec_kernel, nbp=nbp, tc=tc, blk=blk, aliased=aliased),
        grid=(nseq // nbp, t // tc),
        in_specs=in_specs,
        out_specs=[blkspec, st],
        out_shape=[jax.ShapeDtypeStruct((nseq, t, w), F32),
                   jax.ShapeDtypeStruct((depth, nseq, RWKV_HEADS, RWKV_HEAD, RWKV_HEAD), F32)],
        scratch_shapes=[pltpu.VMEM((nbp * RWKV_PAIRS, LANES, LANES), F32)],
        input_output_aliases=aliases,
        compiler_params=_cparams(("parallel", "arbitrary")),
        name="rwkv_rec",
    )(*args)


def _rwkv_post_kernel(y_ref, bonus_ref, g_ref, lnw_ref, lnb_ref, e2_ref, o_ref):
    y = y_ref[...]
    e2 = e2_ref[...]
    mean = _seg_sum(y, e2) * (1.0 / RWKV_HEAD)
    yc = y - mean
    var = _seg_sum(yc * yc, e2) * (1.0 / RWKV_HEAD)
    yn = yc * lax.rsqrt(var + RWKV_GN_EPS) * lnw_ref[...] + lnb_ref[...]
    o_ref[...] = (yn + bonus_ref[...]) * g_ref[...]


def _rwkv_post(y, bonus, g, cs, tm=512):
    m, w = y.shape
    tm = min(tm, m)
    row = lambda i: (i, 0)
    whole = lambda i: (0, 0)
    return pl.pallas_call(
        _rwkv_post_kernel,
        grid=(m // tm,),
        in_specs=[pl.BlockSpec((tm, w), row)] * 3 + [pl.BlockSpec((1, w), whole)] * 2
        + [pl.BlockSpec((2 * w, w), whole)],
        out_specs=pl.BlockSpec((tm, w), row),
        out_shape=jax.ShapeDtypeStruct((m, w), F32),
        compiler_params=_cparams(("parallel",)),
        name="rwkv_post",
    )(y, bonus, g, cs["ln_w"], cs["ln_b"], cs["e2w"])


def _rwkv_consts(mu, w0, w2, a0, a2, g2, k_k, k_a, r_k, ln_w, ln_b):
    w = RWKV_WIDTH
    seg = np.arange(w) // RWKV_HEAD
    e_w = (seg[:, None] == seg[None, :]).astype(np.float32)
    seg = np.arange(LANES) // RWKV_HEAD
    bd = (seg[:, None] == seg[None, :]).astype(np.float32)
    z64 = jnp.zeros((RWKV_HEAD, w), F32)
    return dict(
        mu=mu.reshape(1, -1), w0=w0.reshape(1, -1), a0=a0.reshape(1, -1),
        w2=jnp.concatenate([w2, z64], axis=0).astype(BF16),
        a2=jnp.concatenate([z64, a2], axis=0).astype(BF16),
        g2=g2.astype(BF16), k_k=k_k.reshape(1, -1), k_a=k_a.reshape(1, -1), r_k=r_k.reshape(1, -1),
        ln_w=ln_w.reshape(1, -1), ln_b=ln_b.reshape(1, -1),
        e2w=jnp.asarray(np.concatenate([e_w, e_w], axis=0), BF16),
        bd=jnp.asarray(bd, F32))


def _rwkv(p3, shift_l, wkv_all, l, so_prev, cs, nseq, t):
    if t >= RWKV_TM:
        prev = shift_l.reshape(nseq, 1, RWKV_SHIFT_WIDTH)
    else:
        z3 = p3.reshape(nseq, t, N_IN)[:, :, :RWKV_SHIFT_WIDTH]
        prev = jnp.concatenate([shift_l[:, None, :], z3[:, :-1]], axis=1).reshape(nseq * t, RWKV_SHIFT_WIDTH)
    outs = _rwkv_prep(p3, prev, cs, nseq, t)
    arrs = [a.reshape(nseq, t, RWKV_WIDTH) for a in outs[:8]]
    y, so = _rwkv_rec(arrs, wkv_all, l, so_prev, cs, nseq, t)
    return _rwkv_post(y.reshape(nseq * t, RWKV_WIDTH), outs[9], outs[8], cs), so


def _hgrn_tables(c):
    t = np.arange(c)
    j = np.arange(c)
    mats = [(j[None, :] <= t[:, None]), (j[None, :] > t[:, None])]
    masks = []
    n = 2
    while n <= c:
        m = n // 2
        bs = (t // n) * n
        hi_half = (t % n) >= m
        mats.append(hi_half[:, None] & (j[None, :] >= (bs + m)[:, None]) & (j[None, :] <= t[:, None]))
        mats.append((~hi_half)[:, None] & (j[None, :] > t[:, None]) & (j[None, :] <= (bs + m - 1)[:, None]))
        masks.append(((t[:, None] // n) == (t[None, :] // n)) & hi_half[:, None] & (~hi_half)[None, :])
        n *= 2
    w = np.concatenate(mats, axis=0).astype(np.float32)
    w3 = np.concatenate([w, w, w], axis=1)
    return jnp.asarray(w3, BF16), jnp.asarray(np.stack(masks).astype(np.float32)), len(masks)


def _hgrn_kernel(*refs, rb, c, nlev, chained, aliased):
    q_ref, f_ref, i_ref, g_ref, s0_ref, lb_ref, nw_ref, w3_ref, mk_ref = refs[:9]
    o_ref, so_ref, st_ref = refs[9 + (1 if aliased else 0):]
    cidx = pl.program_id(1)
    hdim = HGRN_DIM

    if chained:
        @pl.when(cidx == 0)
        def _():
            for h in range(HGRN_HEADS):
                st_ref[h] = s0_ref[0, h].T

    w3 = w3_ref[...]
    row_i = lax.broadcasted_iota(jnp.int32, (c, c), 0)
    col_i = lax.broadcasted_iota(jnp.int32, (c, c), 1)
    eye = (row_i == col_i).astype(F32)

    def block(i, carry):
        rows = pl.ds(pl.multiple_of(i * c, c), c)
        for h in range(HGRN_HEADS):
            ls = slice(h * hdim, (h + 1) * hdim)
            lb = lb_ref[:, ls]
            fg = lb + (1.0 - lb) * _sigmoid(f_ref[rows, ls])
            lf = jnp.log(fg)
            kk = 1.0 - fg
            q = q_ref[rows, ls]
            qs = q * _sigmoid(q)
            v = i_ref[rows, ls]
            p1, p2, p3 = _split3(lf)
            ex = jnp.exp(_dot(w3, jnp.concatenate([p1, p2, p3], axis=0)))
            eb = ex[0:c]
            ek = ex[c:2 * c]
            if chained:
                st = st_ref[h]
            else:
                st = s0_ref[i, h].T
            o = _dot_nt((qs * eb).astype(BF16), st.astype(BF16))
            att = eye * jnp.sum(qs * kk, axis=-1, keepdims=True)
            for lv in range(nlev):
                qe = (qs * ex[(2 + 2 * lv) * c:(3 + 2 * lv) * c]).astype(BF16)
                ke = (kk * ex[(3 + 2 * lv) * c:(4 + 2 * lv) * c]).astype(BF16)
                att = att + jnp.where(mk_ref[lv] > 0.0, _dot_nt(qe, ke), 0.0)
            vb = v.astype(BF16)
            o = o + _dot(att.astype(BF16), vb)
            st = st * eb[c - 1:c, :] + _dot_tn(vb, (kk * ek).astype(BF16))
            if chained:
                st_ref[h] = st
            else:
                so_ref[i, h] = st.T
            o = o * lax.rsqrt(jnp.mean(o * o, axis=-1, keepdims=True) + HGRN_EPS)
            gg = g_ref[rows, ls]
            o_ref[rows, ls] = o * nw_ref[:, ls] * (gg * _sigmoid(gg))
        return carry

    lax.fori_loop(0, rb // c, block, 0)

    if chained:
        @pl.when(cidx == pl.num_programs(1) - 1)
        def _():
            for h in range(HGRN_HEADS):
                so_ref[0, h] = st_ref[h].T


def _hgrn(p3, s0_all, l, so_prev, lb, nw, nseq, t):
    rows = nseq * t
    depth = s0_all.shape[0]
    c = math.gcd(t, 64)
    chained = t > c
    w3, masks, nlev = _hgrn_tables(c)
    w = HGRN_WIDTH
    if chained:
        rb = 256
        nblk = t // rb
        grid = (nseq, nblk)
        col = lambda cb: (lambda n, k: (n * nblk + k, cb))
        out_map = lambda n, k: (n * nblk + k, 0)
        nsb = 1
    else:
        nsb = 16
        rb = nsb * c
        grid = (rows // rb, 1)
        col = lambda cb: (lambda n, k: (n, cb))
        out_map = lambda n, k: (n, 0)
    st_spec = pl.BlockSpec((None, nsb, HGRN_HEADS, HGRN_DIM, HGRN_DIM), lambda n, k: (l, n, 0, 0, 0))
    whole2 = lambda n, k: (0, 0)
    whole3 = lambda n, k: (0, 0, 0)
    aliased = so_prev is not None
    in_specs = [pl.BlockSpec((rb, w), col(4)), pl.BlockSpec((rb, w), col(5)), pl.BlockSpec((rb, w), col(6)),
                pl.BlockSpec((rb, w), col(7)), st_spec,
                pl.BlockSpec((1, w), whole2), pl.BlockSpec((1, w), whole2),
                pl.BlockSpec(w3.shape, whole2), pl.BlockSpec(masks.shape, whole3)]
    args = [p3, p3, p3, p3, s0_all, lb, nw, w3, masks]
    aliases = {}
    if aliased:
        in_specs.append(pl.BlockSpec(memory_space=pl.ANY))
        args.append(so_prev)
        aliases = {9: 1}
    return pl.pallas_call(
        functools.partial(_hgrn_kernel, rb=rb, c=c, nlev=nlev, chained=chained, aliased=aliased),
        grid=grid,
        in_specs=in_specs,
        out_specs=[pl.BlockSpec((rb, w), out_map), st_spec],
        out_shape=[jax.ShapeDtypeStruct((rows, w), F32),
                   jax.ShapeDtypeStruct((depth, nseq, HGRN_HEADS, HGRN_DIM, HGRN_DIM), F32)],
        scratch_shapes=[pltpu.VMEM((HGRN_HEADS, HGRN_DIM, HGRN_DIM), F32)],
        input_output_aliases=aliases,
        compiler_params=_cparams(("parallel", "arbitrary")),
        name="hgrn_chain" if chained else "hgrn_tile",
    )(*args)


def _run(x, nseq, t, states, lw, norm_final):
    s5_re0, s5_im0, shift0, wkv0, hgrn0 = states
    depth = len(lw)
    s5_re0 = s5_re0.reshape(depth, nseq, S5_LANES)
    s5_im0 = s5_im0.reshape(depth, nseq, S5_LANES)
    s5_re_l, s5_im_l, shift_l = [], [], []
    wkv_out = None
    hgrn_out = None
    for l in range(depth):
        w = lw[l]
        x = _ffn(x, w["norm_ffn1"], w["f1g"], w["f1u"], w["f1d"])
        p3 = _inproj(x, w["norm_mix"], w["w_in"])
        y_s5, s_re, s_im = _s5(p3, s5_re0, s5_im0, l, w["s5"], nseq, t)
        y_rw, wkv_out = _rwkv(p3, shift0[l], wkv0, l, wkv_out, w["rwkv"], nseq, t)
        y_hg, hgrn_out = _hgrn(p3, hgrn0, l, hgrn_out, w["hgrn_lb"], w["hgrn_nw"], nseq, t)
        x = _outproj(x, y_s5, y_rw, y_hg, w["wo_s5"], w["wo_rw"], w["wo_hg"])
        x = _ffn(x, w["norm_ffn2"], w["f2g"], w["f2u"], w["f2d"])
        s5_re_l.append(s_re)
        s5_im_l.append(s_im)
        shift_l.append(p3.reshape(nseq, t, N_IN)[:, t - 1, :RWKV_SHIFT_WIDTH])
    y = _rmsnorm(x, norm_final.reshape(1, -1))
    return y, jnp.stack(s5_re_l), jnp.stack(s5_im_l), jnp.stack(shift_l), wkv_out, hgrn_out


def kernel(x_prompt, x_sample, state_s5_re, state_s5_im, state_rwkv_shift, state_rwkv_wkv, state_hgrn, norm_ffn1, ffn1_w_gate, ffn1_w_up, ffn1_w_down, norm_mix, w_in, s5_a_re, s5_a_im, s5_log_dt, s5_b_re, s5_b_im, s5_c_re, s5_c_im, s5_d, s5_w_glu, s5_b_glu, rwkv_mu, rwkv_w0, rwkv_w2, rwkv_a0, rwkv_a2, rwkv_g2, rwkv_k_k, rwkv_k_a, rwkv_r_k, rwkv_ln_w, rwkv_ln_b, hgrn_lb_raw, hgrn_norm_w, w_out, norm_ffn2, ffn2_w_gate, ffn2_w_up, ffn2_w_down, norm_final):
    depth = w_in.shape[0]
    nb, seq, d = x_prompt.shape
    ns, dseq, _ = x_sample.shape

    p_lb = jax.nn.softmax(hgrn_lb_raw.astype(F32), axis=0)
    lower_bounds = jnp.cumsum(p_lb, axis=0) - p_lb[0]

    o_rw = S5_WIDTH
    o_hg = S5_WIDTH + RWKV_SHIFT_WIDTH
    lw = []
    for l in range(depth):
        wi = w_in[l]
        wo = w_out[l].astype(BF16)
        lw.append(dict(
            norm_ffn1=norm_ffn1[l].reshape(1, -1), norm_mix=norm_mix[l].reshape(1, -1),
            norm_ffn2=norm_ffn2[l].reshape(1, -1),
            f1g=ffn1_w_gate[l].astype(BF16), f1u=ffn1_w_up[l].astype(BF16), f1d=ffn1_w_down[l].astype(BF16),
            f2g=ffn2_w_gate[l].astype(BF16), f2u=ffn2_w_up[l].astype(BF16), f2d=ffn2_w_down[l].astype(BF16),
            w_in=jnp.concatenate([wi[:, o_rw:o_hg], wi[:, :o_rw], wi[:, o_hg:]], axis=1).astype(BF16),
            wo_s5=wo[:S5_WIDTH], wo_rw=wo[S5_WIDTH:S5_WIDTH + RWKV_WIDTH], wo_hg=wo[S5_WIDTH + RWKV_WIDTH:],
            s5=_s5_consts(s5_a_re[l], s5_a_im[l], s5_log_dt[l], s5_b_re[l], s5_b_im[l], s5_c_re[l], s5_c_im[l],
                          s5_d[l], s5_w_glu[l], s5_b_glu[l]),
            rwkv=_rwkv_consts(rwkv_mu[l], rwkv_w0[l], rwkv_w2[l], rwkv_a0[l], rwkv_a2[l], rwkv_g2[l],
                              rwkv_k_k[l], rwkv_k_a[l], rwkv_r_k[l], rwkv_ln_w[l], rwkv_ln_b[l]),
            hgrn_lb=lower_bounds[l].reshape(1, -1), hgrn_nw=hgrn_norm_w[l].reshape(1, -1),
        ))

    def zeros_like_state(s):
        return jnp.zeros((depth, nb) + s.shape[2:], F32)

    p_states = tuple(zeros_like_state(s) for s in
                     (state_s5_re, state_s5_im, state_rwkv_shift, state_rwkv_wkv, state_hgrn))
    s_states = (state_s5_re, state_s5_im, state_rwkv_shift, state_rwkv_wkv, state_hgrn)
    y_p, s5re_p, s5im_p, shift_p, wkv_p, hgrn_p = _run(x_prompt.reshape(nb * seq, d), nb, seq, p_states, lw, norm_final)
    y_s, s5re_s, s5im_s, shift_s, wkv_s, hgrn_s = _run(x_sample.reshape(ns * dseq, d), ns, dseq, s_states, lw, norm_final)
    return (y_p.reshape(nb, seq, d), y_s.reshape(ns, dseq, d), s5re_p, s5im_p, shift_p, wkv_p, hgrn_p,
            s5re_s, s5im_s, shift_s, wkv_s, hgrn_s)
```

```python
import functools
import math

import numpy as np
import jax
import jax.numpy as jnp
from jax import lax
from jax.experimental import pallas as pl
from jax.experimental.pallas import tpu as pltpu

F32 = jnp.float32
BF16 = jnp.bfloat16

NORM_EPS = 1e-6
RWKV_GN_EPS = 64e-5
HGRN_EPS = 1e-5

D_MODEL = 2048
S5_WIDTH = 512
S5_GROUP = 16
S5_GROUPS = 32
S5_STATE = 64
S5_LANES = S5_GROUPS * S5_STATE
S5_GB = 4
RWKV_WIDTH = 768
RWKV_HEAD = 64
RWKV_HEADS = 12
RWKV_PAIRS = 6
RWKV_SHIFT_WIDTH = 2560
RWKV_BLOCK = 16
RWKV_TM = 128
HGRN_WIDTH = 768
HGRN_HEADS = 6
HGRN_DIM = 128
N_IN = 6144

SUBLANES = 8
LANES = 128
VMEM_LIMIT = 60 * 1024 * 1024


def _cparams(sem):
    return pltpu.CompilerParams(dimension_semantics=sem, vmem_limit_bytes=VMEM_LIMIT)


def _split2(x):
    hi = x.astype(BF16)
    lo = (x - hi.astype(F32)).astype(BF16)
    return hi, lo


def _split3(x):
    p1 = x.astype(BF16)
    r1 = x - p1.astype(F32)
    p2 = r1.astype(BF16)
    p3 = (r1 - p2.astype(F32)).astype(BF16)
    return p1, p2, p3


def _dot(a, b):
    return jnp.dot(a, b, preferred_element_type=F32)


def _dot_nt(a, b):
    return lax.dot_general(a, b, (((1,), (1,)), ((), ())), preferred_element_type=F32)


def _dot_tn(a, b):
    return lax.dot_general(a, b, (((0,), (0,)), ((), ())), preferred_element_type=F32)


def _dot3(a, b):
    ah, al = _split2(a)
    bh, bl = _split2(b)
    return _dot(jnp.concatenate([ah, ah, al], axis=1), jnp.concatenate([bh, bl, bh], axis=0))


def _dot3s(asp, bsp):
    ah, al = asp
    bh, bl = bsp
    return _dot(jnp.concatenate([ah, ah, al], axis=1), jnp.concatenate([bh, bl, bh], axis=0))


def _dot3_nt(a, b):
    ah, al = _split2(a)
    bh, bl = _split2(b)
    return _dot_nt(jnp.concatenate([ah, ah, al], axis=1), jnp.concatenate([bh, bl, bh], axis=1))


def _sigmoid(x):
    return 1.0 / (1.0 + jnp.exp(-x))


def _ffn_kernel(x_ref, nw_ref, wg_ref, wu_ref, wd_ref, o_ref, h_ref):
    j = pl.program_id(1)

    @pl.when(j == 0)
    def _():
        x = x_ref[...]
        ms = jnp.mean(x * x, axis=-1, keepdims=True)
        h_ref[...] = (x * lax.rsqrt(ms + NORM_EPS) * nw_ref[...]).astype(BF16)
        o_ref[...] = jnp.zeros_like(o_ref)

    h = h_ref[...]
    g = _dot(h, wg_ref[...])
    u = _dot(h, wu_ref[...])
    o_ref[...] += _dot((g * _sigmoid(g) * u).astype(BF16), wd_ref[...])

    @pl.when(j == pl.num_programs(1) - 1)
    def _():
        o_ref[...] = x_ref[...] + 0.5 * o_ref[...]


def _ffn(x, nw, wg, wu, wd, l, tm=1024, tf=512):
    m, d = x.shape
    ff = wg.shape[2]
    return pl.pallas_call(
        _ffn_kernel,
        grid=(m // tm, ff // tf),
        in_specs=[
            pl.BlockSpec((tm, d), lambda i, j: (i, 0)),
            pl.BlockSpec((1, d), lambda i, j: (0, 0)),
            pl.BlockSpec((None, d, tf), lambda i, j: (l, 0, j)),
            pl.BlockSpec((None, d, tf), lambda i, j: (l, 0, j)),
            pl.BlockSpec((None, tf, d), lambda i, j: (l, j, 0)),
        ],
        out_specs=pl.BlockSpec((tm, d), lambda i, j: (i, 0)),
        out_shape=jax.ShapeDtypeStruct((m, d), F32),
        scratch_shapes=[pltpu.VMEM((tm, d), BF16)],
        compiler_params=_cparams(("parallel", "arbitrary")),
        name="ffn",
    )(x, nw, wg, wu, wd)


def _inproj_kernel(x_ref, nw_ref, w_ref, o_ref, h_ref):
    @pl.when(pl.program_id(1) == 0)
    def _():
        x = x_ref[...]
        ms = jnp.mean(x * x, axis=-1, keepdims=True)
        h_ref[...] = (x * lax.rsqrt(ms + NORM_EPS) * nw_ref[...]).astype(BF16)

    o_ref[...] = _dot(h_ref[...], w_ref[...])


def _inproj(x, nw, w, tm=1024, tn=1024):
    m, d = x.shape
    n = w.shape[1]
    return pl.pallas_call(
        _inproj_kernel,
        grid=(m // tm, n // tn),
        in_specs=[
            pl.BlockSpec((tm, d), lambda i, j: (i, 0)),
            pl.BlockSpec((1, d), lambda i, j: (0, 0)),
            pl.BlockSpec((d, tn), lambda i, j: (0, j)),
        ],
        out_specs=pl.BlockSpec((tm, tn), lambda i, j: (i, j)),
        out_shape=jax.ShapeDtypeStruct((m, n), F32),
        scratch_shapes=[pltpu.VMEM((tm, d), BF16)],
        compiler_params=_cparams(("parallel", "arbitrary")),
        name="inproj",
    )(x, nw, w)


def _outproj_kernel(x_ref, a_ref, b_ref, c_ref, wa_ref, wb_ref, wc_ref, o_ref):
    acc = _dot(a_ref[...].astype(BF16), wa_ref[...])
    acc += _dot(b_ref[...].astype(BF16), wb_ref[...])
    acc += _dot(c_ref[...].astype(BF16), wc_ref[...])
    o_ref[...] = x_ref[...] + acc


def _outproj(x, ya, yb, yc, wa, wb, wc, tm=512):
    m, d = x.shape
    row = lambda i: (i, 0)
    whole = lambda i: (0, 0)
    return pl.pallas_call(
        _outproj_kernel,
        grid=(m // tm,),
        in_specs=[
            pl.BlockSpec((tm, d), row),
            pl.BlockSpec((tm, ya.shape[1]), row),
            pl.BlockSpec((tm, yb.shape[1]), row),
            pl.BlockSpec((tm, yc.shape[1]), row),
            pl.BlockSpec(wa.shape, whole),
            pl.BlockSpec(wb.shape, whole),
            pl.BlockSpec(wc.shape, whole),
        ],
        out_specs=pl.BlockSpec((tm, d), row),
        out_shape=jax.ShapeDtypeStruct((m, d), F32),
        compiler_params=_cparams(("parallel",)),
        name="outproj",
    )(x, ya, yb, yc, wa, wb, wc)


def _rmsnorm_kernel(x_ref, nw_ref, o_ref):
    x = x_ref[...]
    ms = jnp.mean(x * x, axis=-1, keepdims=True)
    o_ref[...] = x * lax.rsqrt(ms + NORM_EPS) * nw_ref[...]


def _rmsnorm(x, nw, tm=512):
    m, d = x.shape
    return pl.pallas_call(
        _rmsnorm_kernel,
        grid=(m // tm,),
        in_specs=[pl.BlockSpec((tm, d), lambda i: (i, 0)), pl.BlockSpec((1, d), lambda i: (0, 0))],
        out_specs=pl.BlockSpec((tm, d), lambda i: (i, 0)),
        out_shape=jax.ShapeDtypeStruct((m, d), F32),
        compiler_params=_cparams(("parallel",)),
        name="final_norm",
    )(x, nw)


def _s5_kernel(u_ref, x0r_ref, x0i_ref, b3r_ref, b3i_ref, cr_ref, ci_ref, cst_ref, d_ref, wglu_ref,
               bglu_ref, y_ref, xlr_ref, xli_ref, xr_ref, xi_ref, cr_s, ci_s, *, tr, chained):
    c = pl.program_id(1)
    u = u_ref[...]
    uh, ul = _split2(u)
    for gb in range(S5_GB):
        sl = slice(gb * LANES, (gb + 1) * LANES)
        lhs = jnp.concatenate([uh[:, sl], ul[:, sl], uh[:, sl]], axis=1)
        osl = slice(gb * 512, (gb + 1) * 512)
        xr_ref[:, osl] = _dot(lhs, b3r_ref[gb])
        xi_ref[:, osl] = _dot(lhs, b3i_ref[gb])

    if chained:
        @pl.when(c == 0)
        def _():
            cr_s[...] = x0r_ref[0]
            ci_s[...] = x0i_ref[0]

    a1r, a1i, a2r, a2i, a4r, a4i, pr, pi = [cst_ref[i] for i in range(8)]

    def tile(j, carry):
        rows = pl.ds(pl.multiple_of(j * SUBLANES, SUBLANES), SUBLANES)
        xr = xr_ref[rows, :]
        xi = xi_ref[rows, :]
        for k, ar, ai in ((1, a1r, a1i), (2, a2r, a2i), (4, a4r, a4i)):
            sr = pltpu.roll(xr, k, 0)
            si = pltpu.roll(xi, k, 0)
            xr, xi = xr + ar * sr - ai * si, xi + ar * si + ai * sr
        if chained:
            x0r = cr_s[...]
            x0i = ci_s[...]
        else:
            x0r = x0r_ref[pl.ds(j, 1), :]
            x0i = x0i_ref[pl.ds(j, 1), :]
        x0r = jnp.broadcast_to(x0r, xr.shape)
        x0i = jnp.broadcast_to(x0i, xr.shape)
        xr, xi = xr + pr * x0r - pi * x0i, xi + pr * x0i + pi * x0r
        xr_ref[rows, :] = xr
        xi_ref[rows, :] = xi
        if chained:
            cr_s[...] = xr[SUBLANES - 1:SUBLANES, :]
            ci_s[...] = xi[SUBLANES - 1:SUBLANES, :]
        else:
            xlr_ref[pl.ds(j, 1), :] = xr[SUBLANES - 1:SUBLANES, :]
            xli_ref[pl.ds(j, 1), :] = xi[SUBLANES - 1:SUBLANES, :]
        return carry

    lax.fori_loop(0, tr // SUBLANES, tile, 0)

    if chained:
        @pl.when(c == pl.num_programs(1) - 1)
        def _():
            xlr_ref[0] = cr_s[...]
            xli_ref[0] = ci_s[...]

    ys = []
    for gb in range(S5_GB):
        osl = slice(gb * 512, (gb + 1) * 512)
        ys.append(_dot(xr_ref[:, osl].astype(BF16), cr_ref[gb]) + _dot(xi_ref[:, osl].astype(BF16), ci_ref[gb]))
    y = jnp.concatenate(ys, axis=1) + d_ref[...] * u
    y = 0.5 * y * (1.0 + jnp.tanh(math.sqrt(2.0 / math.pi) * (y + 0.044715 * (y * y * y))))
    z = _dot(y.astype(BF16), wglu_ref[...]) + bglu_ref[...]
    y_ref[...] = y * _sigmoid(z)


def _s5(p3, x0r, x0i, l, cs, nseq, t, tr=512):
    rows = nseq * t
    depth = x0r.shape[0]
    chained = t > SUBLANES
    if chained:
        grid = (nseq, t // tr)
        row_map = lambda n, c: (n * (t // tr) + c, 5)
        out_map = lambda n, c: (n * (t // tr) + c, 0)
        x0r = x0r.reshape(depth, nseq, 1, S5_LANES)
        x0i = x0i.reshape(depth, nseq, 1, S5_LANES)
        in_st = pl.BlockSpec((None, 1, 1, S5_LANES), lambda n, c: (l, n, 0, 0))
        st_spec = pl.BlockSpec((1, 1, S5_LANES), lambda n, c: (n, 0, 0))
        st_shape = jax.ShapeDtypeStruct((nseq, 1, S5_LANES), F32)
    else:
        assert t == SUBLANES
        grid = (rows // tr, 1)
        row_map = lambda n, c: (n, 5)
        out_map = lambda n, c: (n, 0)
        in_st = pl.BlockSpec((None, tr // SUBLANES, S5_LANES), lambda n, c: (l, n, 0))
        st_spec = pl.BlockSpec((tr // SUBLANES, S5_LANES), lambda n, c: (n, 0))
        st_shape = jax.ShapeDtypeStruct((nseq, S5_LANES), F32)
    whole3 = lambda n, c: (0, 0, 0)
    whole2 = lambda n, c: (0, 0)
    y, xlr, xli = pl.pallas_call(
        functools.partial(_s5_kernel, tr=tr, chained=chained),
        grid=grid,
        in_specs=[
            pl.BlockSpec((tr, S5_WIDTH), row_map),
            in_st, in_st,
            pl.BlockSpec(cs["b3r"].shape, whole3),
            pl.BlockSpec(cs["b3i"].shape, whole3),
            pl.BlockSpec(cs["cr"].shape, whole3),
            pl.BlockSpec(cs["ci"].shape, whole3),
            pl.BlockSpec(cs["scan"].shape, whole3),
            pl.BlockSpec((1, S5_WIDTH), whole2),
            pl.BlockSpec((S5_WIDTH, S5_WIDTH), whole2),
            pl.BlockSpec((1, S5_WIDTH), whole2),
        ],
        out_specs=[pl.BlockSpec((tr, S5_WIDTH), out_map), st_spec, st_spec],
        out_shape=[jax.ShapeDtypeStruct((rows, S5_WIDTH), F32), st_shape, st_shape],
        scratch_shapes=[pltpu.VMEM((tr, S5_LANES), F32), pltpu.VMEM((tr, S5_LANES), F32),
                        pltpu.VMEM((1, S5_LANES), F32), pltpu.VMEM((1, S5_LANES), F32)],
        compiler_params=_cparams(("parallel", "arbitrary")),
        name="s5_chain" if chained else "s5_tile",
    )(p3, x0r, x0i, cs["b3r"], cs["b3i"], cs["cr"], cs["ci"], cs["scan"], cs["d"], cs["wglu"], cs["bglu"])
    return y, xlr.reshape(nseq, S5_GROUPS, S5_STATE), xli.reshape(nseq, S5_GROUPS, S5_STATE)


def _s5_consts(a_re, a_im, log_dt, b_re, b_im, c_re, c_im, d, w_glu, b_glu):
    g, p, h = S5_GROUPS, S5_STATE, S5_GROUP
    dt = jnp.exp(log_dt)[:, None]
    mag = jnp.exp(a_re * dt)
    abr = mag * jnp.cos(a_im * dt)
    abi = mag * jnp.sin(a_im * dt)
    den = a_re * a_re + a_im * a_im
    qr = ((abr - 1.0) * a_re + abi * a_im) / den
    qi = (abi * a_re - (abr - 1.0) * a_im) / den
    bbr = qr[..., None] * b_re - qi[..., None] * b_im
    bbi = qr[..., None] * b_im + qi[..., None] * b_re
    pr, pi = [abr], [abi]
    for _ in range(SUBLANES - 1):
        pr, pi = pr + [pr[-1] * abr - pi[-1] * abi], pi + [pr[-1] * abi + pi[-1] * abr]
    pr = jnp.stack(pr).reshape(SUBLANES, g * p)
    pi = jnp.stack(pi).reshape(SUBLANES, g * p)
    row = jnp.arange(SUBLANES)[:, None]

    def lvl(k):
        return [jnp.where(row >= k, pr[k - 1][None, :], 0.0), jnp.where(row >= k, pi[k - 1][None, :], 0.0)]

    scan = jnp.stack(lvl(1) + lvl(2) + lvl(4) + [pr, pi]).astype(F32)
    eye = jnp.eye(SUBLANES, dtype=F32)

    def in_blocks(x):
        xx = x.reshape(S5_GB, SUBLANES, p, h).transpose(0, 1, 3, 2)
        blk = (xx[:, :, :, None, :] * eye[None, :, None, :, None]).reshape(S5_GB, SUBLANES * h, SUBLANES * p)
        hi = blk.astype(BF16)
        lo = (blk - hi.astype(F32)).astype(BF16)
        return jnp.concatenate([hi, hi, lo], axis=1)

    def out_blocks(x):
        xx = x.reshape(S5_GB, SUBLANES, h, p).transpose(0, 1, 3, 2)
        return (xx[:, :, :, None, :] * eye[None, :, None, :, None]).reshape(
            S5_GB, SUBLANES * p, SUBLANES * h).astype(BF16)

    return dict(b3r=in_blocks(bbr), b3i=in_blocks(bbi),
                cr=out_blocks(c_re), ci=out_blocks(-c_im), scan=scan,
                d=d.reshape(1, -1), wglu=w_glu.astype(BF16), bglu=b_glu.reshape(1, -1))


def _seg_sum(x, e2):
    hi, lo = _split2(x)
    return _dot(jnp.concatenate([hi, lo], axis=1), e2)


def _rwkv_prep_kernel(*refs, nlev, chained, tiles_per_seq):
    if chained:
        z_ref, prev_ref, sh_ref = refs[:3]
        refs = refs[3:]
    else:
        z_ref, zp_ref = refs[:2]
        refs = refs[2:]
    (mu_ref, w0_ref, a0_ref, w2_ref, a2_ref, g2_ref, kk_ref, ka_ref, rk_ref, e2_ref, tril_ref, mk_ref, hm_ref,
     ap_o, rp_o, bt_o, kp_o, cc_o, y0_o, v_o, gm_o, g_o, bonus_o) = refs
    tm = RWKV_TM
    z = z_ref[...]
    if chained:
        first = (pl.program_id(0) % tiles_per_seq) == 0
        prev_row = jnp.where(first, sh_ref[0], prev_ref[SUBLANES - 1:SUBLANES, :])
        rid = lax.broadcasted_iota(jnp.int32, z.shape, 0)
        zp = jnp.where(rid == 0, prev_row, pltpu.roll(z, 1, 0))
    else:
        zp = zp_ref[...]
    zm = z + (zp - z) * mu_ref[...]
    w = RWKV_WIDTH
    r = zm[:, 0:w]
    k = zm[:, w:2 * w]
    v = zm[:, 2 * w:3 * w]
    wa = zm[:, 3 * w:3 * w + LANES]
    gi = zm[:, 3 * w + LANES:3 * w + 2 * LANES]
    e2 = e2_ref[...]
    wl = w0_ref[...] + _dot(jnp.tanh(wa).astype(BF16), w2_ref[...])
    sp = jnp.maximum(-wl, 0.0) + jnp.log(1.0 + jnp.exp(-jnp.abs(wl)))
    ld = -jnp.exp(-sp - 0.5)
    a = _sigmoid(a0_ref[...] + _dot(wa.astype(BF16), a2_ref[...]))
    g_o[...] = _dot(_sigmoid(gi).astype(BF16), g2_ref[...])
    kk = k * kk_ref[...]
    kk = kk * lax.rsqrt(jnp.maximum(_seg_sum(kk * kk, e2), 1e-24))
    k2 = k * (1.0 + (a - 1.0) * ka_ref[...])
    b = kk * a
    bonus_o[...] = _seg_sum(r * k2 * rk_ref[...], e2) * v
    v_o[...] = v

    p1, p2, p3 = _split3(ld)
    cum = _dot(tril_ref[...], jnp.concatenate([p1, p2, p3], axis=0))
    gam = jnp.exp(cum)
    ginv = jnp.exp(-cum)
    gm_o[...] = gam
    alpha = kk * jnp.exp(cum - ld)
    beta = b * ginv
    kap = k2 * ginv
    rho = r * gam
    bt_o[...] = beta
    kp_o[...] = kap

    strict = mk_ref[0]
    incl = mk_ref[1]
    eye = mk_ref[2]
    heads = [(j, e) for j in range(RWKV_PAIRS) for e in range(2)]
    nh = len(heads)

    def lanes(j):
        return slice(j * LANES, (j + 1) * LANES)

    al = [alpha[:, lanes(j)] * hm_ref[e] for j, e in heads]
    rh = [rho[:, lanes(j)] * hm_ref[e] for j, e in heads]
    vh = [_split2(v[:, lanes(j)] * hm_ref[e]) for j, e in heads]
    wm = [_dot3_nt(jnp.concatenate([al[i], rh[i]], axis=0),
                   jnp.concatenate([beta[:, lanes(j)], kap[:, lanes(j)]], axis=0)) for i, (j, e) in enumerate(heads)]
    mm = [wm[i][:tm, :tm] * strict for i in range(nh)]
    nn = [_split2(wm[i][:tm, tm:] * strict) for i in range(nh)]
    pp = [_split2(wm[i][tm:, :tm] * incl) for i in range(nh)]
    qq = [_split2(wm[i][tm:, tm:] * incl) for i in range(nh)]
    tinv = [eye - mm[i] * mk_ref[3] for i in range(nh)]
    for lv in range(1, nlev):
        ts = [_split2(tinv[i]) for i in range(nh)]
        tmp = [_dot3s(ts[i], _split2(mm[i] * mk_ref[3 + lv])) for i in range(nh)]
        tinv = [tinv[i] - _dot3s(_split2(tmp[i]), ts[i]) for i in range(nh)]
    ts = [_split2(tinv[i]) for i in range(nh)]
    ap = [_dot3s(ts[i], _split2(al[i])) for i in range(nh)]
    nv = [_dot3s(nn[i], vh[i]) for i in range(nh)]
    cc = [_dot3s(ts[i], _split2(nv[i])) for i in range(nh)]
    qv = [_dot3s(qq[i], vh[i]) for i in range(nh)]
    rp = [rh[i] - _dot3s(pp[i], _split2(ap[i])) for i in range(nh)]
    y0 = [qv[i] - _dot3s(pp[i], _split2(cc[i])) for i in range(nh)]
    for j in range(RWKV_PAIRS):
        ap_o[:, lanes(j)] = ap[2 * j] + ap[2 * j + 1]
        rp_o[:, lanes(j)] = rp[2 * j] + rp[2 * j + 1]
        cc_o[:, lanes(j)] = cc[2 * j] + cc[2 * j + 1]
        y0_o[:, lanes(j)] = y0[2 * j] + y0[2 * j + 1]


def _rwkv_tables(tm, blk):
    t = np.arange(tm)
    same = (t[:, None] // blk) == (t[None, :] // blk)
    tril = (same & (t[None, :] <= t[:, None])).astype(np.float32)
    masks = [same & (t[None, :] < t[:, None]), same & (t[None, :] <= t[:, None]), np.eye(tm, dtype=bool)]
    n = 2
    while n <= blk:
        m = n // 2
        masks.append(((t[:, None] // n) == (t[None, :] // n)) & ((t[:, None] % n) >= m) & ((t[None, :] % n) < m))
        n *= 2
    nlev = len(masks) - 3
    hm = np.stack([(np.arange(LANES) < RWKV_HEAD), (np.arange(LANES) >= RWKV_HEAD)]).astype(np.float32)
    return (jnp.asarray(np.concatenate([tril, tril, tril], axis=1), BF16),
            jnp.asarray(np.stack(masks).astype(np.float32)), jnp.asarray(hm.reshape(2, 1, LANES)), nlev)


def _rwkv_prep(p3, prev, cs, nseq, t):
    m = nseq * t
    tm = RWKV_TM
    blk = min(RWKV_BLOCK, t)
    chained = t >= tm
    tril3, masks, hm, nlev = _rwkv_tables(tm, blk)
    w = RWKV_WIDTH
    row = lambda i: (i, 0)
    whole = lambda i: (0, 0)
    whole3 = lambda i: (0, 0, 0)
    vec = pl.BlockSpec((1, w), whole)
    if chained:
        tps = t // tm
        lead = [pl.BlockSpec((tm, RWKV_SHIFT_WIDTH), row),
                pl.BlockSpec((SUBLANES, RWKV_SHIFT_WIDTH),
                             lambda i: (jnp.maximum(i * (tm // SUBLANES) - 1, 0), 0)),
                pl.BlockSpec((1, 1, RWKV_SHIFT_WIDTH), lambda i: (i // tps, 0, 0))]
        args = (p3, p3, prev)
    else:
        tps = 1
        lead = [pl.BlockSpec((tm, RWKV_SHIFT_WIDTH), row), pl.BlockSpec((tm, RWKV_SHIFT_WIDTH), row)]
        args = (p3, prev)
    return pl.pallas_call(
        functools.partial(_rwkv_prep_kernel, nlev=nlev, chained=chained, tiles_per_seq=tps),
        grid=(m // tm,),
        in_specs=lead + [
            pl.BlockSpec((1, RWKV_SHIFT_WIDTH), whole),
            vec, vec,
            pl.BlockSpec((LANES, w), whole), pl.BlockSpec((LANES, w), whole), pl.BlockSpec((LANES, w), whole),
            vec, vec, vec,
            pl.BlockSpec((2 * w, w), whole),
            pl.BlockSpec(tril3.shape, whole), pl.BlockSpec(masks.shape, whole3), pl.BlockSpec(hm.shape, whole3),
        ],
        out_specs=[pl.BlockSpec((tm, w), row)] * 10,
        out_shape=[jax.ShapeDtypeStruct((m, w), F32)] * 10,
        compiler_params=_cparams(("parallel",)),
        name="rwkv_prep",
    )(*args, cs["mu"], cs["w0"], cs["a0"], cs["w2"], cs["a2"], cs["g2"], cs["k_k"], cs["k_a"], cs["r_k"],
      cs["e2w"], tril3, masks, hm)


def _rwkv_rec_kernel(*refs, nbp, tc, blk, aliased):
    ap_ref, rp_ref, bt_ref, kp_ref, cc_ref, y0_ref, v_ref, gm_ref, s0_ref, bd_ref = refs[:10]
    y_ref, so_ref, s_ref = refs[10 + (1 if aliased else 0):]
    c = pl.program_id(1)
    hd = RWKV_HEAD
    pairs = [(n, j) for n in range(nbp) for j in range(RWKV_PAIRS)]

    @pl.when(c == 0)
    def _():
        zero = jnp.zeros((hd, hd), F32)
        for idx, (n, j) in enumerate(pairs):
            top = jnp.concatenate([s0_ref[n, 2 * j], zero], axis=1)
            bot = jnp.concatenate([zero, s0_ref[n, 2 * j + 1]], axis=1)
            s_ref[idx] = jnp.concatenate([top, bot], axis=0)

    bd = bd_ref[...]

    def block(bi, carry):
        rows = pl.ds(pl.multiple_of(bi * blk, blk), blk)
        gmats = []
        for idx, (n, j) in enumerate(pairs):
            ls = slice(j * LANES, (j + 1) * LANES)
            sh, sl = _split2(s_ref[idx])
            xh, xl = _split2(jnp.concatenate([ap_ref[n, rows, ls], rp_ref[n, rows, ls]], axis=0))
            gmats.append(_dot_nt(jnp.concatenate([xh, xh, xl], axis=1), jnp.concatenate([sh, sl, sh], axis=1)))
        upds = []
        for idx, (n, j) in enumerate(pairs):
            ls = slice(j * LANES, (j + 1) * LANES)
            gmat = gmats[idx]
            e = -gmat[:blk] - cc_ref[n, rows, ls]
            y_ref[n, rows, ls] = gmat[blk:] + y0_ref[n, rows, ls]
            evh, evl = _split2(jnp.concatenate([e, v_ref[n, rows, ls]], axis=0))
            bkh, bkl = _split2(jnp.concatenate([bt_ref[n, rows, ls], kp_ref[n, rows, ls]], axis=0))
            upds.append(_dot_tn(jnp.concatenate([evh, evh, evl], axis=0), jnp.concatenate([bkh, bkl, bkh], axis=0)))
        for idx, (n, j) in enumerate(pairs):
            ls = slice(j * LANES, (j + 1) * LANES)
            gl = gm_ref[n, rows, ls][blk - 1:blk, :]
            s_ref[idx] = (s_ref[idx] + upds[idx] * bd) * gl
        return carry

    lax.fori_loop(0, tc // blk, block, 0)

    @pl.when(c == pl.num_programs(1) - 1)
    def _():
        for idx, (n, j) in enumerate(pairs):
            s = s_ref[idx]
            so_ref[n, 2 * j] = s[:hd, :hd]
            so_ref[n, 2 * j + 1] = s[hd:, hd:]


def _rwkv_rec(arrs, s0_all, l, so_prev, cs, nseq, t, nbp=4, tc=128):
    w = RWKV_WIDTH
    tc = min(tc, t)
    blk = min(RWKV_BLOCK, t)
    depth = s0_all.shape[0]
    seq_map = lambda i, c: (i, c, 0)
    st_map = lambda i, c: (l, i, 0, 0, 0)
    blkspec = pl.BlockSpec((nbp, tc, w), seq_map)
    st = pl.BlockSpec((None, nbp, RWKV_HEADS, RWKV_HEAD, RWKV_HEAD), st_map)
    aliased = so_prev is not None
    in_specs = [blkspec] * 8 + [st, pl.BlockSpec((LANES, LANES), lambda i, c: (0, 0))]
    args = list(arrs) + [s0_all, cs["bd"]]
    aliases = {}
    if aliased:
        in_specs.append(pl.BlockSpec(memory_space=pl.ANY))
        args.append(so_prev)
        aliases = {10: 1}
    return pl.pallas_call(
        functools.partial(_rwkv_rec_kernel, nbp=nbp, tc=tc, blk=blk, aliased=aliased),
        grid=(nseq // nbp, t // tc),
        in_specs=in_specs,
        out_specs=[blkspec, st],
        out_shape=[jax.ShapeDtypeStruct((nseq, t, w), F32),
                   jax.ShapeDtypeStruct((depth, nseq, RWKV_HEADS, RWKV_HEAD, RWKV_HEAD), F32)],
        scratch_shapes=[pltpu.VMEM((nbp * RWKV_PAIRS, LANES, LANES), F32)],
        input_output_aliases=aliases,
        compiler_params=_cparams(("parallel", "arbitrary")),
        name="rwkv_rec",
    )(*args)


def _rwkv_post_kernel(y_ref, bonus_ref, g_ref, lnw_ref, lnb_ref, e2_ref, o_ref):
    y = y_ref[...]
    e2 = e2_ref[...]
    mean = _seg_sum(y, e2) * (1.0 / RWKV_HEAD)
    yc = y - mean
    var = _seg_sum(yc * yc, e2) * (1.0 / RWKV_HEAD)
    yn = yc * lax.rsqrt(var + RWKV_GN_EPS) * lnw_ref[...] + lnb_ref[...]
    o_ref[...] = (yn + bonus_ref[...]) * g_ref[...]


def _rwkv_post(y, bonus, g, cs, tm=512):
    m, w = y.shape
    tm = min(tm, m)
    row = lambda i: (i, 0)
    whole = lambda i: (0, 0)
    return pl.pallas_call(
        _rwkv_post_kernel,
        grid=(m // tm,),
        in_specs=[pl.BlockSpec((tm, w), row)] * 3 + [pl.BlockSpec((1, w), whole)] * 2
        + [pl.BlockSpec((2 * w, w), whole)],
        out_specs=pl.BlockSpec((tm, w), row),
        out_shape=jax.ShapeDtypeStruct((m, w), F32),
        compiler_params=_cparams(("parallel",)),
        name="rwkv_post",
    )(y, bonus, g, cs["ln_w"], cs["ln_b"], cs["e2w"])


def _rwkv_consts(mu, w0, w2, a0, a2, g2, k_k, k_a, r_k, ln_w, ln_b):
    w = RWKV_WIDTH
    seg = np.arange(w) // RWKV_HEAD
    e_w = (seg[:, None] == seg[None, :]).astype(np.float32)
    seg = np.arange(LANES) // RWKV_HEAD
    bd = (seg[:, None] == seg[None, :]).astype(np.float32)
    z64 = jnp.zeros((RWKV_HEAD, w), F32)
    return dict(
        mu=mu.reshape(1, -1), w0=w0.reshape(1, -1), a0=a0.reshape(1, -1),
        w2=jnp.concatenate([w2, z64], axis=0).astype(BF16),
        a2=jnp.concatenate([z64, a2], axis=0).astype(BF16),
        g2=g2.astype(BF16), k_k=k_k.reshape(1, -1), k_a=k_a.reshape(1, -1), r_k=r_k.reshape(1, -1),
        ln_w=ln_w.reshape(1, -1), ln_b=ln_b.reshape(1, -1),
        e2w=jnp.asarray(np.concatenate([e_w, e_w], axis=0), BF16),
        bd=jnp.asarray(bd, F32))


def _rwkv(p3, shift_l, wkv_all, l, so_prev, cs, nseq, t):
    if t >= RWKV_TM:
        prev = shift_l.reshape(nseq, 1, RWKV_SHIFT_WIDTH)
    else:
        z3 = p3.reshape(nseq, t, N_IN)[:, :, :RWKV_SHIFT_WIDTH]
        prev = jnp.concatenate([shift_l[:, None, :], z3[:, :-1]], axis=1).reshape(nseq * t, RWKV_SHIFT_WIDTH)
    outs = _rwkv_prep(p3, prev, cs, nseq, t)
    arrs = [a.reshape(nseq, t, RWKV_WIDTH) for a in outs[:8]]
    y, so = _rwkv_rec(arrs, wkv_all, l, so_prev, cs, nseq, t)
    return _rwkv_post(y.reshape(nseq * t, RWKV_WIDTH), outs[9], outs[8], cs), so


def _hgrn_tables(c):
    t = np.arange(c)
    j = np.arange(c)
    mats = [(j[None, :] <= t[:, None]), (j[None, :] > t[:, None])]
    masks = []
    n = 2
    while n <= c:
        m = n // 2
        bs = (t // n) * n
        hi_half = (t % n) >= m
        mats.append(hi_half[:, None] & (j[None, :] >= (bs + m)[:, None]) & (j[None, :] <= t[:, None]))
        mats.append((~hi_half)[:, None] & (j[None, :] > t[:, None]) & (j[None, :] <= (bs + m - 1)[:, None]))
        masks.append(((t[:, None] // n) == (t[None, :] // n)) & hi_half[:, None] & (~hi_half)[None, :])
        n *= 2
    w = np.concatenate(mats, axis=0).astype(np.float32)
    w3 = np.concatenate([w, w, w], axis=1)
    return jnp.asarray(w3, BF16), jnp.asarray(np.stack(masks).astype(np.float32)), len(masks)


def _hgrn_kernel(*refs, rb, c, nlev, chained, aliased, sub=2):
    q_ref, f_ref, i_ref, g_ref, s0_ref, lb_ref, nw_ref, w3_ref, mk_ref = refs[:9]
    o_ref, so_ref, st_ref = refs[9 + (1 if aliased else 0):]
    cidx = pl.program_id(1)
    hdim = HGRN_DIM

    if chained:
        @pl.when(cidx == 0)
        def _():
            for h in range(HGRN_HEADS):
                st_ref[h] = s0_ref[0, h].T

    w3 = w3_ref[...]
    row_i = lax.broadcasted_iota(jnp.int32, (c, c), 0)
    col_i = lax.broadcasted_iota(jnp.int32, (c, c), 1)
    eye = (row_i == col_i).astype(F32)

    heads = range(HGRN_HEADS)

    def lanes(h):
        return slice(h * hdim, (h + 1) * hdim)

    def block(it, carry):
        rows_s = [pl.ds(pl.multiple_of((it * sub + s) * c, c), c) for s in range(sub)]
        items = [(s, h) for s in range(sub) for h in heads]
        fg = {k: lb_ref[:, lanes(k[1])] + (1.0 - lb_ref[:, lanes(k[1])]) * _sigmoid(f_ref[rows_s[k[0]], lanes(k[1])])
              for k in items}
        kk = {k: 1.0 - fg[k] for k in items}
        qs = {}
        for k in items:
            q = q_ref[rows_s[k[0]], lanes(k[1])]
            qs[k] = q * _sigmoid(q)
        ex = {k: jnp.exp(_dot(w3, jnp.concatenate(_split3(jnp.log(fg[k])), axis=0))) for k in items}
        vb = {k: i_ref[rows_s[k[0]], lanes(k[1])].astype(BF16) for k in items}
        lev = {(k, lv): _dot_nt((qs[k] * ex[k][(2 + 2 * lv) * c:(3 + 2 * lv) * c]).astype(BF16),
                                (kk[k] * ex[k][(3 + 2 * lv) * c:(4 + 2 * lv) * c]).astype(BF16))
               for k in items for lv in range(nlev)}
        att = {}
        for k in items:
            a = eye * jnp.sum(qs[k] * kk[k], axis=-1, keepdims=True)
            for lv in range(nlev):
                a = a + jnp.where(mk_ref[lv] > 0.0, lev[(k, lv)], 0.0)
            att[k] = a.astype(BF16)
        intra = {k: _dot(att[k], vb[k]) for k in items}
        kv = {k: _dot_tn(vb[k], (kk[k] * ex[k][c:2 * c]).astype(BF16)) for k in items}
        qe = {k: (qs[k] * ex[k][0:c]).astype(BF16) for k in items}
        st = {h: st_ref[h] for h in heads} if chained else None
        for s in range(sub):
            if not chained:
                st = {h: s0_ref[it * sub + s, h].T for h in heads}
            inter = {h: _dot_nt(qe[(s, h)], st[h].astype(BF16)) for h in heads}
            st = {h: st[h] * ex[(s, h)][c - 1:c, :] + kv[(s, h)] for h in heads}
            if not chained:
                for h in heads:
                    so_ref[it * sub + s, h] = st[h].T
            for h in heads:
                o = inter[h] + intra[(s, h)]
                o = o * lax.rsqrt(jnp.mean(o * o, axis=-1, keepdims=True) + HGRN_EPS)
                gg = g_ref[rows_s[s], lanes(h)]
                o_ref[rows_s[s], lanes(h)] = o * nw_ref[:, lanes(h)] * (gg * _sigmoid(gg))
        if chained:
            for h in heads:
                st_ref[h] = st[h]
        return carry

    lax.fori_loop(0, rb // (c * sub), block, 0)

    if chained:
        @pl.when(cidx == pl.num_programs(1) - 1)
        def _():
            for h in range(HGRN_HEADS):
                so_ref[0, h] = st_ref[h].T


def _hgrn(p3, s0_all, l, so_prev, lb, nw, nseq, t):
    rows = nseq * t
    depth = s0_all.shape[0]
    c = math.gcd(t, 64)
    chained = t > c
    w3, masks, nlev = _hgrn_tables(c)
    w = HGRN_WIDTH
    if chained:
        rb = 256
        nblk = t // rb
        grid = (nseq, nblk)
        col = lambda cb: (lambda n, k: (n * nblk + k, cb))
        out_map = lambda n, k: (n * nblk + k, 0)
        nsb = 1
    else:
        nsb = 16
        rb = nsb * c
        grid = (rows // rb, 1)
        col = lambda cb: (lambda n, k: (n, cb))
        out_map = lambda n, k: (n, 0)
    st_spec = pl.BlockSpec((None, nsb, HGRN_HEADS, HGRN_DIM, HGRN_DIM), lambda n, k: (l, n, 0, 0, 0))
    whole2 = lambda n, k: (0, 0)
    whole3 = lambda n, k: (0, 0, 0)
    aliased = so_prev is not None
    in_specs = [pl.BlockSpec((rb, w), col(4)), pl.BlockSpec((rb, w), col(5)), pl.BlockSpec((rb, w), col(6)),
                pl.BlockSpec((rb, w), col(7)), st_spec,
                pl.BlockSpec((1, w), whole2), pl.BlockSpec((1, w), whole2),
                pl.BlockSpec(w3.shape, whole2), pl.BlockSpec(masks.shape, whole3)]
    args = [p3, p3, p3, p3, s0_all, lb, nw, w3, masks]
    aliases = {}
    if aliased:
        in_specs.append(pl.BlockSpec(memory_space=pl.ANY))
        args.append(so_prev)
        aliases = {9: 1}
    return pl.pallas_call(
        functools.partial(_hgrn_kernel, rb=rb, c=c, nlev=nlev, chained=chained, aliased=aliased),
        grid=grid,
        in_specs=in_specs,
        out_specs=[pl.BlockSpec((rb, w), out_map), st_spec],
        out_shape=[jax.ShapeDtypeStruct((rows, w), F32),
                   jax.ShapeDtypeStruct((depth, nseq, HGRN_HEADS, HGRN_DIM, HGRN_DIM), F32)],
        scratch_shapes=[pltpu.VMEM((HGRN_HEADS, HGRN_DIM, HGRN_DIM), F32)],
        input_output_aliases=aliases,
        compiler_params=_cparams(("parallel", "arbitrary")),
        name="hgrn_chain" if chained else "hgrn_tile",
    )(*args)


def _run(x, nseq, t, states, lw, ffw, norm_final):
    s5_re0, s5_im0, shift0, wkv0, hgrn0 = states
    depth = len(lw)
    s5_re0 = s5_re0.reshape(depth, nseq, S5_LANES)
    s5_im0 = s5_im0.reshape(depth, nseq, S5_LANES)
    s5_re_l, s5_im_l, shift_l = [], [], []
    wkv_out = None
    hgrn_out = None
    for l in range(depth):
        w = lw[l]
        x = _ffn(x, w["norm_ffn1"], ffw["f1g"], ffw["f1u"], ffw["f1d"], l)
        p3 = _inproj(x, w["norm_mix"], w["w_in"])
        y_s5, s_re, s_im = _s5(p3, s5_re0, s5_im0, l, w["s5"], nseq, t)
        y_rw, wkv_out = _rwkv(p3, shift0[l], wkv0, l, wkv_out, w["rwkv"], nseq, t)
        y_hg, hgrn_out = _hgrn(p3, hgrn0, l, hgrn_out, w["hgrn_lb"], w["hgrn_nw"], nseq, t)
        x = _outproj(x, y_s5, y_rw, y_hg, w["wo_s5"], w["wo_rw"], w["wo_hg"])
        x = _ffn(x, w["norm_ffn2"], ffw["f2g"], ffw["f2u"], ffw["f2d"], l)
        s5_re_l.append(s_re)
        s5_im_l.append(s_im)
        shift_l.append(p3.reshape(nseq, t, N_IN)[:, t - 1, :RWKV_SHIFT_WIDTH])
    y = _rmsnorm(x, norm_final.reshape(1, -1))
    return y, jnp.stack(s5_re_l), jnp.stack(s5_im_l), jnp.stack(shift_l), wkv_out, hgrn_out


def kernel(x_prompt, x_sample, state_s5_re, state_s5_im, state_rwkv_shift, state_rwkv_wkv, state_hgrn, norm_ffn1, ffn1_w_gate, ffn1_w_up, ffn1_w_down, norm_mix, w_in, s5_a_re, s5_a_im, s5_log_dt, s5_b_re, s5_b_im, s5_c_re, s5_c_im, s5_d, s5_w_glu, s5_b_glu, rwkv_mu, rwkv_w0, rwkv_w2, rwkv_a0, rwkv_a2, rwkv_g2, rwkv_k_k, rwkv_k_a, rwkv_r_k, rwkv_ln_w, rwkv_ln_b, hgrn_lb_raw, hgrn_norm_w, w_out, norm_ffn2, ffn2_w_gate, ffn2_w_up, ffn2_w_down, norm_final):
    depth = w_in.shape[0]
    nb, seq, d = x_prompt.shape
    ns, dseq, _ = x_sample.shape

    p_lb = jax.nn.softmax(hgrn_lb_raw.astype(F32), axis=0)
    lower_bounds = jnp.cumsum(p_lb, axis=0) - p_lb[0]

    o_rw = S5_WIDTH
    o_hg = S5_WIDTH + RWKV_SHIFT_WIDTH
    lw = []
    for l in range(depth):
        wi = w_in[l]
        wo = w_out[l].astype(BF16)
        lw.append(dict(
            norm_ffn1=norm_ffn1[l].reshape(1, -1), norm_mix=norm_mix[l].reshape(1, -1),
            norm_ffn2=norm_ffn2[l].reshape(1, -1),
            w_in=jnp.concatenate([wi[:, o_rw:o_hg], wi[:, :o_rw], wi[:, o_hg:]], axis=1).astype(BF16),
            wo_s5=wo[:S5_WIDTH], wo_rw=wo[S5_WIDTH:S5_WIDTH + RWKV_WIDTH], wo_hg=wo[S5_WIDTH + RWKV_WIDTH:],
            s5=_s5_consts(s5_a_re[l], s5_a_im[l], s5_log_dt[l], s5_b_re[l], s5_b_im[l], s5_c_re[l], s5_c_im[l],
                          s5_d[l], s5_w_glu[l], s5_b_glu[l]),
            rwkv=_rwkv_consts(rwkv_mu[l], rwkv_w0[l], rwkv_w2[l], rwkv_a0[l], rwkv_a2[l], rwkv_g2[l],
                              rwkv_k_k[l], rwkv_k_a[l], rwkv_r_k[l], rwkv_ln_w[l], rwkv_ln_b[l]),
            hgrn_lb=lower_bounds[l].reshape(1, -1), hgrn_nw=hgrn_norm_w[l].reshape(1, -1),
        ))

    def zeros_like_state(s):
        return jnp.zeros((depth, nb) + s.shape[2:], F32)

    p_states = tuple(zeros_like_state(s) for s in
                     (state_s5_re, state_s5_im, state_rwkv_shift, state_rwkv_wkv, state_hgrn))
    s_states = (state_s5_re, state_s5_im, state_rwkv_shift, state_rwkv_wkv, state_hgrn)
    ffw = dict(f1g=ffn1_w_gate.astype(BF16), f1u=ffn1_w_up.astype(BF16), f1d=ffn1_w_down.astype(BF16),
               f2g=ffn2_w_gate.astype(BF16), f2u=ffn2_w_up.astype(BF16), f2d=ffn2_w_down.astype(BF16))
    y_p, s5re_p, s5im_p, shift_p, wkv_p, hgrn_p = _run(
        x_prompt.reshape(nb * seq, d), nb, seq, p_states, lw, ffw, norm_final)
    y_s, s5re_s, s5im_s, shift_s, wkv_s, hgrn_s = _run(
        x_sample.reshape(ns * dseq, d), ns, dseq, s_states, lw, ffw, norm_final)
    return (y_p.reshape(nb, seq, d), y_s.reshape(ns, dseq, d), s5re_p, s5im_p, shift_p, wkv_p, hgrn_p,
            s5re_s, s5im_s, shift_s, wkv_s, hgrn_s)
```

```python
import functools
import math

import numpy as np
import jax
import jax.numpy as jnp
from jax import lax
from jax.experimental import pallas as pl
from jax.experimental.pallas import tpu as pltpu

F32 = jnp.float32
BF16 = jnp.bfloat16

NORM_EPS = 1e-6
RWKV_GN_EPS = 64e-5
HGRN_EPS = 1e-5

D_MODEL = 2048
S5_WIDTH = 512
S5_GROUP = 16
S5_GROUPS = 32
S5_STATE = 64
S5_LANES = S5_GROUPS * S5_STATE
S5_GB = 4
RWKV_WIDTH = 768
RWKV_HEAD = 64
RWKV_HEADS = 12
RWKV_PAIRS = 6
RWKV_SHIFT_WIDTH = 2560
RWKV_BLOCK = 16
RWKV_TM = 128
SEG_CHUNK = 256
HGRN_WIDTH = 768
HGRN_HEADS = 6
HGRN_DIM = 128
N_IN = 6144

SUBLANES = 8
LANES = 128
VMEM_LIMIT = 60 * 1024 * 1024


def _cparams(sem):
    return pltpu.CompilerParams(dimension_semantics=sem, vmem_limit_bytes=VMEM_LIMIT)


def _split2(x):
    hi = x.astype(BF16)
    lo = (x - hi.astype(F32)).astype(BF16)
    return hi, lo


def _split3(x):
    p1 = x.astype(BF16)
    r1 = x - p1.astype(F32)
    p2 = r1.astype(BF16)
    p3 = (r1 - p2.astype(F32)).astype(BF16)
    return p1, p2, p3


def _dot(a, b):
    return jnp.dot(a, b, preferred_element_type=F32)


def _dot_nt(a, b):
    return lax.dot_general(a, b, (((1,), (1,)), ((), ())), preferred_element_type=F32)


def _dot_tn(a, b):
    return lax.dot_general(a, b, (((0,), (0,)), ((), ())), preferred_element_type=F32)


def _sigmoid(x):
    return 1.0 / (1.0 + jnp.exp(-x))


def _ffn_kernel(x_ref, nw_ref, wg_ref, wu_ref, wd_ref, o_ref, h_ref):
    j = pl.program_id(1)

    @pl.when(j == 0)
    def _():
        x = x_ref[...]
        ms = jnp.mean(x * x, axis=-1, keepdims=True)
        h_ref[...] = (x * lax.rsqrt(ms + NORM_EPS) * nw_ref[...]).astype(BF16)
        o_ref[...] = jnp.zeros_like(o_ref)

    h = h_ref[...]
    g = _dot(h, wg_ref[...])
    u = _dot(h, wu_ref[...])
    o_ref[...] += _dot((g * _sigmoid(g) * u).astype(BF16), wd_ref[...])

    @pl.when(j == pl.num_programs(1) - 1)
    def _():
        o_ref[...] = x_ref[...] + 0.5 * o_ref[...]


def _ffn(x, nw, wg, wu, wd, l, tm=1024, tf=512):
    m, d = x.shape
    ff = wg.shape[2]
    return pl.pallas_call(
        _ffn_kernel,
        grid=(m // tm, ff // tf),
        in_specs=[
            pl.BlockSpec((tm, d), lambda i, j: (i, 0)),
            pl.BlockSpec((1, d), lambda i, j: (0, 0)),
            pl.BlockSpec((None, d, tf), lambda i, j: (l, 0, j)),
            pl.BlockSpec((None, d, tf), lambda i, j: (l, 0, j)),
            pl.BlockSpec((None, tf, d), lambda i, j: (l, j, 0)),
        ],
        out_specs=pl.BlockSpec((tm, d), lambda i, j: (i, 0)),
        out_shape=jax.ShapeDtypeStruct((m, d), F32),
        scratch_shapes=[pltpu.VMEM((tm, d), BF16)],
        compiler_params=_cparams(("parallel", "arbitrary")),
        name="ffn",
    )(x, nw, wg, wu, wd)


def _inproj_kernel(x_ref, nw_ref, w_ref, o_ref, h_ref):
    @pl.when(pl.program_id(1) == 0)
    def _():
        x = x_ref[...]
        ms = jnp.mean(x * x, axis=-1, keepdims=True)
        h_ref[...] = (x * lax.rsqrt(ms + NORM_EPS) * nw_ref[...]).astype(BF16)

    o_ref[...] = _dot(h_ref[...], w_ref[...])


def _inproj(x, nw, w, tm=1024, tn=1024):
    m, d = x.shape
    n = w.shape[1]
    return pl.pallas_call(
        _inproj_kernel,
        grid=(m // tm, n // tn),
        in_specs=[
            pl.BlockSpec((tm, d), lambda i, j: (i, 0)),
            pl.BlockSpec((1, d), lambda i, j: (0, 0)),
            pl.BlockSpec((d, tn), lambda i, j: (0, j)),
        ],
        out_specs=pl.BlockSpec((tm, tn), lambda i, j: (i, j)),
        out_shape=jax.ShapeDtypeStruct((m, n), F32),
        scratch_shapes=[pltpu.VMEM((tm, d), BF16)],
        compiler_params=_cparams(("parallel", "arbitrary")),
        name="inproj",
    )(x, nw, w)


def _outproj_kernel(x_ref, a_ref, b_ref, c_ref, wa_ref, wb_ref, wc_ref, o_ref):
    acc = _dot(a_ref[...].astype(BF16), wa_ref[...])
    acc += _dot(b_ref[...].astype(BF16), wb_ref[...])
    acc += _dot(c_ref[...].astype(BF16), wc_ref[...])
    o_ref[...] = x_ref[...] + acc


def _outproj(x, ya, yb, yc, wa, wb, wc, tm=512):
    m, d = x.shape
    row = lambda i: (i, 0)
    whole = lambda i: (0, 0)
    return pl.pallas_call(
        _outproj_kernel,
        grid=(m // tm,),
        in_specs=[
            pl.BlockSpec((tm, d), row),
            pl.BlockSpec((tm, ya.shape[1]), row),
            pl.BlockSpec((tm, yb.shape[1]), row),
            pl.BlockSpec((tm, yc.shape[1]), row),
            pl.BlockSpec(wa.shape, whole),
            pl.BlockSpec(wb.shape, whole),
            pl.BlockSpec(wc.shape, whole),
        ],
        out_specs=pl.BlockSpec((tm, d), row),
        out_shape=jax.ShapeDtypeStruct((m, d), F32),
        compiler_params=_cparams(("parallel",)),
        name="outproj",
    )(x, ya, yb, yc, wa, wb, wc)


def _rmsnorm_kernel(x_ref, nw_ref, o_ref):
    x = x_ref[...]
    ms = jnp.mean(x * x, axis=-1, keepdims=True)
    o_ref[...] = x * lax.rsqrt(ms + NORM_EPS) * nw_ref[...]


def _rmsnorm(x, nw, tm=512):
    m, d = x.shape
    return pl.pallas_call(
        _rmsnorm_kernel,
        grid=(m // tm,),
        in_specs=[pl.BlockSpec((tm, d), lambda i: (i, 0)), pl.BlockSpec((1, d), lambda i: (0, 0))],
        out_specs=pl.BlockSpec((tm, d), lambda i: (i, 0)),
        out_shape=jax.ShapeDtypeStruct((m, d), F32),
        compiler_params=_cparams(("parallel",)),
        name="final_norm",
    )(x, nw)


def _s5_kernel(u_ref, x0r_ref, x0i_ref, b3r_ref, b3i_ref, cr_ref, ci_ref, cst_ref, d_ref, wglu_ref,
               bglu_ref, y_ref, xlr_ref, xli_ref, xr_ref, xi_ref, cr_s, ci_s, *, tr, chained):
    c = pl.program_id(1)
    u = u_ref[...]
    uh, ul = _split2(u)
    for gb in range(S5_GB):
        sl = slice(gb * LANES, (gb + 1) * LANES)
        osl = slice(gb * 512, (gb + 1) * 512)
        if chained:
            xr_ref[:, osl] = _dot(uh[:, sl], b3r_ref[gb, 0:LANES, :])
            xi_ref[:, osl] = _dot(uh[:, sl], b3i_ref[gb, 0:LANES, :])
        else:
            lhs = jnp.concatenate([uh[:, sl], ul[:, sl], uh[:, sl]], axis=1)
            xr_ref[:, osl] = _dot(lhs, b3r_ref[gb])
            xi_ref[:, osl] = _dot(lhs, b3i_ref[gb])

    if chained:
        @pl.when(c == 0)
        def _():
            cr_s[...] = x0r_ref[0]
            ci_s[...] = x0i_ref[0]

    a1r, a1i, a2r, a2i, a4r, a4i, pr, pi = [cst_ref[i] for i in range(8)]

    def tile(j, carry):
        rows = pl.ds(pl.multiple_of(j * SUBLANES, SUBLANES), SUBLANES)
        xr = xr_ref[rows, :]
        xi = xi_ref[rows, :]
        for k, ar, ai in ((1, a1r, a1i), (2, a2r, a2i), (4, a4r, a4i)):
            sr = pltpu.roll(xr, k, 0)
            si = pltpu.roll(xi, k, 0)
            xr, xi = xr + ar * sr - ai * si, xi + ar * si + ai * sr
        if chained:
            x0r = cr_s[...]
            x0i = ci_s[...]
        else:
            x0r = x0r_ref[pl.ds(j, 1), :]
            x0i = x0i_ref[pl.ds(j, 1), :]
        x0r = jnp.broadcast_to(x0r, xr.shape)
        x0i = jnp.broadcast_to(x0i, xr.shape)
        xr, xi = xr + pr * x0r - pi * x0i, xi + pr * x0i + pi * x0r
        xr_ref[rows, :] = xr
        xi_ref[rows, :] = xi
        if chained:
            cr_s[...] = xr[SUBLANES - 1:SUBLANES, :]
            ci_s[...] = xi[SUBLANES - 1:SUBLANES, :]
        else:
            xlr_ref[pl.ds(j, 1), :] = xr[SUBLANES - 1:SUBLANES, :]
            xli_ref[pl.ds(j, 1), :] = xi[SUBLANES - 1:SUBLANES, :]
        return carry

    lax.fori_loop(0, tr // SUBLANES, tile, 0)

    if chained:
        @pl.when(c == pl.num_programs(1) - 1)
        def _():
            xlr_ref[0] = cr_s[...]
            xli_ref[0] = ci_s[...]

    ys = []
    for gb in range(S5_GB):
        osl = slice(gb * 512, (gb + 1) * 512)
        ys.append(_dot(xr_ref[:, osl].astype(BF16), cr_ref[gb]) + _dot(xi_ref[:, osl].astype(BF16), ci_ref[gb]))
    y = jnp.concatenate(ys, axis=1) + d_ref[...] * u
    y = 0.5 * y * (1.0 + jnp.tanh(math.sqrt(2.0 / math.pi) * (y + 0.044715 * (y * y * y))))
    z = _dot(y.astype(BF16), wglu_ref[...]) + bglu_ref[...]
    y_ref[...] = y * _sigmoid(z)


def _s5(p3, x0r, x0i, l, cs, nseq, t, tr=512):
    rows = nseq * t
    depth = x0r.shape[0]
    chained = t > SUBLANES
    if chained:
        grid = (nseq, t // tr)
        row_map = lambda n, c: (n * (t // tr) + c, 5)
        out_map = lambda n, c: (n * (t // tr) + c, 0)
        x0r = x0r.reshape(depth, nseq, 1, S5_LANES)
        x0i = x0i.reshape(depth, nseq, 1, S5_LANES)
        in_st = pl.BlockSpec((None, 1, 1, S5_LANES), lambda n, c: (l, n, 0, 0))
        st_spec = pl.BlockSpec((1, 1, S5_LANES), lambda n, c: (n, 0, 0))
        st_shape = jax.ShapeDtypeStruct((nseq, 1, S5_LANES), F32)
    else:
        assert t == SUBLANES
        grid = (rows // tr, 1)
        row_map = lambda n, c: (n, 5)
        out_map = lambda n, c: (n, 0)
        in_st = pl.BlockSpec((None, tr // SUBLANES, S5_LANES), lambda n, c: (l, n, 0))
        st_spec = pl.BlockSpec((tr // SUBLANES, S5_LANES), lambda n, c: (n, 0))
        st_shape = jax.ShapeDtypeStruct((nseq, S5_LANES), F32)
    whole3 = lambda n, c: (0, 0, 0)
    whole2 = lambda n, c: (0, 0)
    y, xlr, xli = pl.pallas_call(
        functools.partial(_s5_kernel, tr=tr, chained=chained),
        grid=grid,
        in_specs=[
            pl.BlockSpec((tr, S5_WIDTH), row_map),
            in_st, in_st,
            pl.BlockSpec(cs["b3r"].shape, whole3),
            pl.BlockSpec(cs["b3i"].shape, whole3),
            pl.BlockSpec(cs["cr"].shape, whole3),
            pl.BlockSpec(cs["ci"].shape, whole3),
            pl.BlockSpec(cs["scan"].shape, whole3),
            pl.BlockSpec((1, S5_WIDTH), whole2),
            pl.BlockSpec((S5_WIDTH, S5_WIDTH), whole2),
            pl.BlockSpec((1, S5_WIDTH), whole2),
        ],
        out_specs=[pl.BlockSpec((tr, S5_WIDTH), out_map), st_spec, st_spec],
        out_shape=[jax.ShapeDtypeStruct((rows, S5_WIDTH), F32), st_shape, st_shape],
        scratch_shapes=[pltpu.VMEM((tr, S5_LANES), F32), pltpu.VMEM((tr, S5_LANES), F32),
                        pltpu.VMEM((1, S5_LANES), F32), pltpu.VMEM((1, S5_LANES), F32)],
        compiler_params=_cparams(("parallel", "arbitrary")),
        name="s5_chain" if chained else "s5_tile",
    )(p3, x0r, x0i, cs["b3r"], cs["b3i"], cs["cr"], cs["ci"], cs["scan"], cs["d"], cs["wglu"], cs["bglu"])
    return y, xlr.reshape(nseq, S5_GROUPS, S5_STATE), xli.reshape(nseq, S5_GROUPS, S5_STATE)


def _s5_consts(a_re, a_im, log_dt, b_re, b_im, c_re, c_im, d, w_glu, b_glu):
    g, p, h = S5_GROUPS, S5_STATE, S5_GROUP
    dt = jnp.exp(log_dt)[:, None]
    mag = jnp.exp(a_re * dt)
    abr = mag * jnp.cos(a_im * dt)
    abi = mag * jnp.sin(a_im * dt)
    den = a_re * a_re + a_im * a_im
    qr = ((abr - 1.0) * a_re + abi * a_im) / den
    qi = (abi * a_re - (abr - 1.0) * a_im) / den
    bbr = qr[..., None] * b_re - qi[..., None] * b_im
    bbi = qr[..., None] * b_im + qi[..., None] * b_re
    pr, pi = [abr], [abi]
    for _ in range(SUBLANES - 1):
        pr, pi = pr + [pr[-1] * abr - pi[-1] * abi], pi + [pr[-1] * abi + pi[-1] * abr]
    pr = jnp.stack(pr).reshape(SUBLANES, g * p)
    pi = jnp.stack(pi).reshape(SUBLANES, g * p)
    row = jnp.arange(SUBLANES)[:, None]

    def lvl(k):
        return [jnp.where(row >= k, pr[k - 1][None, :], 0.0), jnp.where(row >= k, pi[k - 1][None, :], 0.0)]

    scan = jnp.stack(lvl(1) + lvl(2) + lvl(4) + [pr, pi]).astype(F32)
    eye = jnp.eye(SUBLANES, dtype=F32)

    def in_blocks(x):
        xx = x.reshape(S5_GB, SUBLANES, p, h).transpose(0, 1, 3, 2)
        blk = (xx[:, :, :, None, :] * eye[None, :, None, :, None]).reshape(S5_GB, SUBLANES * h, SUBLANES * p)
        hi = blk.astype(BF16)
        lo = (blk - hi.astype(F32)).astype(BF16)
        return jnp.concatenate([hi, hi, lo], axis=1)

    def out_blocks(x):
        xx = x.reshape(S5_GB, SUBLANES, h, p).transpose(0, 1, 3, 2)
        return (xx[:, :, :, None, :] * eye[None, :, None, :, None]).reshape(
            S5_GB, SUBLANES * p, SUBLANES * h).astype(BF16)

    return dict(b3r=in_blocks(bbr), b3i=in_blocks(bbi),
                cr=out_blocks(c_re), ci=out_blocks(-c_im), scan=scan,
                d=d.reshape(1, -1), wglu=w_glu.astype(BF16), bglu=b_glu.reshape(1, -1))


def _seg_sum(x, e2):
    hi, lo = _split2(x)
    cw = e2.shape[1]
    return jnp.concatenate(
        [_dot(jnp.concatenate([hi[:, c:c + cw], lo[:, c:c + cw]], axis=1), e2) for c in range(0, x.shape[1], cw)],
        axis=1)


def _rwkv_prep_kernel(*refs, nlev, chained, tiles_per_seq):
    if chained:
        z_ref, prev_ref, sh_ref = refs[:3]
        refs = refs[3:]
    else:
        z_ref, zp_ref = refs[:2]
        refs = refs[2:]
    (mu_ref, w0_ref, a0_ref, w2_ref, a2_ref, g2_ref, kk_ref, ka_ref, rk_ref, e2_ref, tril_ref, mk_ref, hm_ref,
     ap_o, rp_o, bt_o, kp_o, cc_o, y0_o, v_o, gm_o, g_o, bonus_o) = refs
    tm = RWKV_TM
    z = z_ref[...]
    if chained:
        first = (pl.program_id(0) % tiles_per_seq) == 0
        prev_row = jnp.where(first, sh_ref[0], prev_ref[SUBLANES - 1:SUBLANES, :])
        rid = lax.broadcasted_iota(jnp.int32, z.shape, 0)
        zp = jnp.where(rid == 0, prev_row, pltpu.roll(z, 1, 0))
    else:
        zp = zp_ref[...]
    zm = z + (zp - z) * mu_ref[...]
    w = RWKV_WIDTH
    r = zm[:, 0:w]
    k = zm[:, w:2 * w]
    v = zm[:, 2 * w:3 * w]
    wa = zm[:, 3 * w:3 * w + LANES]
    gi = zm[:, 3 * w + LANES:3 * w + 2 * LANES]
    e2 = e2_ref[...]
    wl = w0_ref[...] + _dot(jnp.tanh(wa).astype(BF16), w2_ref[...])
    sp = jnp.maximum(-wl, 0.0) + jnp.log(1.0 + jnp.exp(-jnp.abs(wl)))
    ld = -jnp.exp(-sp - 0.5)
    a = _sigmoid(a0_ref[...] + _dot(wa.astype(BF16), a2_ref[...]))
    g_o[...] = _dot(_sigmoid(gi).astype(BF16), g2_ref[...])
    kk = k * kk_ref[...]
    kk = kk * lax.rsqrt(jnp.maximum(_seg_sum(kk * kk, e2), 1e-24))
    k2 = k * (1.0 + (a - 1.0) * ka_ref[...])
    b = kk * a
    bonus_o[...] = _seg_sum(r * k2 * rk_ref[...], e2) * v
    v_o[...] = v

    p1, p2, p3 = _split3(ld)
    cum = _dot(tril_ref[...], jnp.concatenate([p1, p2, p3], axis=0))
    gam = jnp.exp(cum)
    ginv = jnp.exp(-cum)
    gm_o[...] = gam
    alpha = kk * jnp.exp(cum - ld)
    beta = b * ginv
    kap = k2 * ginv
    rho = r * gam
    bt_o[...] = beta
    kp_o[...] = kap

    strict = mk_ref[0]
    incl = mk_ref[1]
    eye = mk_ref[2]
    heads = [(j, e) for j in range(RWKV_PAIRS) for e in range(2)]
    nh = len(heads)

    def lanes(j):
        return slice(j * LANES, (j + 1) * LANES)

    al = [alpha[:, lanes(j)] * hm_ref[e] for j, e in heads]
    rh = [rho[:, lanes(j)] * hm_ref[e] for j, e in heads]
    vh = [(v[:, lanes(j)] * hm_ref[e]).astype(BF16) for j, e in heads]
    yk = [jnp.concatenate([beta[:, lanes(j)], kap[:, lanes(j)]], axis=0).astype(BF16) for j in range(RWKV_PAIRS)]
    wu = [_dot_nt(jnp.concatenate(_split2(al[i]), axis=1), jnp.concatenate([yk[j], yk[j]], axis=1))
          for i, (j, e) in enumerate(heads)]
    wl = [_dot_nt(rh[i].astype(BF16), yk[j]) for i, (j, e) in enumerate(heads)]
    mm = [wu[i][:, :tm] * strict for i in range(nh)]
    nn = [wu[i][:, tm:] * strict for i in range(nh)]
    pp = [(wl[i][:, :tm] * incl).astype(BF16) for i in range(nh)]
    qq = [(wl[i][:, tm:] * incl).astype(BF16) for i in range(nh)]
    tinv = [eye - mm[i] * mk_ref[3] for i in range(nh)]
    for lv in range(1, nlev):
        th = [tinv[i].astype(BF16) for i in range(nh)]
        tmp = [_dot(jnp.concatenate([th[i], th[i]], axis=1),
                    jnp.concatenate(_split2(mm[i] * mk_ref[3 + lv]), axis=0)) for i in range(nh)]
        tinv = [tinv[i] - _dot(jnp.concatenate(_split2(tmp[i]), axis=1),
                               jnp.concatenate([th[i], th[i]], axis=0)) for i in range(nh)]
    th = [tinv[i].astype(BF16) for i in range(nh)]
    nv = [_dot(jnp.concatenate(_split2(nn[i]), axis=1), jnp.concatenate([vh[i], vh[i]], axis=0)) for i in range(nh)]
    qv = [_dot(qq[i], vh[i]) for i in range(nh)]
    tac = [_dot(jnp.concatenate([th[i], th[i]], axis=1),
                jnp.concatenate(_split2(jnp.concatenate([al[i], nv[i]], axis=1)), axis=0)) for i in range(nh)]
    pac = [_dot(pp[i], tac[i].astype(BF16)) for i in range(nh)]
    ap = [tac[i][:, :LANES] for i in range(nh)]
    cc = [tac[i][:, LANES:] for i in range(nh)]
    rp = [rh[i] - pac[i][:, :LANES] for i in range(nh)]
    y0 = [qv[i] - pac[i][:, LANES:] for i in range(nh)]
    for j in range(RWKV_PAIRS):
        ap_o[:, lanes(j)] = ap[2 * j] + ap[2 * j + 1]
        rp_o[:, lanes(j)] = rp[2 * j] + rp[2 * j + 1]
        cc_o[:, lanes(j)] = cc[2 * j] + cc[2 * j + 1]
        y0_o[:, lanes(j)] = y0[2 * j] + y0[2 * j + 1]


def _rwkv_tables(tm, blk):
    t = np.arange(tm)
    same = (t[:, None] // blk) == (t[None, :] // blk)
    tril = (same & (t[None, :] <= t[:, None])).astype(np.float32)
    masks = [same & (t[None, :] < t[:, None]), same & (t[None, :] <= t[:, None]), np.eye(tm, dtype=bool)]
    n = 2
    while n <= blk:
        m = n // 2
        masks.append(((t[:, None] // n) == (t[None, :] // n)) & ((t[:, None] % n) >= m) & ((t[None, :] % n) < m))
        n *= 2
    nlev = len(masks) - 3
    hm = np.stack([(np.arange(LANES) < RWKV_HEAD), (np.arange(LANES) >= RWKV_HEAD)]).astype(np.float32)
    return (jnp.asarray(np.concatenate([tril, tril, tril], axis=1), BF16),
            jnp.asarray(np.stack(masks).astype(np.float32)), jnp.asarray(hm.reshape(2, 1, LANES)), nlev)


def _rwkv_prep(p3, prev, cs, nseq, t):
    m = nseq * t
    tm = RWKV_TM
    blk = min(RWKV_BLOCK, t)
    chained = t >= tm
    tril3, masks, hm, nlev = _rwkv_tables(tm, blk)
    w = RWKV_WIDTH
    row = lambda i: (i, 0)
    whole = lambda i: (0, 0)
    whole3 = lambda i: (0, 0, 0)
    vec = pl.BlockSpec((1, w), whole)
    if chained:
        tps = t // tm
        lead = [pl.BlockSpec((tm, RWKV_SHIFT_WIDTH), row),
                pl.BlockSpec((SUBLANES, RWKV_SHIFT_WIDTH),
                             lambda i: (jnp.maximum(i * (tm // SUBLANES) - 1, 0), 0)),
                pl.BlockSpec((1, 1, RWKV_SHIFT_WIDTH), lambda i: (i // tps, 0, 0))]
        args = (p3, p3, prev)
    else:
        tps = 1
        lead = [pl.BlockSpec((tm, RWKV_SHIFT_WIDTH), row), pl.BlockSpec((tm, RWKV_SHIFT_WIDTH), row)]
        args = (p3, prev)
    return pl.pallas_call(
        functools.partial(_rwkv_prep_kernel, nlev=nlev, chained=chained, tiles_per_seq=tps),
        grid=(m // tm,),
        in_specs=lead + [
            pl.BlockSpec((1, RWKV_SHIFT_WIDTH), whole),
            vec, vec,
            pl.BlockSpec((LANES, w), whole), pl.BlockSpec((LANES, w), whole), pl.BlockSpec((LANES, w), whole),
            vec, vec, vec,
            pl.BlockSpec((2 * SEG_CHUNK, SEG_CHUNK), whole),
            pl.BlockSpec(tril3.shape, whole), pl.BlockSpec(masks.shape, whole3), pl.BlockSpec(hm.shape, whole3),
        ],
        out_specs=[pl.BlockSpec((tm, w), row)] * 10,
        out_shape=[jax.ShapeDtypeStruct((m, w), F32)] * 10,
        compiler_params=_cparams(("parallel",)),
        name="rwkv_prep",
    )(*args, cs["mu"], cs["w0"], cs["a0"], cs["w2"], cs["a2"], cs["g2"], cs["k_k"], cs["k_a"], cs["r_k"],
      cs["e2w"], tril3, masks, hm)


def _rwkv_rec_kernel(*refs, nbp, tc, blk, aliased):
    ap_ref, rp_ref, bt_ref, kp_ref, cc_ref, y0_ref, v_ref, gm_ref, s0_ref, bd_ref = refs[:10]
    y_ref, so_ref, s_ref = refs[10 + (1 if aliased else 0):]
    c = pl.program_id(1)
    hd = RWKV_HEAD
    pairs = [(n, j) for n in range(nbp) for j in range(RWKV_PAIRS)]

    @pl.when(c == 0)
    def _():
        zero = jnp.zeros((hd, hd), F32)
        for idx, (n, j) in enumerate(pairs):
            top = jnp.concatenate([s0_ref[n, 2 * j], zero], axis=1)
            bot = jnp.concatenate([zero, s0_ref[n, 2 * j + 1]], axis=1)
            s_ref[idx] = jnp.concatenate([top, bot], axis=0)

    bd = bd_ref[...]

    def block(bi, carry):
        rows = pl.ds(pl.multiple_of(bi * blk, blk), blk)
        gmats = []
        for idx, (n, j) in enumerate(pairs):
            ls = slice(j * LANES, (j + 1) * LANES)
            sh, sl = _split2(s_ref[idx])
            xh, xl = _split2(jnp.concatenate([ap_ref[n, rows, ls], rp_ref[n, rows, ls]], axis=0))
            gmats.append(_dot_nt(jnp.concatenate([xh, xh, xl], axis=1), jnp.concatenate([sh, sl, sh], axis=1)))
        upds = []
        for idx, (n, j) in enumerate(pairs):
            ls = slice(j * LANES, (j + 1) * LANES)
            gmat = gmats[idx]
            e = -gmat[:blk] - cc_ref[n, rows, ls]
            y_ref[n, rows, ls] = gmat[blk:] + y0_ref[n, rows, ls]
            evh, evl = _split2(jnp.concatenate([e, v_ref[n, rows, ls]], axis=0))
            bkh, bkl = _split2(jnp.concatenate([bt_ref[n, rows, ls], kp_ref[n, rows, ls]], axis=0))
            upds.append(_dot_tn(jnp.concatenate([evh, evh, evl], axis=0), jnp.concatenate([bkh, bkl, bkh], axis=0)))
        for idx, (n, j) in enumerate(pairs):
            ls = slice(j * LANES, (j + 1) * LANES)
            gl = gm_ref[n, rows, ls][blk - 1:blk, :]
            s_ref[idx] = (s_ref[idx] + upds[idx] * bd) * gl
        return carry

    lax.fori_loop(0, tc // blk, block, 0)

    @pl.when(c == pl.num_programs(1) - 1)
    def _():
        for idx, (n, j) in enumerate(pairs):
            s = s_ref[idx]
            so_ref[n, 2 * j] = s[:hd, :hd]
            so_ref[n, 2 * j + 1] = s[hd:, hd:]


def _rwkv_rec(arrs, s0_all, l, so_prev, cs, nseq, t, nbp=4, tc=128):
    w = RWKV_WIDTH
    tc = min(tc, t)
    blk = min(RWKV_BLOCK, t)
    depth = s0_all.shape[0]
    seq_map = lambda i, c: (i, c, 0)
    st_map = lambda i, c: (l, i, 0, 0, 0)
    blkspec = pl.BlockSpec((nbp, tc, w), seq_map)
    st = pl.BlockSpec((None, nbp, RWKV_HEADS, RWKV_HEAD, RWKV_HEAD), st_map)
    aliased = so_prev is not None
    in_specs = [blkspec] * 8 + [st, pl.BlockSpec((LANES, LANES), lambda i, c: (0, 0))]
    args = list(arrs) + [s0_all, cs["bd"]]
    aliases = {}
    if aliased:
        in_specs.append(pl.BlockSpec(memory_space=pl.ANY))
        args.append(so_prev)
        aliases = {10: 1}
    return pl.pallas_call(
        functools.partial(_rwkv_rec_kernel, nbp=nbp, tc=tc, blk=blk, aliased=aliased),
        grid=(nseq // nbp, t // tc),
        in_specs=in_specs,
        out_specs=[blkspec, st],
        out_shape=[jax.ShapeDtypeStruct((nseq, t, w), F32),
                   jax.ShapeDtypeStruct((depth, nseq, RWKV_HEADS, RWKV_HEAD, RWKV_HEAD), F32)],
        scratch_shapes=[pltpu.VMEM((nbp * RWKV_PAIRS, LANES, LANES), F32)],
        input_output_aliases=aliases,
        compiler_params=_cparams(("parallel", "arbitrary")),
        name="rwkv_rec",
    )(*args)


def _rwkv_post_kernel(y_ref, bonus_ref, g_ref, lnw_ref, lnb_ref, e2_ref, o_ref):
    y = y_ref[...]
    e2 = e2_ref[...]
    mean = _seg_sum(y, e2) * (1.0 / RWKV_HEAD)
    yc = y - mean
    var = _seg_sum(yc * yc, e2) * (1.0 / RWKV_HEAD)
    yn = yc * lax.rsqrt(var + RWKV_GN_EPS) * lnw_ref[...] + lnb_ref[...]
    o_ref[...] = (yn + bonus_ref[...]) * g_ref[...]


def _rwkv_post(y, bonus, g, cs, tm=512):
    m, w = y.shape
    tm = min(tm, m)
    row = lambda i: (i, 0)
    whole = lambda i: (0, 0)
    return pl.pallas_call(
        _rwkv_post_kernel,
        grid=(m // tm,),
        in_specs=[pl.BlockSpec((tm, w), row)] * 3 + [pl.BlockSpec((1, w), whole)] * 2
        + [pl.BlockSpec((2 * SEG_CHUNK, SEG_CHUNK), whole)],
        out_specs=pl.BlockSpec((tm, w), row),
        out_shape=jax.ShapeDtypeStruct((m, w), F32),
        compiler_params=_cparams(("parallel",)),
        name="rwkv_post",
    )(y, bonus, g, cs["ln_w"], cs["ln_b"], cs["e2w"])


def _rwkv_consts(mu, w0, w2, a0, a2, g2, k_k, k_a, r_k, ln_w, ln_b):
    w = RWKV_WIDTH
    seg = np.arange(SEG_CHUNK) // RWKV_HEAD
    e_w = (seg[:, None] == seg[None, :]).astype(np.float32)
    seg = np.arange(LANES) // RWKV_HEAD
    bd = (seg[:, None] == seg[None, :]).astype(np.float32)
    z64 = jnp.zeros((RWKV_HEAD, w), F32)
    return dict(
        mu=mu.reshape(1, -1), w0=w0.reshape(1, -1), a0=a0.reshape(1, -1),
        w2=jnp.concatenate([w2, z64], axis=0).astype(BF16),
        a2=jnp.concatenate([z64, a2], axis=0).astype(BF16),
        g2=g2.astype(BF16), k_k=k_k.reshape(1, -1), k_a=k_a.reshape(1, -1), r_k=r_k.reshape(1, -1),
        ln_w=ln_w.reshape(1, -1), ln_b=ln_b.reshape(1, -1),
        e2w=jnp.asarray(np.concatenate([e_w, e_w], axis=0), BF16),
        bd=jnp.asarray(bd, F32))


def _rwkv(p3, shift_l, wkv_all, l, so_prev, cs, nseq, t):
    if t >= RWKV_TM:
        prev = shift_l.reshape(nseq, 1, RWKV_SHIFT_WIDTH)
    else:
        z3 = p3.reshape(nseq, t, N_IN)[:, :, :RWKV_SHIFT_WIDTH]
        prev = jnp.concatenate([shift_l[:, None, :], z3[:, :-1]], axis=1).reshape(nseq * t, RWKV_SHIFT_WIDTH)
    outs = _rwkv_prep(p3, prev, cs, nseq, t)
    arrs = [a.reshape(nseq, t, RWKV_WIDTH) for a in outs[:8]]
    y, so = _rwkv_rec(arrs, wkv_all, l, so_prev, cs, nseq, t)
    return _rwkv_post(y.reshape(nseq * t, RWKV_WIDTH), outs[9], outs[8], cs), so


def _hgrn_tables(c):
    t = np.arange(c)
    j = np.arange(c)
    mats = [(j[None, :] <= t[:, None]), (j[None, :] > t[:, None])]
    masks = []
    n = 2
    while n <= c:
        m = n // 2
        bs = (t // n) * n
        hi_half = (t % n) >= m
        mats.append(hi_half[:, None] & (j[None, :] >= (bs + m)[:, None]) & (j[None, :] <= t[:, None]))
        mats.append((~hi_half)[:, None] & (j[None, :] > t[:, None]) & (j[None, :] <= (bs + m - 1)[:, None]))
        masks.append(((t[:, None] // n) == (t[None, :] // n)) & hi_half[:, None] & (~hi_half)[None, :])
        n *= 2
    w = np.concatenate(mats, axis=0).astype(np.float32)
    w3 = np.concatenate([w, w, w], axis=1)
    return jnp.asarray(w3, BF16), jnp.asarray(np.stack(masks).astype(np.float32)), len(masks)


def _hgrn_kernel(*refs, rb, c, nlev, chained, aliased, sub=2):
    q_ref, f_ref, i_ref, g_ref, s0_ref, lb_ref, nw_ref, w3_ref, mk_ref = refs[:9]
    o_ref, so_ref, st_ref = refs[9 + (1 if aliased else 0):]
    cidx = pl.program_id(1)
    hdim = HGRN_DIM

    if chained:
        @pl.when(cidx == 0)
        def _():
            for h in range(HGRN_HEADS):
                st_ref[h] = s0_ref[0, h].T

    w3 = w3_ref[...]
    row_i = lax.broadcasted_iota(jnp.int32, (c, c), 0)
    col_i = lax.broadcasted_iota(jnp.int32, (c, c), 1)
    eye = (row_i == col_i).astype(F32)

    heads = range(HGRN_HEADS)

    def lanes(h):
        return slice(h * hdim, (h + 1) * hdim)

    def block(it, carry):
        rows_s = [pl.ds(pl.multiple_of((it * sub + s) * c, c), c) for s in range(sub)]
        items = [(s, h) for s in range(sub) for h in heads]
        fg = {k: lb_ref[:, lanes(k[1])] + (1.0 - lb_ref[:, lanes(k[1])]) * _sigmoid(f_ref[rows_s[k[0]], lanes(k[1])])
              for k in items}
        kk = {k: 1.0 - fg[k] for k in items}
        qs = {}
        for k in items:
            q = q_ref[rows_s[k[0]], lanes(k[1])]
            qs[k] = q * _sigmoid(q)
        ex = {k: jnp.exp(_dot(w3, jnp.concatenate(_split3(jnp.log(fg[k])), axis=0))) for k in items}
        vb = {k: i_ref[rows_s[k[0]], lanes(k[1])].astype(BF16) for k in items}
        lev = {(k, lv): _dot_nt((qs[k] * ex[k][(2 + 2 * lv) * c:(3 + 2 * lv) * c]).astype(BF16),
                                (kk[k] * ex[k][(3 + 2 * lv) * c:(4 + 2 * lv) * c]).astype(BF16))
               for k in items for lv in range(nlev)}
        att = {}
        for k in items:
            a = eye * jnp.sum(qs[k] * kk[k], axis=-1, keepdims=True)
            for lv in range(nlev):
                a = a + jnp.where(mk_ref[lv] > 0.0, lev[(k, lv)], 0.0)
            att[k] = a.astype(BF16)
        intra = {k: _dot(att[k], vb[k]) for k in items}
        kv = {k: _dot_tn(vb[k], (kk[k] * ex[k][c:2 * c]).astype(BF16)) for k in items}
        qe = {k: (qs[k] * ex[k][0:c]).astype(BF16) for k in items}
        st = {h: st_ref[h] for h in heads} if chained else None
        for s in range(sub):
            if not chained:
                st = {h: s0_ref[it * sub + s, h].T for h in heads}
            inter = {h: _dot_nt(qe[(s, h)], st[h].astype(BF16)) for h in heads}
            st = {h: st[h] * ex[(s, h)][c - 1:c, :] + kv[(s, h)] for h in heads}
            if not chained:
                for h in heads:
                    so_ref[it * sub + s, h] = st[h].T
            for h in heads:
                o = inter[h] + intra[(s, h)]
                o = o * lax.rsqrt(jnp.mean(o * o, axis=-1, keepdims=True) + HGRN_EPS)
                gg = g_ref[rows_s[s], lanes(h)]
                o_ref[rows_s[s], lanes(h)] = o * nw_ref[:, lanes(h)] * (gg * _sigmoid(gg))
        if chained:
            for h in heads:
                st_ref[h] = st[h]
        return carry

    lax.fori_loop(0, rb // (c * sub), block, 0)

    if chained:
        @pl.when(cidx == pl.num_programs(1) - 1)
        def _():
            for h in range(HGRN_HEADS):
                so_ref[0, h] = st_ref[h].T


def _hgrn(p3, s0_all, l, so_prev, lb, nw, nseq, t):
    rows = nseq * t
    depth = s0_all.shape[0]
    c = math.gcd(t, 64)
    chained = t > c
    w3, masks, nlev = _hgrn_tables(c)
    w = HGRN_WIDTH
    if chained:
        rb = 256
        nblk = t // rb
        grid = (nseq, nblk)
        col = lambda cb: (lambda n, k: (n * nblk + k, cb))
        out_map = lambda n, k: (n * nblk + k, 0)
        nsb = 1
    else:
        nsb = 16
        rb = nsb * c
        grid = (rows // rb, 1)
        col = lambda cb: (lambda n, k: (n, cb))
        out_map = lambda n, k: (n, 0)
    st_spec = pl.BlockSpec((None, nsb, HGRN_HEADS, HGRN_DIM, HGRN_DIM), lambda n, k: (l, n, 0, 0, 0))
    whole2 = lambda n, k: (0, 0)
    whole3 = lambda n, k: (0, 0, 0)
    aliased = so_prev is not None
    in_specs = [pl.BlockSpec((rb, w), col(4)), pl.BlockSpec((rb, w), col(5)), pl.BlockSpec((rb, w), col(6)),
                pl.BlockSpec((rb, w), col(7)), st_spec,
                pl.BlockSpec((1, w), whole2), pl.BlockSpec((1, w), whole2),
                pl.BlockSpec(w3.shape, whole2), pl.BlockSpec(masks.shape, whole3)]
    args = [p3, p3, p3, p3, s0_all, lb, nw, w3, masks]
    aliases = {}
    if aliased:
        in_specs.append(pl.BlockSpec(memory_space=pl.ANY))
        args.append(so_prev)
        aliases = {9: 1}
    return pl.pallas_call(
        functools.partial(_hgrn_kernel, rb=rb, c=c, nlev=nlev, chained=chained, aliased=aliased),
        grid=grid,
        in_specs=in_specs,
        out_specs=[pl.BlockSpec((rb, w), out_map), st_spec],
        out_shape=[jax.ShapeDtypeStruct((rows, w), F32),
                   jax.ShapeDtypeStruct((depth, nseq, HGRN_HEADS, HGRN_DIM, HGRN_DIM), F32)],
        scratch_shapes=[pltpu.VMEM((HGRN_HEADS, HGRN_DIM, HGRN_DIM), F32)],
        input_output_aliases=aliases,
        compiler_params=_cparams(("parallel", "arbitrary")),
        name="hgrn_chain" if chained else "hgrn_tile",
    )(*args)


def _run(x, nseq, t, states, lw, ffw, norm_final):
    s5_re0, s5_im0, shift0, wkv0, hgrn0 = states
    depth = len(lw)
    s5_re0 = s5_re0.reshape(depth, nseq, S5_LANES)
    s5_im0 = s5_im0.reshape(depth, nseq, S5_LANES)
    s5_re_l, s5_im_l, shift_l = [], [], []
    wkv_out = None
    hgrn_out = None
    for l in range(depth):
        w = lw[l]
        x = _ffn(x, w["norm_ffn1"], ffw["f1g"], ffw["f1u"], ffw["f1d"], l)
        p3 = _inproj(x, w["norm_mix"], w["w_in"])
        y_s5, s_re, s_im = _s5(p3, s5_re0, s5_im0, l, w["s5"], nseq, t)
        y_rw, wkv_out = _rwkv(p3, shift0[l], wkv0, l, wkv_out, w["rwkv"], nseq, t)
        y_hg, hgrn_out = _hgrn(p3, hgrn0, l, hgrn_out, w["hgrn_lb"], w["hgrn_nw"], nseq, t)
        x = _outproj(x, y_s5, y_rw, y_hg, w["wo_s5"], w["wo_rw"], w["wo_hg"])
        x = _ffn(x, w["norm_ffn2"], ffw["f2g"], ffw["f2u"], ffw["f2d"], l)
        s5_re_l.append(s_re)
        s5_im_l.append(s_im)
        shift_l.append(p3.reshape(nseq, t, N_IN)[:, t - 1, :RWKV_SHIFT_WIDTH])
    y = _rmsnorm(x, norm_final.reshape(1, -1))
    return y, jnp.stack(s5_re_l), jnp.stack(s5_im_l), jnp.stack(shift_l), wkv_out, hgrn_out


def kernel(x_prompt, x_sample, state_s5_re, state_s5_im, state_rwkv_shift, state_rwkv_wkv, state_hgrn, norm_ffn1, ffn1_w_gate, ffn1_w_up, ffn1_w_down, norm_mix, w_in, s5_a_re, s5_a_im, s5_log_dt, s5_b_re, s5_b_im, s5_c_re, s5_c_im, s5_d, s5_w_glu, s5_b_glu, rwkv_mu, rwkv_w0, rwkv_w2, rwkv_a0, rwkv_a2, rwkv_g2, rwkv_k_k, rwkv_k_a, rwkv_r_k, rwkv_ln_w, rwkv_ln_b, hgrn_lb_raw, hgrn_norm_w, w_out, norm_ffn2, ffn2_w_gate, ffn2_w_up, ffn2_w_down, norm_final):
    depth = w_in.shape[0]
    nb, seq, d = x_prompt.shape
    ns, dseq, _ = x_sample.shape

    p_lb = jax.nn.softmax(hgrn_lb_raw.astype(F32), axis=0)
    lower_bounds = jnp.cumsum(p_lb, axis=0) - p_lb[0]

    o_rw = S5_WIDTH
    o_hg = S5_WIDTH + RWKV_SHIFT_WIDTH
    lw = []
    for l in range(depth):
        wi = w_in[l]
        wo = w_out[l].astype(BF16)
        lw.append(dict(
            norm_ffn1=norm_ffn1[l].reshape(1, -1), norm_mix=norm_mix[l].reshape(1, -1),
            norm_ffn2=norm_ffn2[l].reshape(1, -1),
            w_in=jnp.concatenate([wi[:, o_rw:o_hg], wi[:, :o_rw], wi[:, o_hg:]], axis=1).astype(BF16),
            wo_s5=wo[:S5_WIDTH], wo_rw=wo[S5_WIDTH:S5_WIDTH + RWKV_WIDTH], wo_hg=wo[S5_WIDTH + RWKV_WIDTH:],
            s5=_s5_consts(s5_a_re[l], s5_a_im[l], s5_log_dt[l], s5_b_re[l], s5_b_im[l], s5_c_re[l], s5_c_im[l],
                          s5_d[l], s5_w_glu[l], s5_b_glu[l]),
            rwkv=_rwkv_consts(rwkv_mu[l], rwkv_w0[l], rwkv_w2[l], rwkv_a0[l], rwkv_a2[l], rwkv_g2[l],
                              rwkv_k_k[l], rwkv_k_a[l], rwkv_r_k[l], rwkv_ln_w[l], rwkv_ln_b[l]),
            hgrn_lb=lower_bounds[l].reshape(1, -1), hgrn_nw=hgrn_norm_w[l].reshape(1, -1),
        ))

    def zeros_like_state(s):
        return jnp.zeros((depth, nb) + s.shape[2:], F32)

    p_states = tuple(zeros_like_state(s) for s in
                     (state_s5_re, state_s5_im, state_rwkv_shift, state_rwkv_wkv, state_hgrn))
    s_states = (state_s5_re, state_s5_im, state_rwkv_shift, state_rwkv_wkv, state_hgrn)
    ffw = dict(f1g=ffn1_w_gate.astype(BF16), f1u=ffn1_w_up.astype(BF16), f1d=ffn1_w_down.astype(BF16),
               f2g=ffn2_w_gate.astype(BF16), f2u=ffn2_w_up.astype(BF16), f2d=ffn2_w_down.astype(BF16))
    y_p, s5re_p, s5im_p, shift_p, wkv_p, hgrn_p = _run(
        x_prompt.reshape(nb * seq, d), nb, seq, p_states, lw, ffw, norm_final)
    y_s, s5re_s, s5im_s, shift_s, wkv_s, hgrn_s = _run(
        x_sample.reshape(ns * dseq, d), ns, dseq, s_states, lw, ffw, norm_final)
    return (y_p.reshape(nb, seq, d), y_s.reshape(ns, dseq, d), s5re_p, s5im_p, shift_p, wkv_p, hgrn_p,
            s5re_s, s5im_s, shift_s, wkv_s, hgrn_s)
```

```python
import functools
import math

import numpy as np
import jax
import jax.numpy as jnp
from jax import lax
from jax.experimental import pallas as pl
from jax.experimental.pallas import tpu as pltpu

F32 = jnp.float32
BF16 = jnp.bfloat16

NORM_EPS = 1e-6
RWKV_GN_EPS = 64e-5
HGRN_EPS = 1e-5

D_MODEL = 2048
S5_WIDTH = 512
S5_GROUP = 16
S5_GROUPS = 32
S5_STATE = 64
S5_LANES = S5_GROUPS * S5_STATE
S5_GB = 4
RWKV_WIDTH = 768
RWKV_HEAD = 64
RWKV_HEADS = 12
RWKV_PAIRS = 6
RWKV_SHIFT_WIDTH = 2560
RWKV_BLOCK = 16
RWKV_TM = 128
SEG_CHUNK = 256
HGRN_WIDTH = 768
HGRN_HEADS = 6
HGRN_DIM = 128
N_IN = 6144

SUBLANES = 8
LANES = 128
VMEM_LIMIT = 60 * 1024 * 1024


def _cparams(sem):
    return pltpu.CompilerParams(dimension_semantics=sem, vmem_limit_bytes=VMEM_LIMIT)


def _split2(x):
    hi = x.astype(BF16)
    lo = (x - hi.astype(F32)).astype(BF16)
    return hi, lo


def _split3(x):
    p1 = x.astype(BF16)
    r1 = x - p1.astype(F32)
    p2 = r1.astype(BF16)
    p3 = (r1 - p2.astype(F32)).astype(BF16)
    return p1, p2, p3


def _dot(a, b):
    return jnp.dot(a, b, preferred_element_type=F32)


def _dot_nt(a, b):
    return lax.dot_general(a, b, (((1,), (1,)), ((), ())), preferred_element_type=F32)


def _dot_tn(a, b):
    return lax.dot_general(a, b, (((0,), (0,)), ((), ())), preferred_element_type=F32)


def _sigmoid(x):
    return 1.0 / (1.0 + jnp.exp(-x))


def _ffn_kernel(*refs, cast):
    if cast:
        x_ref, nw_ref, wg_ref, wu_ref, wd_ref, o_ref, wgb_ref, wub_ref, wdb_ref, h_ref = refs
    else:
        x_ref, nw_ref, wg_ref, wu_ref, wd_ref, o_ref, h_ref = refs
    j = pl.program_id(1)

    @pl.when(j == 0)
    def _():
        x = x_ref[...]
        ms = jnp.mean(x * x, axis=-1, keepdims=True)
        h_ref[...] = (x * lax.rsqrt(ms + NORM_EPS) * nw_ref[...]).astype(BF16)
        o_ref[...] = jnp.zeros_like(o_ref)

    wg = wg_ref[...].astype(BF16)
    wu = wu_ref[...].astype(BF16)
    wd = wd_ref[...].astype(BF16)
    if cast:
        wgb_ref[...] = wg
        wub_ref[...] = wu
        wdb_ref[...] = wd
    h = h_ref[...]
    g = _dot(h, wg)
    u = _dot(h, wu)
    o_ref[...] += _dot((g * _sigmoid(g) * u).astype(BF16), wd)

    @pl.when(j == pl.num_programs(1) - 1)
    def _():
        o_ref[...] = x_ref[...] + 0.5 * o_ref[...]


def _ffn(x, nw, wg, wu, wd, l, cast=False, tm=1024):
    m, d = x.shape
    ff = wg.shape[2]
    tf = 256 if cast else 512
    out_specs = [pl.BlockSpec((tm, d), lambda i, j: (i, 0))]
    out_shape = [jax.ShapeDtypeStruct((m, d), F32)]
    if cast:
        assert m == tm
        out_specs += [pl.BlockSpec((d, tf), lambda i, j: (0, j)), pl.BlockSpec((d, tf), lambda i, j: (0, j)),
                      pl.BlockSpec((tf, d), lambda i, j: (j, 0))]
        out_shape += [jax.ShapeDtypeStruct((d, ff), BF16), jax.ShapeDtypeStruct((d, ff), BF16),
                      jax.ShapeDtypeStruct((ff, d), BF16)]
    outs = pl.pallas_call(
        functools.partial(_ffn_kernel, cast=cast),
        grid=(m // tm, ff // tf),
        in_specs=[
            pl.BlockSpec((tm, d), lambda i, j: (i, 0)),
            pl.BlockSpec((1, d), lambda i, j: (0, 0)),
            pl.BlockSpec((None, d, tf), lambda i, j: (l, 0, j)),
            pl.BlockSpec((None, d, tf), lambda i, j: (l, 0, j)),
            pl.BlockSpec((None, tf, d), lambda i, j: (l, j, 0)),
        ],
        out_specs=out_specs,
        out_shape=out_shape,
        scratch_shapes=[pltpu.VMEM((tm, d), BF16)],
        compiler_params=_cparams(("parallel", "arbitrary")),
        name="ffn_cast" if cast else "ffn",
    )(x, nw, wg, wu, wd)
    return outs if cast else outs[0]


def _inproj_kernel(x_ref, nw_ref, w_ref, o_ref, h_ref):
    @pl.when(pl.program_id(1) == 0)
    def _():
        x = x_ref[...]
        ms = jnp.mean(x * x, axis=-1, keepdims=True)
        h_ref[...] = (x * lax.rsqrt(ms + NORM_EPS) * nw_ref[...]).astype(BF16)

    o_ref[...] = _dot(h_ref[...], w_ref[...])


def _inproj(x, nw, w, tm=1024, tn=1024):
    m, d = x.shape
    n = w.shape[1]
    return pl.pallas_call(
        _inproj_kernel,
        grid=(m // tm, n // tn),
        in_specs=[
            pl.BlockSpec((tm, d), lambda i, j: (i, 0)),
            pl.BlockSpec((1, d), lambda i, j: (0, 0)),
            pl.BlockSpec((d, tn), lambda i, j: (0, j)),
        ],
        out_specs=pl.BlockSpec((tm, tn), lambda i, j: (i, j)),
        out_shape=jax.ShapeDtypeStruct((m, n), F32),
        scratch_shapes=[pltpu.VMEM((tm, d), BF16)],
        compiler_params=_cparams(("parallel", "arbitrary")),
        name="inproj",
    )(x, nw, w)


def _outproj_kernel(x_ref, a_ref, b_ref, c_ref, wa_ref, wb_ref, wc_ref, o_ref):
    acc = _dot(a_ref[...].astype(BF16), wa_ref[...])
    acc += _dot(b_ref[...].astype(BF16), wb_ref[...])
    acc += _dot(c_ref[...].astype(BF16), wc_ref[...])
    o_ref[...] = x_ref[...] + acc


def _outproj(x, ya, yb, yc, wa, wb, wc, tm=512):
    m, d = x.shape
    row = lambda i: (i, 0)
    whole = lambda i: (0, 0)
    return pl.pallas_call(
        _outproj_kernel,
        grid=(m // tm,),
        in_specs=[
            pl.BlockSpec((tm, d), row),
            pl.BlockSpec((tm, ya.shape[1]), row),
            pl.BlockSpec((tm, yb.shape[1]), row),
            pl.BlockSpec((tm, yc.shape[1]), row),
            pl.BlockSpec(wa.shape, whole),
            pl.BlockSpec(wb.shape, whole),
            pl.BlockSpec(wc.shape, whole),
        ],
        out_specs=pl.BlockSpec((tm, d), row),
        out_shape=jax.ShapeDtypeStruct((m, d), F32),
        compiler_params=_cparams(("parallel",)),
        name="outproj",
    )(x, ya, yb, yc, wa, wb, wc)


def _rmsnorm_kernel(x_ref, nw_ref, o_ref):
    x = x_ref[...]
    ms = jnp.mean(x * x, axis=-1, keepdims=True)
    o_ref[...] = x * lax.rsqrt(ms + NORM_EPS) * nw_ref[...]


def _rmsnorm(x, nw, tm=512):
    m, d = x.shape
    return pl.pallas_call(
        _rmsnorm_kernel,
        grid=(m // tm,),
        in_specs=[pl.BlockSpec((tm, d), lambda i: (i, 0)), pl.BlockSpec((1, d), lambda i: (0, 0))],
        out_specs=pl.BlockSpec((tm, d), lambda i: (i, 0)),
        out_shape=jax.ShapeDtypeStruct((m, d), F32),
        compiler_params=_cparams(("parallel",)),
        name="final_norm",
    )(x, nw)


def _s5_kernel(u_ref, x0r_ref, x0i_ref, b3r_ref, b3i_ref, cr_ref, ci_ref, cst_ref, d_ref, wglu_ref,
               bglu_ref, y_ref, xlr_ref, xli_ref, xr_ref, xi_ref, cr_s, ci_s, *, tr, chained):
    c = pl.program_id(1)
    u = u_ref[...]
    uh, ul = _split2(u)
    for gb in range(S5_GB):
        sl = slice(gb * LANES, (gb + 1) * LANES)
        osl = slice(gb * 512, (gb + 1) * 512)
        if chained:
            xr_ref[:, osl] = _dot(uh[:, sl], b3r_ref[gb, 0:LANES, :])
            xi_ref[:, osl] = _dot(uh[:, sl], b3i_ref[gb, 0:LANES, :])
        else:
            lhs = jnp.concatenate([uh[:, sl], ul[:, sl], uh[:, sl]], axis=1)
            xr_ref[:, osl] = _dot(lhs, b3r_ref[gb])
            xi_ref[:, osl] = _dot(lhs, b3i_ref[gb])

    if chained:
        @pl.when(c == 0)
        def _():
            cr_s[...] = x0r_ref[0]
            ci_s[...] = x0i_ref[0]

    a1r, a1i, a2r, a2i, a4r, a4i, pr, pi = [cst_ref[i] for i in range(8)]

    def tile(j, carry):
        rows = pl.ds(pl.multiple_of(j * SUBLANES, SUBLANES), SUBLANES)
        xr = xr_ref[rows, :]
        xi = xi_ref[rows, :]
        for k, ar, ai in ((1, a1r, a1i), (2, a2r, a2i), (4, a4r, a4i)):
            sr = pltpu.roll(xr, k, 0)
            si = pltpu.roll(xi, k, 0)
            xr, xi = xr + ar * sr - ai * si, xi + ar * si + ai * sr
        if chained:
            x0r = cr_s[...]
            x0i = ci_s[...]
        else:
            x0r = x0r_ref[pl.ds(j, 1), :]
            x0i = x0i_ref[pl.ds(j, 1), :]
        x0r = jnp.broadcast_to(x0r, xr.shape)
        x0i = jnp.broadcast_to(x0i, xr.shape)
        xr, xi = xr + pr * x0r - pi * x0i, xi + pr * x0i + pi * x0r
        xr_ref[rows, :] = xr
        xi_ref[rows, :] = xi
        if chained:
            cr_s[...] = xr[SUBLANES - 1:SUBLANES, :]
            ci_s[...] = xi[SUBLANES - 1:SUBLANES, :]
        else:
            xlr_ref[pl.ds(j, 1), :] = xr[SUBLANES - 1:SUBLANES, :]
            xli_ref[pl.ds(j, 1), :] = xi[SUBLANES - 1:SUBLANES, :]
        return carry

    lax.fori_loop(0, tr // SUBLANES, tile, 0)

    if chained:
        @pl.when(c == pl.num_programs(1) - 1)
        def _():
            xlr_ref[0] = cr_s[...]
            xli_ref[0] = ci_s[...]

    ys = []
    for gb in range(S5_GB):
        osl = slice(gb * 512, (gb + 1) * 512)
        ys.append(_dot(xr_ref[:, osl].astype(BF16), cr_ref[gb]) + _dot(xi_ref[:, osl].astype(BF16), ci_ref[gb]))
    y = jnp.concatenate(ys, axis=1) + d_ref[...] * u
    y = 0.5 * y * (1.0 + jnp.tanh(math.sqrt(2.0 / math.pi) * (y + 0.044715 * (y * y * y))))
    z = _dot(y.astype(BF16), wglu_ref[...]) + bglu_ref[...]
    y_ref[...] = y * _sigmoid(z)


def _s5(p3, x0r, x0i, l, cs, nseq, t, tr=512):
    rows = nseq * t
    depth = x0r.shape[0]
    chained = t > SUBLANES
    if chained:
        grid = (nseq, t // tr)
        row_map = lambda n, c: (n * (t // tr) + c, 5)
        out_map = lambda n, c: (n * (t // tr) + c, 0)
        x0r = x0r.reshape(depth, nseq, 1, S5_LANES)
        x0i = x0i.reshape(depth, nseq, 1, S5_LANES)
        in_st = pl.BlockSpec((None, 1, 1, S5_LANES), lambda n, c: (l, n, 0, 0))
        st_spec = pl.BlockSpec((1, 1, S5_LANES), lambda n, c: (n, 0, 0))
        st_shape = jax.ShapeDtypeStruct((nseq, 1, S5_LANES), F32)
    else:
        assert t == SUBLANES
        grid = (rows // tr, 1)
        row_map = lambda n, c: (n, 5)
        out_map = lambda n, c: (n, 0)
        in_st = pl.BlockSpec((None, tr // SUBLANES, S5_LANES), lambda n, c: (l, n, 0))
        st_spec = pl.BlockSpec((tr // SUBLANES, S5_LANES), lambda n, c: (n, 0))
        st_shape = jax.ShapeDtypeStruct((nseq, S5_LANES), F32)
    whole3 = lambda n, c: (0, 0, 0)
    whole2 = lambda n, c: (0, 0)
    y, xlr, xli = pl.pallas_call(
        functools.partial(_s5_kernel, tr=tr, chained=chained),
        grid=grid,
        in_specs=[
            pl.BlockSpec((tr, S5_WIDTH), row_map),
            in_st, in_st,
            pl.BlockSpec(cs["b3r"].shape, whole3),
            pl.BlockSpec(cs["b3i"].shape, whole3),
            pl.BlockSpec(cs["cr"].shape, whole3),
            pl.BlockSpec(cs["ci"].shape, whole3),
            pl.BlockSpec(cs["scan"].shape, whole3),
            pl.BlockSpec((1, S5_WIDTH), whole2),
            pl.BlockSpec((S5_WIDTH, S5_WIDTH), whole2),
            pl.BlockSpec((1, S5_WIDTH), whole2),
        ],
        out_specs=[pl.BlockSpec((tr, S5_WIDTH), out_map), st_spec, st_spec],
        out_shape=[jax.ShapeDtypeStruct((rows, S5_WIDTH), F32), st_shape, st_shape],
        scratch_shapes=[pltpu.VMEM((tr, S5_LANES), F32), pltpu.VMEM((tr, S5_LANES), F32),
                        pltpu.VMEM((1, S5_LANES), F32), pltpu.VMEM((1, S5_LANES), F32)],
        compiler_params=_cparams(("parallel", "arbitrary")),
        name="s5_chain" if chained else "s5_tile",
    )(p3, x0r, x0i, cs["b3r"], cs["b3i"], cs["cr"], cs["ci"], cs["scan"], cs["d"], cs["wglu"], cs["bglu"])
    return y, xlr.reshape(nseq, S5_GROUPS, S5_STATE), xli.reshape(nseq, S5_GROUPS, S5_STATE)


def _s5_consts(a_re, a_im, log_dt, b_re, b_im, c_re, c_im, d, w_glu, b_glu):
    g, p, h = S5_GROUPS, S5_STATE, S5_GROUP
    dt = jnp.exp(log_dt)[:, None]
    mag = jnp.exp(a_re * dt)
    abr = mag * jnp.cos(a_im * dt)
    abi = mag * jnp.sin(a_im * dt)
    den = a_re * a_re + a_im * a_im
    qr = ((abr - 1.0) * a_re + abi * a_im) / den
    qi = (abi * a_re - (abr - 1.0) * a_im) / den
    bbr = qr[..., None] * b_re - qi[..., None] * b_im
    bbi = qr[..., None] * b_im + qi[..., None] * b_re
    pr, pi = [abr], [abi]
    for _ in range(SUBLANES - 1):
        pr, pi = pr + [pr[-1] * abr - pi[-1] * abi], pi + [pr[-1] * abi + pi[-1] * abr]
    pr = jnp.stack(pr).reshape(SUBLANES, g * p)
    pi = jnp.stack(pi).reshape(SUBLANES, g * p)
    row = jnp.arange(SUBLANES)[:, None]

    def lvl(k):
        return [jnp.where(row >= k, pr[k - 1][None, :], 0.0), jnp.where(row >= k, pi[k - 1][None, :], 0.0)]

    scan = jnp.stack(lvl(1) + lvl(2) + lvl(4) + [pr, pi]).astype(F32)
    eye = jnp.eye(SUBLANES, dtype=F32)

    def in_blocks(x):
        xx = x.reshape(S5_GB, SUBLANES, p, h).transpose(0, 1, 3, 2)
        blk = (xx[:, :, :, None, :] * eye[None, :, None, :, None]).reshape(S5_GB, SUBLANES * h, SUBLANES * p)
        hi = blk.astype(BF16)
        lo = (blk - hi.astype(F32)).astype(BF16)
        return jnp.concatenate([hi, hi, lo], axis=1)

    def out_blocks(x):
        xx = x.reshape(S5_GB, SUBLANES, h, p).transpose(0, 1, 3, 2)
        return (xx[:, :, :, None, :] * eye[None, :, None, :, None]).reshape(
            S5_GB, SUBLANES * p, SUBLANES * h).astype(BF16)

    return dict(b3r=in_blocks(bbr), b3i=in_blocks(bbi),
                cr=out_blocks(c_re), ci=out_blocks(-c_im), scan=scan,
                d=d.reshape(1, -1), wglu=w_glu.astype(BF16), bglu=b_glu.reshape(1, -1))


def _seg_sum(x, e2):
    hi, lo = _split2(x)
    cw = e2.shape[1]
    return jnp.concatenate(
        [_dot(jnp.concatenate([hi[:, c:c + cw], lo[:, c:c + cw]], axis=1), e2) for c in range(0, x.shape[1], cw)],
        axis=1)


def _rwkv_prep_kernel(*refs, nlev, chained, tiles_per_seq):
    if chained:
        z_ref, prev_ref, sh_ref = refs[:3]
        refs = refs[3:]
    else:
        z_ref, zp_ref = refs[:2]
        refs = refs[2:]
    (mu_ref, w0_ref, a0_ref, w2_ref, a2_ref, g2_ref, kk_ref, ka_ref, rk_ref, e2_ref, tril_ref, mk_ref, hm_ref,
     ap_o, rp_o, bt_o, kp_o, cc_o, y0_o, v_o, gm_o, g_o, bonus_o) = refs
    tm = RWKV_TM
    z = z_ref[...]
    if chained:
        first = (pl.program_id(0) % tiles_per_seq) == 0
        prev_row = jnp.where(first, sh_ref[0], prev_ref[SUBLANES - 1:SUBLANES, :])
        rid = lax.broadcasted_iota(jnp.int32, z.shape, 0)
        zp = jnp.where(rid == 0, prev_row, pltpu.roll(z, 1, 0))
    else:
        zp = zp_ref[...]
    zm = z + (zp - z) * mu_ref[...]
    w = RWKV_WIDTH
    r = zm[:, 0:w]
    k = zm[:, w:2 * w]
    v = zm[:, 2 * w:3 * w]
    wa = zm[:, 3 * w:3 * w + LANES]
    gi = zm[:, 3 * w + LANES:3 * w + 2 * LANES]
    e2 = e2_ref[...]
    wl = w0_ref[...] + _dot(jnp.tanh(wa).astype(BF16), w2_ref[...])
    sp = jnp.maximum(-wl, 0.0) + jnp.log(1.0 + jnp.exp(-jnp.abs(wl)))
    ld = -jnp.exp(-sp - 0.5)
    a = _sigmoid(a0_ref[...] + _dot(wa.astype(BF16), a2_ref[...]))
    g_o[...] = _dot(_sigmoid(gi).astype(BF16), g2_ref[...])
    kk = k * kk_ref[...]
    kk = kk * lax.rsqrt(jnp.maximum(_seg_sum(kk * kk, e2), 1e-24))
    k2 = k * (1.0 + (a - 1.0) * ka_ref[...])
    b = kk * a
    bonus_o[...] = _seg_sum(r * k2 * rk_ref[...], e2) * v
    v_o[...] = v

    p1, p2, p3 = _split3(ld)
    cum = _dot(tril_ref[...], jnp.concatenate([p1, p2, p3], axis=0))
    gam = jnp.exp(cum)
    ginv = jnp.exp(-cum)
    gm_o[...] = gam
    alpha = kk * jnp.exp(cum - ld)
    beta = b * ginv
    kap = k2 * ginv
    rho = r * gam
    bt_o[...] = beta
    kp_o[...] = kap

    strict = mk_ref[0]
    incl = mk_ref[1]
    eye = mk_ref[2]
    heads = [(j, e) for j in range(RWKV_PAIRS) for e in range(2)]
    nh = len(heads)

    def lanes(j):
        return slice(j * LANES, (j + 1) * LANES)

    al = [alpha[:, lanes(j)] * hm_ref[e] for j, e in heads]
    rh = [rho[:, lanes(j)] * hm_ref[e] for j, e in heads]
    vh = [(v[:, lanes(j)] * hm_ref[e]).astype(BF16) for j, e in heads]
    yk = [jnp.concatenate([beta[:, lanes(j)], kap[:, lanes(j)]], axis=0).astype(BF16) for j in range(RWKV_PAIRS)]
    wu = [_dot_nt(jnp.concatenate(_split2(al[i]), axis=1), jnp.concatenate([yk[j], yk[j]], axis=1))
          for i, (j, e) in enumerate(heads)]
    wl = [_dot_nt(rh[i].astype(BF16), yk[j]) for i, (j, e) in enumerate(heads)]
    mm = [wu[i][:, :tm] * strict for i in range(nh)]
    nn = [wu[i][:, tm:] * strict for i in range(nh)]
    pp = [(wl[i][:, :tm] * incl).astype(BF16) for i in range(nh)]
    qq = [(wl[i][:, tm:] * incl).astype(BF16) for i in range(nh)]
    tinv = [eye - mm[i] * mk_ref[3] for i in range(nh)]
    for lv in range(1, nlev):
        th = [tinv[i].astype(BF16) for i in range(nh)]
        tmp = [_dot(jnp.concatenate([th[i], th[i]], axis=1),
                    jnp.concatenate(_split2(mm[i] * mk_ref[3 + lv]), axis=0)) for i in range(nh)]
        tinv = [tinv[i] - _dot(jnp.concatenate(_split2(tmp[i]), axis=1),
                               jnp.concatenate([th[i], th[i]], axis=0)) for i in range(nh)]
    th = [tinv[i].astype(BF16) for i in range(nh)]
    nv = [_dot(jnp.concatenate(_split2(nn[i]), axis=1), jnp.concatenate([vh[i], vh[i]], axis=0)) for i in range(nh)]
    qv = [_dot(qq[i], vh[i]) for i in range(nh)]
    tac = [_dot(jnp.concatenate([th[i], th[i]], axis=1),
                jnp.concatenate(_split2(jnp.concatenate([al[i], nv[i]], axis=1)), axis=0)) for i in range(nh)]
    pac = [_dot(pp[i], tac[i].astype(BF16)) for i in range(nh)]
    ap = [tac[i][:, :LANES] for i in range(nh)]
    cc = [tac[i][:, LANES:] for i in range(nh)]
    rp = [rh[i] - pac[i][:, :LANES] for i in range(nh)]
    y0 = [qv[i] - pac[i][:, LANES:] for i in range(nh)]
    for j in range(RWKV_PAIRS):
        ap_o[:, lanes(j)] = ap[2 * j] + ap[2 * j + 1]
        rp_o[:, lanes(j)] = rp[2 * j] + rp[2 * j + 1]
        cc_o[:, lanes(j)] = cc[2 * j] + cc[2 * j + 1]
        y0_o[:, lanes(j)] = y0[2 * j] + y0[2 * j + 1]


def _rwkv_tables(tm, blk):
    t = np.arange(tm)
    same = (t[:, None] // blk) == (t[None, :] // blk)
    tril = (same & (t[None, :] <= t[:, None])).astype(np.float32)
    masks = [same & (t[None, :] < t[:, None]), same & (t[None, :] <= t[:, None]), np.eye(tm, dtype=bool)]
    n = 2
    while n <= blk:
        m = n // 2
        masks.append(((t[:, None] // n) == (t[None, :] // n)) & ((t[:, None] % n) >= m) & ((t[None, :] % n) < m))
        n *= 2
    nlev = len(masks) - 3
    hm = np.stack([(np.arange(LANES) < RWKV_HEAD), (np.arange(LANES) >= RWKV_HEAD)]).astype(np.float32)
    return (jnp.asarray(np.concatenate([tril, tril, tril], axis=1), BF16),
            jnp.asarray(np.stack(masks).astype(np.float32)), jnp.asarray(hm.reshape(2, 1, LANES)), nlev)


def _rwkv_prep(p3, prev, cs, nseq, t):
    m = nseq * t
    tm = RWKV_TM
    blk = min(RWKV_BLOCK, t)
    chained = t >= tm
    tril3, masks, hm, nlev = _rwkv_tables(tm, blk)
    w = RWKV_WIDTH
    row = lambda i: (i, 0)
    whole = lambda i: (0, 0)
    whole3 = lambda i: (0, 0, 0)
    vec = pl.BlockSpec((1, w), whole)
    if chained:
        tps = t // tm
        lead = [pl.BlockSpec((tm, RWKV_SHIFT_WIDTH), row),
                pl.BlockSpec((SUBLANES, RWKV_SHIFT_WIDTH),
                             lambda i: (jnp.maximum(i * (tm // SUBLANES) - 1, 0), 0)),
                pl.BlockSpec((1, 1, RWKV_SHIFT_WIDTH), lambda i: (i // tps, 0, 0))]
        args = (p3, p3, prev)
    else:
        tps = 1
        lead = [pl.BlockSpec((tm, RWKV_SHIFT_WIDTH), row), pl.BlockSpec((tm, RWKV_SHIFT_WIDTH), row)]
        args = (p3, prev)
    return pl.pallas_call(
        functools.partial(_rwkv_prep_kernel, nlev=nlev, chained=chained, tiles_per_seq=tps),
        grid=(m // tm,),
        in_specs=lead + [
            pl.BlockSpec((1, RWKV_SHIFT_WIDTH), whole),
            vec, vec,
            pl.BlockSpec((LANES, w), whole), pl.BlockSpec((LANES, w), whole), pl.BlockSpec((LANES, w), whole),
            vec, vec, vec,
            pl.BlockSpec((2 * SEG_CHUNK, SEG_CHUNK), whole),
            pl.BlockSpec(tril3.shape, whole), pl.BlockSpec(masks.shape, whole3), pl.BlockSpec(hm.shape, whole3),
        ],
        out_specs=[pl.BlockSpec((tm, w), row)] * 10,
        out_shape=[jax.ShapeDtypeStruct((m, w), F32)] * 10,
        compiler_params=_cparams(("parallel",)),
        name="rwkv_prep",
    )(*args, cs["mu"], cs["w0"], cs["a0"], cs["w2"], cs["a2"], cs["g2"], cs["k_k"], cs["k_a"], cs["r_k"],
      cs["e2w"], tril3, masks, hm)


def _rwkv_rec_kernel(*refs, nbp, tc, blk, aliased):
    ap_ref, rp_ref, bt_ref, kp_ref, cc_ref, y0_ref, v_ref, gm_ref, s0_ref, bd_ref = refs[:10]
    y_ref, so_ref, s_ref = refs[10 + (1 if aliased else 0):]
    c = pl.program_id(1)
    hd = RWKV_HEAD
    pairs = [(n, j) for n in range(nbp) for j in range(RWKV_PAIRS)]

    @pl.when(c == 0)
    def _():
        zero = jnp.zeros((hd, hd), F32)
        for idx, (n, j) in enumerate(pairs):
            top = jnp.concatenate([s0_ref[n, 2 * j], zero], axis=1)
            bot = jnp.concatenate([zero, s0_ref[n, 2 * j + 1]], axis=1)
            s_ref[idx] = jnp.concatenate([top, bot], axis=0)

    bd = bd_ref[...]

    def block(bi, carry):
        rows = pl.ds(pl.multiple_of(bi * blk, blk), blk)
        gmats = []
        for idx, (n, j) in enumerate(pairs):
            ls = slice(j * LANES, (j + 1) * LANES)
            sh, sl = _split2(s_ref[idx])
            xh = jnp.concatenate([ap_ref[n, rows, ls], rp_ref[n, rows, ls]], axis=0).astype(BF16)
            gmats.append(_dot_nt(jnp.concatenate([xh, xh], axis=1), jnp.concatenate([sh, sl], axis=1)))
        upds = []
        for idx, (n, j) in enumerate(pairs):
            ls = slice(j * LANES, (j + 1) * LANES)
            gmat = gmats[idx]
            e = -gmat[:blk] - cc_ref[n, rows, ls]
            y_ref[n, rows, ls] = gmat[blk:] + y0_ref[n, rows, ls]
            evh, evl = _split2(jnp.concatenate([e, v_ref[n, rows, ls]], axis=0))
            bkh, bkl = _split2(jnp.concatenate([bt_ref[n, rows, ls], kp_ref[n, rows, ls]], axis=0))
            upds.append(_dot_tn(jnp.concatenate([evh, evh, evl], axis=0), jnp.concatenate([bkh, bkl, bkh], axis=0)))
        for idx, (n, j) in enumerate(pairs):
            ls = slice(j * LANES, (j + 1) * LANES)
            gl = gm_ref[n, rows, ls][blk - 1:blk, :]
            s_ref[idx] = (s_ref[idx] + upds[idx] * bd) * gl
        return carry

    lax.fori_loop(0, tc // blk, block, 0)

    @pl.when(c == pl.num_programs(1) - 1)
    def _():
        for idx, (n, j) in enumerate(pairs):
            s = s_ref[idx]
            so_ref[n, 2 * j] = s[:hd, :hd]
            so_ref[n, 2 * j + 1] = s[hd:, hd:]


def _rwkv_rec(arrs, s0_all, l, so_prev, cs, nseq, t, nbp=4, tc=128):
    w = RWKV_WIDTH
    tc = min(tc, t)
    blk = min(RWKV_BLOCK, t)
    depth = s0_all.shape[0]
    seq_map = lambda i, c: (i, c, 0)
    st_map = lambda i, c: (l, i, 0, 0, 0)
    blkspec = pl.BlockSpec((nbp, tc, w), seq_map)
    st = pl.BlockSpec((None, nbp, RWKV_HEADS, RWKV_HEAD, RWKV_HEAD), st_map)
    aliased = so_prev is not None
    in_specs = [blkspec] * 8 + [st, pl.BlockSpec((LANES, LANES), lambda i, c: (0, 0))]
    args = list(arrs) + [s0_all, cs["bd"]]
    aliases = {}
    if aliased:
        in_specs.append(pl.BlockSpec(memory_space=pl.ANY))
        args.append(so_prev)
        aliases = {10: 1}
    return pl.pallas_call(
        functools.partial(_rwkv_rec_kernel, nbp=nbp, tc=tc, blk=blk, aliased=aliased),
        grid=(nseq // nbp, t // tc),
        in_specs=in_specs,
        out_specs=[blkspec, st],
        out_shape=[jax.ShapeDtypeStruct((nseq, t, w), F32),
                   jax.ShapeDtypeStruct((depth, nseq, RWKV_HEADS, RWKV_HEAD, RWKV_HEAD), F32)],
        scratch_shapes=[pltpu.VMEM((nbp * RWKV_PAIRS, LANES, LANES), F32)],
        input_output_aliases=aliases,
        compiler_params=_cparams(("parallel", "arbitrary")),
        name="rwkv_rec",
    )(*args)


def _rwkv_post_kernel(y_ref, bonus_ref, g_ref, lnw_ref, lnb_ref, e2_ref, o_ref):
    y = y_ref[...]
    e2 = e2_ref[...]
    mean = _seg_sum(y, e2) * (1.0 / RWKV_HEAD)
    yc = y - mean
    var = _seg_sum(yc * yc, e2) * (1.0 / RWKV_HEAD)
    yn = yc * lax.rsqrt(var + RWKV_GN_EPS) * lnw_ref[...] + lnb_ref[...]
    o_ref[...] = (yn + bonus_ref[...]) * g_ref[...]


def _rwkv_post(y, bonus, g, cs, tm=512):
    m, w = y.shape
    tm = min(tm, m)
    row = lambda i: (i, 0)
    whole = lambda i: (0, 0)
    return pl.pallas_call(
        _rwkv_post_kernel,
        grid=(m // tm,),
        in_specs=[pl.BlockSpec((tm, w), row)] * 3 + [pl.BlockSpec((1, w), whole)] * 2
        + [pl.BlockSpec((2 * SEG_CHUNK, SEG_CHUNK), whole)],
        out_specs=pl.BlockSpec((tm, w), row),
        out_shape=jax.ShapeDtypeStruct((m, w), F32),
        compiler_params=_cparams(("parallel",)),
        name="rwkv_post",
    )(y, bonus, g, cs["ln_w"], cs["ln_b"], cs["e2w"])


def _rwkv_consts(mu, w0, w2, a0, a2, g2, k_k, k_a, r_k, ln_w, ln_b):
    w = RWKV_WIDTH
    seg = np.arange(SEG_CHUNK) // RWKV_HEAD
    e_w = (seg[:, None] == seg[None, :]).astype(np.float32)
    seg = np.arange(LANES) // RWKV_HEAD
    bd = (seg[:, None] == seg[None, :]).astype(np.float32)
    z64 = jnp.zeros((RWKV_HEAD, w), F32)
    return dict(
        mu=mu.reshape(1, -1), w0=w0.reshape(1, -1), a0=a0.reshape(1, -1),
        w2=jnp.concatenate([w2, z64], axis=0).astype(BF16),
        a2=jnp.concatenate([z64, a2], axis=0).astype(BF16),
        g2=g2.astype(BF16), k_k=k_k.reshape(1, -1), k_a=k_a.reshape(1, -1), r_k=r_k.reshape(1, -1),
        ln_w=ln_w.reshape(1, -1), ln_b=ln_b.reshape(1, -1),
        e2w=jnp.asarray(np.concatenate([e_w, e_w], axis=0), BF16),
        bd=jnp.asarray(bd, F32))


def _rwkv(p3, shift_l, wkv_all, l, so_prev, cs, nseq, t):
    if t >= RWKV_TM:
        prev = shift_l.reshape(nseq, 1, RWKV_SHIFT_WIDTH)
    else:
        z3 = p3.reshape(nseq, t, N_IN)[:, :, :RWKV_SHIFT_WIDTH]
        prev = jnp.concatenate([shift_l[:, None, :], z3[:, :-1]], axis=1).reshape(nseq * t, RWKV_SHIFT_WIDTH)
    outs = _rwkv_prep(p3, prev, cs, nseq, t)
    arrs = [a.reshape(nseq, t, RWKV_WIDTH) for a in outs[:8]]
    y, so = _rwkv_rec(arrs, wkv_all, l, so_prev, cs, nseq, t)
    return _rwkv_post(y.reshape(nseq * t, RWKV_WIDTH), outs[9], outs[8], cs), so


def _hgrn_tables(c):
    t = np.arange(c)
    j = np.arange(c)
    mats = [(j[None, :] <= t[:, None]), (j[None, :] > t[:, None])]
    masks = []
    n = 2
    while n <= c:
        m = n // 2
        bs = (t // n) * n
        hi_half = (t % n) >= m
        mats.append((hi_half[:, None] & (j[None, :] >= (bs + m)[:, None]) & (j[None, :] <= t[:, None]))
                    | ((~hi_half)[:, None] & (j[None, :] > t[:, None]) & (j[None, :] <= (bs + m - 1)[:, None])))
        masks.append(((t[:, None] // n) == (t[None, :] // n)) & hi_half[:, None] & (~hi_half)[None, :])
        n *= 2
    w = np.concatenate(mats, axis=0).astype(np.float32)
    w3 = np.concatenate([w, w, w], axis=1)
    return jnp.asarray(w3, BF16), jnp.asarray(np.stack(masks).astype(np.float32)), len(masks)


def _hgrn_kernel(*refs, rb, c, nlev, chained, aliased, sub=2):
    q_ref, f_ref, i_ref, g_ref, s0_ref, lb_ref, nw_ref, w3_ref, mk_ref = refs[:9]
    o_ref, so_ref, st_ref = refs[9 + (1 if aliased else 0):]
    cidx = pl.program_id(1)
    hdim = HGRN_DIM

    if chained:
        @pl.when(cidx == 0)
        def _():
            for h in range(HGRN_HEADS):
                st_ref[h] = s0_ref[0, h].T

    w3 = w3_ref[...]
    row_i = lax.broadcasted_iota(jnp.int32, (c, c), 0)
    col_i = lax.broadcasted_iota(jnp.int32, (c, c), 1)
    eye = (row_i == col_i).astype(F32)

    heads = range(HGRN_HEADS)

    def lanes(h):
        return slice(h * hdim, (h + 1) * hdim)

    def block(it, carry):
        rows_s = [pl.ds(pl.multiple_of((it * sub + s) * c, c), c) for s in range(sub)]
        items = [(s, h) for s in range(sub) for h in heads]
        fg = {k: lb_ref[:, lanes(k[1])] + (1.0 - lb_ref[:, lanes(k[1])]) * _sigmoid(f_ref[rows_s[k[0]], lanes(k[1])])
              for k in items}
        kk = {k: 1.0 - fg[k] for k in items}
        qs = {}
        for k in items:
            q = q_ref[rows_s[k[0]], lanes(k[1])]
            qs[k] = q * _sigmoid(q)
        ex = {k: jnp.exp(_dot(w3, jnp.concatenate(_split3(jnp.log(fg[k])), axis=0))) for k in items}
        vb = {k: i_ref[rows_s[k[0]], lanes(k[1])].astype(BF16) for k in items}
        lev = {(k, lv): _dot_nt((qs[k] * ex[k][(2 + lv) * c:(3 + lv) * c]).astype(BF16),
                                (kk[k] * ex[k][(2 + lv) * c:(3 + lv) * c]).astype(BF16))
               for k in items for lv in range(nlev)}
        att = {}
        for k in items:
            a = eye * jnp.sum(qs[k] * kk[k], axis=-1, keepdims=True)
            for lv in range(nlev):
                a = a + jnp.where(mk_ref[lv] > 0.0, lev[(k, lv)], 0.0)
            att[k] = a.astype(BF16)
        intra = {k: _dot(att[k], vb[k]) for k in items}
        kv = {k: _dot_tn(vb[k], (kk[k] * ex[k][c:2 * c]).astype(BF16)) for k in items}
        qe = {k: (qs[k] * ex[k][0:c]).astype(BF16) for k in items}
        st = {h: st_ref[h] for h in heads} if chained else None
        for s in range(sub):
            if not chained:
                st = {h: s0_ref[it * sub + s, h].T for h in heads}
            inter = {h: _dot_nt(qe[(s, h)], st[h].astype(BF16)) for h in heads}
            st = {h: st[h] * ex[(s, h)][c - 1:c, :] + kv[(s, h)] for h in heads}
            if not chained:
                for h in heads:
                    so_ref[it * sub + s, h] = st[h].T
            for h in heads:
                o = inter[h] + intra[(s, h)]
                o = o * lax.rsqrt(jnp.mean(o * o, axis=-1, keepdims=True) + HGRN_EPS)
                gg = g_ref[rows_s[s], lanes(h)]
                o_ref[rows_s[s], lanes(h)] = o * nw_ref[:, lanes(h)] * (gg * _sigmoid(gg))
        if chained:
            for h in heads:
                st_ref[h] = st[h]
        return carry

    lax.fori_loop(0, rb // (c * sub), block, 0)

    if chained:
        @pl.when(cidx == pl.num_programs(1) - 1)
        def _():
            for h in range(HGRN_HEADS):
                so_ref[0, h] = st_ref[h].T


def _hgrn(p3, s0_all, l, so_prev, lb, nw, nseq, t):
    rows = nseq * t
    depth = s0_all.shape[0]
    c = math.gcd(t, 64)
    chained = t > c
    w3, masks, nlev = _hgrn_tables(c)
    w = HGRN_WIDTH
    if chained:
        rb = 256
        nblk = t // rb
        grid = (nseq, nblk)
        col = lambda cb: (lambda n, k: (n * nblk + k, cb))
        out_map = lambda n, k: (n * nblk + k, 0)
        nsb = 1
    else:
        nsb = 16
        rb = nsb * c
        grid = (rows // rb, 1)
        col = lambda cb: (lambda n, k: (n, cb))
        out_map = lambda n, k: (n, 0)
    st_spec = pl.BlockSpec((None, nsb, HGRN_HEADS, HGRN_DIM, HGRN_DIM), lambda n, k: (l, n, 0, 0, 0))
    whole2 = lambda n, k: (0, 0)
    whole3 = lambda n, k: (0, 0, 0)
    aliased = so_prev is not None
    in_specs = [pl.BlockSpec((rb, w), col(4)), pl.BlockSpec((rb, w), col(5)), pl.BlockSpec((rb, w), col(6)),
                pl.BlockSpec((rb, w), col(7)), st_spec,
                pl.BlockSpec((1, w), whole2), pl.BlockSpec((1, w), whole2),
                pl.BlockSpec(w3.shape, whole2), pl.BlockSpec(masks.shape, whole3)]
    args = [p3, p3, p3, p3, s0_all, lb, nw, w3, masks]
    aliases = {}
    if aliased:
        in_specs.append(pl.BlockSpec(memory_space=pl.ANY))
        args.append(so_prev)
        aliases = {9: 1}
    return pl.pallas_call(
        functools.partial(_hgrn_kernel, rb=rb, c=c, nlev=nlev, chained=chained, aliased=aliased),
        grid=grid,
        in_specs=in_specs,
        out_specs=[pl.BlockSpec((rb, w), out_map), st_spec],
        out_shape=[jax.ShapeDtypeStruct((rows, w), F32),
                   jax.ShapeDtypeStruct((depth, nseq, HGRN_HEADS, HGRN_DIM, HGRN_DIM), F32)],
        scratch_shapes=[pltpu.VMEM((HGRN_HEADS, HGRN_DIM, HGRN_DIM), F32)],
        input_output_aliases=aliases,
        compiler_params=_cparams(("parallel", "arbitrary")),
        name="hgrn_chain" if chained else "hgrn_tile",
    )(*args)


def _run(x, nseq, t, states, lw, ffw, norm_final):
    s5_re0, s5_im0, shift0, wkv0, hgrn0 = states
    depth = len(lw)
    s5_re0 = s5_re0.reshape(depth, nseq, S5_LANES)
    s5_im0 = s5_im0.reshape(depth, nseq, S5_LANES)
    s5_re_l, s5_im_l, shift_l = [], [], []
    wkv_out = None
    hgrn_out = None
    cast = not isinstance(ffw["f1g"], list)
    ffb = {k: [] for k in ffw}

    def ffn(x, nw, names, l):
        if not cast:
            return _ffn(x, nw, *(ffw[k][l][None] for k in names), 0)
        res = _ffn(x, nw, *(ffw[k] for k in names), l, cast=True)
        for k, wb in zip(names, res[1:]):
            ffb[k].append(wb)
        return res[0]

    for l in range(depth):
        w = lw[l]
        x = ffn(x, w["norm_ffn1"], ("f1g", "f1u", "f1d"), l)
        p3 = _inproj(x, w["norm_mix"], w["w_in"])
        y_s5, s_re, s_im = _s5(p3, s5_re0, s5_im0, l, w["s5"], nseq, t)
        y_rw, wkv_out = _rwkv(p3, shift0[l], wkv0, l, wkv_out, w["rwkv"], nseq, t)
        y_hg, hgrn_out = _hgrn(p3, hgrn0, l, hgrn_out, w["hgrn_lb"], w["hgrn_nw"], nseq, t)
        x = _outproj(x, y_s5, y_rw, y_hg, w["wo_s5"], w["wo_rw"], w["wo_hg"])
        x = ffn(x, w["norm_ffn2"], ("f2g", "f2u", "f2d"), l)
        s5_re_l.append(s_re)
        s5_im_l.append(s_im)
        shift_l.append(p3.reshape(nseq, t, N_IN)[:, t - 1, :RWKV_SHIFT_WIDTH])
    y = _rmsnorm(x, norm_final.reshape(1, -1))
    return (y, jnp.stack(s5_re_l), jnp.stack(s5_im_l), jnp.stack(shift_l), wkv_out, hgrn_out), ffb


def kernel(x_prompt, x_sample, state_s5_re, state_s5_im, state_rwkv_shift, state_rwkv_wkv, state_hgrn, norm_ffn1, ffn1_w_gate, ffn1_w_up, ffn1_w_down, norm_mix, w_in, s5_a_re, s5_a_im, s5_log_dt, s5_b_re, s5_b_im, s5_c_re, s5_c_im, s5_d, s5_w_glu, s5_b_glu, rwkv_mu, rwkv_w0, rwkv_w2, rwkv_a0, rwkv_a2, rwkv_g2, rwkv_k_k, rwkv_k_a, rwkv_r_k, rwkv_ln_w, rwkv_ln_b, hgrn_lb_raw, hgrn_norm_w, w_out, norm_ffn2, ffn2_w_gate, ffn2_w_up, ffn2_w_down, norm_final):
    depth = w_in.shape[0]
    nb, seq, d = x_prompt.shape
    ns, dseq, _ = x_sample.shape

    p_lb = jax.nn.softmax(hgrn_lb_raw.astype(F32), axis=0)
    lower_bounds = jnp.cumsum(p_lb, axis=0) - p_lb[0]

    o_rw = S5_WIDTH
    o_hg = S5_WIDTH + RWKV_SHIFT_WIDTH
    lw = []
    for l in range(depth):
        wi = w_in[l]
        wo = w_out[l].astype(BF16)
        lw.append(dict(
            norm_ffn1=norm_ffn1[l].reshape(1, -1), norm_mix=norm_mix[l].reshape(1, -1),
            norm_ffn2=norm_ffn2[l].reshape(1, -1),
            w_in=jnp.concatenate([wi[:, o_rw:o_hg], wi[:, :o_rw], wi[:, o_hg:]], axis=1).astype(BF16),
            wo_s5=wo[:S5_WIDTH], wo_rw=wo[S5_WIDTH:S5_WIDTH + RWKV_WIDTH], wo_hg=wo[S5_WIDTH + RWKV_WIDTH:],
            s5=_s5_consts(s5_a_re[l], s5_a_im[l], s5_log_dt[l], s5_b_re[l], s5_b_im[l], s5_c_re[l], s5_c_im[l],
                          s5_d[l], s5_w_glu[l], s5_b_glu[l]),
            rwkv=_rwkv_consts(rwkv_mu[l], rwkv_w0[l], rwkv_w2[l], rwkv_a0[l], rwkv_a2[l], rwkv_g2[l],
                              rwkv_k_k[l], rwkv_k_a[l], rwkv_r_k[l], rwkv_ln_w[l], rwkv_ln_b[l]),
            hgrn_lb=lower_bounds[l].reshape(1, -1), hgrn_nw=hgrn_norm_w[l].reshape(1, -1),
        ))

    def zeros_like_state(s):
        return jnp.zeros((depth, nb) + s.shape[2:], F32)

    p_states = tuple(zeros_like_state(s) for s in
                     (state_s5_re, state_s5_im, state_rwkv_shift, state_rwkv_wkv, state_hgrn))
    s_states = (state_s5_re, state_s5_im, state_rwkv_shift, state_rwkv_wkv, state_hgrn)
    ffw = dict(f1g=ffn1_w_gate, f1u=ffn1_w_up, f1d=ffn1_w_down, f2g=ffn2_w_gate, f2u=ffn2_w_up, f2d=ffn2_w_down)
    (y_s, s5re_s, s5im_s, shift_s, wkv_s, hgrn_s), ffb = _run(
        x_sample.reshape(ns * dseq, d), ns, dseq, s_states, lw, ffw, norm_final)
    (y_p, s5re_p, s5im_p, shift_p, wkv_p, hgrn_p), _ = _run(
        x_prompt.reshape(nb * seq, d), nb, seq, p_states, lw, ffb, norm_final)
    return (y_p.reshape(nb, seq, d), y_s.reshape(ns, dseq, d), s5re_p, s5im_p, shift_p, wkv_p, hgrn_p,
            s5re_s, s5im_s, shift_s, wkv_s, hgrn_s)
```

```python
import functools
import math

import numpy as np
import jax
import jax.numpy as jnp
from jax import lax
from jax.experimental import pallas as pl
from jax.experimental.pallas import tpu as pltpu

F32 = jnp.float32
BF16 = jnp.bfloat16

NORM_EPS = 1e-6
RWKV_GN_EPS = 64e-5
HGRN_EPS = 1e-5

D_MODEL = 2048
S5_WIDTH = 512
S5_GROUP = 16
S5_GROUPS = 32
S5_STATE = 64
S5_LANES = S5_GROUPS * S5_STATE
S5_GB = 4
RWKV_WIDTH = 768
RWKV_HEAD = 64
RWKV_HEADS = 12
RWKV_PAIRS = 6
RWKV_SHIFT_WIDTH = 2560
RWKV_ZBLOCKS = RWKV_SHIFT_WIDTH // S5_WIDTH
RWKV_BLOCK = 16
RWKV_TM = 128
SEG_CHUNK = 256
HGRN_WIDTH = 768
HGRN_HEADS = 6
HGRN_DIM = 128
N_IN = 6144

SUBLANES = 8
LANES = 128
VMEM_LIMIT = 60 * 1024 * 1024


def _cparams(sem):
    return pltpu.CompilerParams(dimension_semantics=sem, vmem_limit_bytes=VMEM_LIMIT)


def _split2(x):
    hi = x.astype(BF16)
    lo = (x - hi.astype(F32)).astype(BF16)
    return hi, lo


def _split3(x):
    p1 = x.astype(BF16)
    r1 = x - p1.astype(F32)
    p2 = r1.astype(BF16)
    p3 = (r1 - p2.astype(F32)).astype(BF16)
    return p1, p2, p3


def _dot(a, b):
    return jnp.dot(a, b, preferred_element_type=F32)


def _dot_nt(a, b):
    return lax.dot_general(a, b, (((1,), (1,)), ((), ())), preferred_element_type=F32)


def _dot_tn(a, b):
    return lax.dot_general(a, b, (((0,), (0,)), ((), ())), preferred_element_type=F32)


def _sigmoid(x):
    return 1.0 / (1.0 + jnp.exp(-x))


def _ffn_kernel(*refs, cast, final):
    x_ref, nw_ref, wg_ref, wu_ref, wd_ref = refs[:5]
    refs = refs[5:]
    if final:
        fnw_ref = refs[0]
        refs = refs[1:]
    if cast:
        o_ref, wgb_ref, wub_ref, wdb_ref, h_ref = refs
    else:
        o_ref, h_ref = refs
    j = pl.program_id(1)

    @pl.when(j == 0)
    def _():
        rows = o_ref.shape[0]
        step = min(rows, 256)
        for r0 in range(0, rows, step):
            rs = slice(r0, r0 + step)
            x = x_ref[rs, :]
            ms = jnp.mean(x * x, axis=-1, keepdims=True)
            h_ref[rs, :] = (x * lax.rsqrt(ms + NORM_EPS) * nw_ref[...]).astype(BF16)
        o_ref[...] = jnp.zeros_like(o_ref)

    wg = wg_ref[...].astype(BF16)
    wu = wu_ref[...].astype(BF16)
    wd = wd_ref[...].astype(BF16)
    if cast:
        wgb_ref[...] = wg
        wub_ref[...] = wu
        wdb_ref[...] = wd
    h = h_ref[...]
    g = _dot(h, wg)
    u = _dot(h, wu)
    o_ref[...] += _dot((g * _sigmoid(g) * u).astype(BF16), wd)

    @pl.when(j == pl.num_programs(1) - 1)
    def _():
        rows = o_ref.shape[0]
        step = min(rows, 256)
        for r0 in range(0, rows, step):
            rs = slice(r0, r0 + step)
            o = x_ref[rs, :] + 0.5 * o_ref[rs, :]
            if final:
                o = o * lax.rsqrt(jnp.mean(o * o, axis=-1, keepdims=True) + NORM_EPS) * fnw_ref[...]
            o_ref[rs, :] = o


def _ffn(x, nw, wg, wu, wd, l, cast=False, fnw=None, tm=1024):
    m, d = x.shape
    ff = wg.shape[2]
    tf = 256 if cast else 512
    final = fnw is not None
    if final and not cast:
        tm = tm // 2
    out_specs = [pl.BlockSpec((tm, d), lambda i, j: (i, 0))]
    out_shape = [jax.ShapeDtypeStruct((m, d), F32)]
    if cast:
        assert m == tm
        out_specs += [pl.BlockSpec((d, tf), lambda i, j: (0, j)), pl.BlockSpec((d, tf), lambda i, j: (0, j)),
                      pl.BlockSpec((tf, d), lambda i, j: (j, 0))]
        out_shape += [jax.ShapeDtypeStruct((d, ff), BF16), jax.ShapeDtypeStruct((d, ff), BF16),
                      jax.ShapeDtypeStruct((ff, d), BF16)]
    vec = pl.BlockSpec((1, d), lambda i, j: (0, 0))
    outs = pl.pallas_call(
        functools.partial(_ffn_kernel, cast=cast, final=final),
        grid=(m // tm, ff // tf),
        in_specs=[
            pl.BlockSpec((tm, d), lambda i, j: (i, 0)),
            vec,
            pl.BlockSpec((None, d, tf), lambda i, j: (l, 0, j)),
            pl.BlockSpec((None, d, tf), lambda i, j: (l, 0, j)),
            pl.BlockSpec((None, tf, d), lambda i, j: (l, j, 0)),
        ] + ([vec] if final else []),
        out_specs=out_specs,
        out_shape=out_shape,
        scratch_shapes=[pltpu.VMEM((tm, d), BF16)],
        compiler_params=_cparams(("parallel", "arbitrary")),
        name="ffn_cast" if cast else "ffn",
    )(x, nw, wg, wu, wd, *((fnw,) if final else ()))
    return outs if cast else outs[0]


def _inproj_kernel(x_ref, nw_ref, w_ref, o_ref, h_ref):
    @pl.when(pl.program_id(1) == 0)
    def _():
        x = x_ref[...]
        ms = jnp.mean(x * x, axis=-1, keepdims=True)
        h_ref[...] = (x * lax.rsqrt(ms + NORM_EPS) * nw_ref[...]).astype(BF16)

    o_ref[...] = _dot(h_ref[...], w_ref[...])


def _inproj(x, nw, w, l, tm=1024, tn=1024):
    m, d = x.shape
    n = w.shape[2]
    return pl.pallas_call(
        _inproj_kernel,
        grid=(m // tm, n // tn),
        in_specs=[
            pl.BlockSpec((tm, d), lambda i, j: (i, 0)),
            pl.BlockSpec((1, d), lambda i, j: (0, 0)),
            pl.BlockSpec((None, d, tn), lambda i, j: (l, 0, j)),
        ],
        out_specs=pl.BlockSpec((tm, tn), lambda i, j: (i, j)),
        out_shape=jax.ShapeDtypeStruct((m, n), F32),
        scratch_shapes=[pltpu.VMEM((tm, d), BF16)],
        compiler_params=_cparams(("parallel", "arbitrary")),
        name="inproj",
    )(x, nw, w)


def _outproj_kernel(x_ref, a_ref, b_ref, c_ref, wa_ref, wb_ref, wc_ref, o_ref):
    acc = _dot(a_ref[...].astype(BF16), wa_ref[...])
    acc += _dot(b_ref[...].astype(BF16), wb_ref[...])
    acc += _dot(c_ref[...].astype(BF16), wc_ref[...])
    o_ref[...] = x_ref[...] + acc


def _outproj(x, ya, yb, yc, wa, wb, wc, tm=512):
    m, d = x.shape
    row = lambda i: (i, 0)
    whole = lambda i: (0, 0)
    return pl.pallas_call(
        _outproj_kernel,
        grid=(m // tm,),
        in_specs=[
            pl.BlockSpec((tm, d), row),
            pl.BlockSpec((tm, ya.shape[1]), row),
            pl.BlockSpec((tm, yb.shape[1]), row),
            pl.BlockSpec((tm, yc.shape[1]), row),
            pl.BlockSpec(wa.shape, whole),
            pl.BlockSpec(wb.shape, whole),
            pl.BlockSpec(wc.shape, whole),
        ],
        out_specs=pl.BlockSpec((tm, d), row),
        out_shape=jax.ShapeDtypeStruct((m, d), F32),
        compiler_params=_cparams(("parallel",)),
        name="outproj",
    )(x, ya, yb, yc, wa, wb, wc)


def _s5_kernel(u_ref, x0r_ref, x0i_ref, b3r_ref, b3i_ref, cr_ref, ci_ref, cst_ref, d_ref, wglu_ref,
               bglu_ref, y_ref, xlr_ref, xli_ref, xr_ref, xi_ref, cr_s, ci_s, *, tr, chained):
    c = pl.program_id(1)
    u = u_ref[...]
    uh, ul = _split2(u)
    for gb in range(S5_GB):
        sl = slice(gb * LANES, (gb + 1) * LANES)
        osl = slice(gb * 512, (gb + 1) * 512)
        if chained:
            xr_ref[:, osl] = _dot(uh[:, sl], b3r_ref[gb, 0:LANES, :])
            xi_ref[:, osl] = _dot(uh[:, sl], b3i_ref[gb, 0:LANES, :])
        else:
            lhs = jnp.concatenate([uh[:, sl], ul[:, sl], uh[:, sl]], axis=1)
            xr_ref[:, osl] = _dot(lhs, b3r_ref[gb])
            xi_ref[:, osl] = _dot(lhs, b3i_ref[gb])

    if chained:
        @pl.when(c == 0)
        def _():
            cr_s[...] = x0r_ref[0]
            ci_s[...] = x0i_ref[0]

    a1r, a1i, a2r, a2i, a4r, a4i, pr, pi = [cst_ref[i] for i in range(8)]

    def tile(j, carry):
        rows = pl.ds(pl.multiple_of(j * SUBLANES, SUBLANES), SUBLANES)
        xr = xr_ref[rows, :]
        xi = xi_ref[rows, :]
        for k, ar, ai in ((1, a1r, a1i), (2, a2r, a2i), (4, a4r, a4i)):
            sr = pltpu.roll(xr, k, 0)
            si = pltpu.roll(xi, k, 0)
            xr, xi = xr + ar * sr - ai * si, xi + ar * si + ai * sr
        if chained:
            x0r = cr_s[...]
            x0i = ci_s[...]
        else:
            x0r = x0r_ref[pl.ds(j, 1), :]
            x0i = x0i_ref[pl.ds(j, 1), :]
        x0r = jnp.broadcast_to(x0r, xr.shape)
        x0i = jnp.broadcast_to(x0i, xr.shape)
        xr, xi = xr + pr * x0r - pi * x0i, xi + pr * x0i + pi * x0r
        xr_ref[rows, :] = xr
        xi_ref[rows, :] = xi
        if chained:
            cr_s[...] = xr[SUBLANES - 1:SUBLANES, :]
            ci_s[...] = xi[SUBLANES - 1:SUBLANES, :]
        else:
            xlr_ref[pl.ds(j, 1), :] = xr[SUBLANES - 1:SUBLANES, :]
            xli_ref[pl.ds(j, 1), :] = xi[SUBLANES - 1:SUBLANES, :]
        return carry

    lax.fori_loop(0, tr // SUBLANES, tile, 0)

    if chained:
        @pl.when(c == pl.num_programs(1) - 1)
        def _():
            xlr_ref[0] = cr_s[...]
            xli_ref[0] = ci_s[...]

    ys = []
    for gb in range(S5_GB):
        osl = slice(gb * 512, (gb + 1) * 512)
        ys.append(_dot(xr_ref[:, osl].astype(BF16), cr_ref[gb]) + _dot(xi_ref[:, osl].astype(BF16), ci_ref[gb]))
    y = jnp.concatenate(ys, axis=1) + d_ref[...] * u
    y = 0.5 * y * (1.0 + jnp.tanh(math.sqrt(2.0 / math.pi) * (y + 0.044715 * (y * y * y))))
    z = _dot(y.astype(BF16), wglu_ref[...]) + bglu_ref[...]
    y_ref[...] = y * _sigmoid(z)


def _s5(p3, x0r, x0i, l, cs, nseq, t, tr=512):
    rows = nseq * t
    depth = x0r.shape[0]
    chained = t > SUBLANES
    if chained:
        grid = (nseq, t // tr)
        row_map = lambda n, c: (n * (t // tr) + c, 0)
        out_map = lambda n, c: (n * (t // tr) + c, 0)
        x0r = x0r.reshape(depth, nseq, 1, S5_LANES)
        x0i = x0i.reshape(depth, nseq, 1, S5_LANES)
        in_st = pl.BlockSpec((None, 1, 1, S5_LANES), lambda n, c: (l, n, 0, 0))
        st_spec = pl.BlockSpec((1, 1, S5_LANES), lambda n, c: (n, 0, 0))
        st_shape = jax.ShapeDtypeStruct((nseq, 1, S5_LANES), F32)
    else:
        assert t == SUBLANES
        grid = (rows // tr, 1)
        row_map = lambda n, c: (n, 0)
        out_map = lambda n, c: (n, 0)
        in_st = pl.BlockSpec((None, tr // SUBLANES, S5_LANES), lambda n, c: (l, n, 0))
        st_spec = pl.BlockSpec((tr // SUBLANES, S5_LANES), lambda n, c: (n, 0))
        st_shape = jax.ShapeDtypeStruct((nseq, S5_LANES), F32)
    whole3 = lambda n, c: (0, 0, 0)
    whole2 = lambda n, c: (0, 0)
    y, xlr, xli = pl.pallas_call(
        functools.partial(_s5_kernel, tr=tr, chained=chained),
        grid=grid,
        in_specs=[
            pl.BlockSpec((tr, S5_WIDTH), row_map),
            in_st, in_st,
            pl.BlockSpec(cs["b3r"].shape, whole3),
            pl.BlockSpec(cs["b3i"].shape, whole3),
            pl.BlockSpec(cs["cr"].shape, whole3),
            pl.BlockSpec(cs["ci"].shape, whole3),
            pl.BlockSpec(cs["scan"].shape, whole3),
            pl.BlockSpec((1, S5_WIDTH), whole2),
            pl.BlockSpec((S5_WIDTH, S5_WIDTH), whole2),
            pl.BlockSpec((1, S5_WIDTH), whole2),
        ],
        out_specs=[pl.BlockSpec((tr, S5_WIDTH), out_map), st_spec, st_spec],
        out_shape=[jax.ShapeDtypeStruct((rows, S5_WIDTH), F32), st_shape, st_shape],
        scratch_shapes=[pltpu.VMEM((tr, S5_LANES), F32), pltpu.VMEM((tr, S5_LANES), F32),
                        pltpu.VMEM((1, S5_LANES), F32), pltpu.VMEM((1, S5_LANES), F32)],
        compiler_params=_cparams(("parallel", "arbitrary")),
        name="s5_chain" if chained else "s5_tile",
    )(p3, x0r, x0i, cs["b3r"], cs["b3i"], cs["cr"], cs["ci"], cs["scan"], cs["d"], cs["wglu"], cs["bglu"])
    return y, xlr.reshape(nseq, S5_GROUPS, S5_STATE), xli.reshape(nseq, S5_GROUPS, S5_STATE)


def _s5_consts(a_re, a_im, log_dt, b_re, b_im, c_re, c_im, d, w_glu, b_glu):
    g, p, h = S5_GROUPS, S5_STATE, S5_GROUP
    dt = jnp.exp(log_dt)[:, None]
    mag = jnp.exp(a_re * dt)
    abr = mag * jnp.cos(a_im * dt)
    abi = mag * jnp.sin(a_im * dt)
    den = a_re * a_re + a_im * a_im
    qr = ((abr - 1.0) * a_re + abi * a_im) / den
    qi = (abi * a_re - (abr - 1.0) * a_im) / den
    bbr = qr[..., None] * b_re - qi[..., None] * b_im
    bbi = qr[..., None] * b_im + qi[..., None] * b_re
    pr, pi = [abr], [abi]
    for _ in range(SUBLANES - 1):
        pr, pi = pr + [pr[-1] * abr - pi[-1] * abi], pi + [pr[-1] * abi + pi[-1] * abr]
    pr = jnp.stack(pr).reshape(SUBLANES, g * p)
    pi = jnp.stack(pi).reshape(SUBLANES, g * p)
    row = jnp.arange(SUBLANES)[:, None]

    def lvl(k):
        return [jnp.where(row >= k, pr[k - 1][None, :], 0.0), jnp.where(row >= k, pi[k - 1][None, :], 0.0)]

    scan = jnp.stack(lvl(1) + lvl(2) + lvl(4) + [pr, pi]).astype(F32)
    eye = jnp.eye(SUBLANES, dtype=F32)

    def in_blocks(x):
        xx = x.reshape(S5_GB, SUBLANES, p, h).transpose(0, 1, 3, 2)
        blk = (xx[:, :, :, None, :] * eye[None, :, None, :, None]).reshape(S5_GB, SUBLANES * h, SUBLANES * p)
        hi = blk.astype(BF16)
        lo = (blk - hi.astype(F32)).astype(BF16)
        return jnp.concatenate([hi, hi, lo], axis=1)

    def out_blocks(x):
        xx = x.reshape(S5_GB, SUBLANES, h, p).transpose(0, 1, 3, 2)
        return (xx[:, :, :, None, :] * eye[None, :, None, :, None]).reshape(
            S5_GB, SUBLANES * p, SUBLANES * h).astype(BF16)

    return dict(b3r=in_blocks(bbr), b3i=in_blocks(bbi),
                cr=out_blocks(c_re), ci=out_blocks(-c_im), scan=scan,
                d=d.reshape(1, -1), wglu=w_glu.astype(BF16), bglu=b_glu.reshape(1, -1))


def _seg_sum(x, e2):
    hi, lo = _split2(x)
    cw = e2.shape[1]
    return jnp.concatenate(
        [_dot(jnp.concatenate([hi[:, c:c + cw], lo[:, c:c + cw]], axis=1), e2) for c in range(0, x.shape[1], cw)],
        axis=1)


def _rwkv_prep_kernel(*refs, nlev, chained, tiles_per_seq):
    nz = RWKV_ZBLOCKS
    z = jnp.concatenate([r[...] for r in refs[:nz]], axis=1)
    if chained:
        prev8 = jnp.concatenate([r[...] for r in refs[nz:2 * nz]], axis=1)
        sh_ref = refs[2 * nz]
        refs = refs[2 * nz + 1:]
    else:
        zp_ref = refs[nz]
        refs = refs[nz + 1:]
    (mu_ref, w0_ref, a0_ref, w2_ref, a2_ref, g2_ref, kk_ref, ka_ref, rk_ref, e2_ref, tril_ref, mk_ref, hm_ref,
     ap_o, rp_o, bt_o, kp_o, cc_o, y0_o, v_o, gm_o, g_o, bonus_o) = refs
    tm = RWKV_TM
    if chained:
        first = (pl.program_id(0) % tiles_per_seq) == 0
        prev_row = jnp.where(first, sh_ref[0], prev8[SUBLANES - 1:SUBLANES, :])
        rid = lax.broadcasted_iota(jnp.int32, z.shape, 0)
        zp = jnp.where(rid == 0, prev_row, pltpu.roll(z, 1, 0))
    else:
        zp = zp_ref[...]
    zm = z + (zp - z) * mu_ref[...]
    w = RWKV_WIDTH
    r = zm[:, 0:w]
    k = zm[:, w:2 * w]
    v = zm[:, 2 * w:3 * w]
    wa = zm[:, 3 * w:3 * w + LANES]
    gi = zm[:, 3 * w + LANES:3 * w + 2 * LANES]
    e2 = e2_ref[...]
    wl = w0_ref[...] + _dot(jnp.tanh(wa).astype(BF16), w2_ref[...])
    sp = jnp.maximum(-wl, 0.0) + jnp.log(1.0 + jnp.exp(-jnp.abs(wl)))
    ld = -jnp.exp(-sp - 0.5)
    a = _sigmoid(a0_ref[...] + _dot(wa.astype(BF16), a2_ref[...]))
    g_o[...] = _dot(_sigmoid(gi).astype(BF16), g2_ref[...])
    kk = k * kk_ref[...]
    kk = kk * lax.rsqrt(jnp.maximum(_seg_sum(kk * kk, e2), 1e-24))
    k2 = k * (1.0 + (a - 1.0) * ka_ref[...])
    b = kk * a
    bonus_o[...] = _seg_sum(r * k2 * rk_ref[...], e2) * v
    v_o[...] = v

    p1, p2, p3 = _split3(ld)
    cum = _dot(tril_ref[...], jnp.concatenate([p1, p2, p3], axis=0))
    gam = jnp.exp(cum)
    ginv = jnp.exp(-cum)
    gm_o[...] = gam
    alpha = kk * jnp.exp(cum - ld)
    beta = b * ginv
    kap = k2 * ginv
    rho = r * gam
    bt_o[...] = beta
    kp_o[...] = kap

    strict = mk_ref[0]
    incl = mk_ref[1]
    eye = mk_ref[2]
    heads = [(j, e) for j in range(RWKV_PAIRS) for e in range(2)]
    nh = len(heads)

    def lanes(j):
        return slice(j * LANES, (j + 1) * LANES)

    al = [alpha[:, lanes(j)] * hm_ref[e] for j, e in heads]
    rh = [rho[:, lanes(j)] * hm_ref[e] for j, e in heads]
    vh = [(v[:, lanes(j)] * hm_ref[e]).astype(BF16) for j, e in heads]
    yk = [jnp.concatenate([beta[:, lanes(j)], kap[:, lanes(j)]], axis=0).astype(BF16) for j in range(RWKV_PAIRS)]
    wu = [_dot_nt(jnp.concatenate(_split2(al[i]), axis=1), jnp.concatenate([yk[j], yk[j]], axis=1))
          for i, (j, e) in enumerate(heads)]
    wl = [_dot_nt(rh[i].astype(BF16), yk[j]) for i, (j, e) in enumerate(heads)]
    mm = [wu[i][:, :tm] * strict for i in range(nh)]
    nn = [wu[i][:, tm:] * strict for i in range(nh)]
    pp = [(wl[i][:, :tm] * incl).astype(BF16) for i in range(nh)]
    qq = [(wl[i][:, tm:] * incl).astype(BF16) for i in range(nh)]
    tinv = [eye - mm[i] * mk_ref[3] for i in range(nh)]
    for lv in range(1, nlev):
        th = [tinv[i].astype(BF16) for i in range(nh)]
        tmp = [_dot(jnp.concatenate([th[i], th[i]], axis=1),
                    jnp.concatenate(_split2(mm[i] * mk_ref[3 + lv]), axis=0)) for i in range(nh)]
        tinv = [tinv[i] - _dot(jnp.concatenate(_split2(tmp[i]), axis=1),
                               jnp.concatenate([th[i], th[i]], axis=0)) for i in range(nh)]
    th = [tinv[i].astype(BF16) for i in range(nh)]
    nv = [_dot(jnp.concatenate(_split2(nn[i]), axis=1), jnp.concatenate([vh[i], vh[i]], axis=0)) for i in range(nh)]
    qv = [_dot(qq[i], vh[i]) for i in range(nh)]
    tac = [_dot(jnp.concatenate([th[i], th[i]], axis=1),
                jnp.concatenate(_split2(jnp.concatenate([al[i], nv[i]], axis=1)), axis=0)) for i in range(nh)]
    pac = [_dot(pp[i], tac[i].astype(BF16)) for i in range(nh)]
    ap = [tac[i][:, :LANES] for i in range(nh)]
    cc = [tac[i][:, LANES:] for i in range(nh)]
    rp = [rh[i] - pac[i][:, :LANES] for i in range(nh)]
    y0 = [qv[i] - pac[i][:, LANES:] for i in range(nh)]
    for j in range(RWKV_PAIRS):
        ap_o[:, lanes(j)] = ap[2 * j] + ap[2 * j + 1]
        rp_o[:, lanes(j)] = rp[2 * j] + rp[2 * j + 1]
        cc_o[:, lanes(j)] = cc[2 * j] + cc[2 * j + 1]
        y0_o[:, lanes(j)] = y0[2 * j] + y0[2 * j + 1]


def _rwkv_tables(tm, blk):
    t = np.arange(tm)
    same = (t[:, None] // blk) == (t[None, :] // blk)
    tril = (same & (t[None, :] <= t[:, None])).astype(np.float32)
    masks = [same & (t[None, :] < t[:, None]), same & (t[None, :] <= t[:, None]), np.eye(tm, dtype=bool)]
    n = 2
    while n <= blk:
        m = n // 2
        masks.append(((t[:, None] // n) == (t[None, :] // n)) & ((t[:, None] % n) >= m) & ((t[None, :] % n) < m))
        n *= 2
    nlev = len(masks) - 3
    hm = np.stack([(np.arange(LANES) < RWKV_HEAD), (np.arange(LANES) >= RWKV_HEAD)]).astype(np.float32)
    return (jnp.asarray(np.concatenate([tril, tril, tril], axis=1), BF16),
            jnp.asarray(np.stack(masks).astype(np.float32)), jnp.asarray(hm.reshape(2, 1, LANES)), nlev)


def _rwkv_prep(p3, prev, cs, nseq, t):
    m = nseq * t
    tm = RWKV_TM
    blk = min(RWKV_BLOCK, t)
    chained = t >= tm
    tril3, masks, hm, nlev = _rwkv_tables(tm, blk)
    w = RWKV_WIDTH
    row = lambda i: (i, 0)
    whole = lambda i: (0, 0)
    whole3 = lambda i: (0, 0, 0)
    vec = pl.BlockSpec((1, w), whole)
    zspecs = [pl.BlockSpec((tm, S5_WIDTH), lambda i, c=c: (i, 1 + c)) for c in range(RWKV_ZBLOCKS)]
    if chained:
        tps = t // tm
        lead = zspecs + [pl.BlockSpec((SUBLANES, S5_WIDTH),
                                      lambda i, c=c: (jnp.maximum(i * (tm // SUBLANES) - 1, 0), 1 + c))
                         for c in range(RWKV_ZBLOCKS)]
        lead.append(pl.BlockSpec((1, 1, RWKV_SHIFT_WIDTH), lambda i: (i // tps, 0, 0)))
        args = (p3,) * (2 * RWKV_ZBLOCKS) + (prev,)
    else:
        tps = 1
        lead = zspecs + [pl.BlockSpec((tm, RWKV_SHIFT_WIDTH), row)]
        args = (p3,) * RWKV_ZBLOCKS + (prev,)
    return pl.pallas_call(
        functools.partial(_rwkv_prep_kernel, nlev=nlev, chained=chained, tiles_per_seq=tps),
        grid=(m // tm,),
        in_specs=lead + [
            pl.BlockSpec((1, RWKV_SHIFT_WIDTH), whole),
            vec, vec,
            pl.BlockSpec((LANES, w), whole), pl.BlockSpec((LANES, w), whole), pl.BlockSpec((LANES, w), whole),
            vec, vec, vec,
            pl.BlockSpec((2 * SEG_CHUNK, SEG_CHUNK), whole),
            pl.BlockSpec(tril3.shape, whole), pl.BlockSpec(masks.shape, whole3), pl.BlockSpec(hm.shape, whole3),
        ],
        out_specs=[pl.BlockSpec((tm, w), row)] * 10,
        out_shape=[jax.ShapeDtypeStruct((m, w), F32)] * 10,
        compiler_params=_cparams(("parallel",)),
        name="rwkv_prep",
    )(*args, cs["mu"], cs["w0"], cs["a0"], cs["w2"], cs["a2"], cs["g2"], cs["k_k"], cs["k_a"], cs["r_k"],
      cs["e2w"], tril3, masks, hm)


def _rwkv_rec_kernel(*refs, nbp, tc, blk, aliased):
    ap_ref, rp_ref, bt_ref, kp_ref, cc_ref, y0_ref, v_ref, gm_ref, s0_ref, bd_ref = refs[:10]
    y_ref, so_ref, s_ref = refs[10 + (1 if aliased else 0):]
    c = pl.program_id(1)
    hd = RWKV_HEAD
    pairs = [(n, j) for n in range(nbp) for j in range(RWKV_PAIRS)]

    @pl.when(c == 0)
    def _():
        zero = jnp.zeros((hd, hd), F32)
        for idx, (n, j) in enumerate(pairs):
            top = jnp.concatenate([s0_ref[n, 2 * j], zero], axis=1)
            bot = jnp.concatenate([zero, s0_ref[n, 2 * j + 1]], axis=1)
            s_ref[idx] = jnp.concatenate([top, bot], axis=0)

    bd = bd_ref[...]

    def block(bi, carry):
        rows = pl.ds(pl.multiple_of(bi * blk, blk), blk)
        gmats = []
        for idx, (n, j) in enumerate(pairs):
            ls = slice(j * LANES, (j + 1) * LANES)
            sh, sl = _split2(s_ref[idx])
            xh = jnp.concatenate([ap_ref[n, rows, ls], rp_ref[n, rows, ls]], axis=0).astype(BF16)
            gmats.append(_dot_nt(jnp.concatenate([xh, xh], axis=1), jnp.concatenate([sh, sl], axis=1)))
        upds = []
        for idx, (n, j) in enumerate(pairs):
            ls = slice(j * LANES, (j + 1) * LANES)
            gmat = gmats[idx]
            e = -gmat[:blk] - cc_ref[n, rows, ls]
            y_ref[n, rows, ls] = gmat[blk:] + y0_ref[n, rows, ls]
            evh, evl = _split2(jnp.concatenate([e, v_ref[n, rows, ls]], axis=0))
            bkh, bkl = _split2(jnp.concatenate([bt_ref[n, rows, ls], kp_ref[n, rows, ls]], axis=0))
            upds.append(_dot_tn(jnp.concatenate([evh, evh, evl], axis=0), jnp.concatenate([bkh, bkl, bkh], axis=0)))
        for idx, (n, j) in enumerate(pairs):
            ls = slice(j * LANES, (j + 1) * LANES)
            gl = gm_ref[n, rows, ls][blk - 1:blk, :]
            s_ref[idx] = (s_ref[idx] + upds[idx] * bd) * gl
        return carry

    lax.fori_loop(0, tc // blk, block, 0)

    @pl.when(c == pl.num_programs(1) - 1)
    def _():
        for idx, (n, j) in enumerate(pairs):
            s = s_ref[idx]
            so_ref[n, 2 * j] = s[:hd, :hd]
            so_ref[n, 2 * j + 1] = s[hd:, hd:]


def _rwkv_rec(arrs, s0_all, l, so_prev, cs, nseq, t, nbp=4, tc=128):
    w = RWKV_WIDTH
    tc = min(tc, t)
    blk = min(RWKV_BLOCK, t)
    depth = s0_all.shape[0]
    seq_map = lambda i, c: (i, c, 0)
    st_map = lambda i, c: (l, i, 0, 0, 0)
    blkspec = pl.BlockSpec((nbp, tc, w), seq_map)
    st = pl.BlockSpec((None, nbp, RWKV_HEADS, RWKV_HEAD, RWKV_HEAD), st_map)
    aliased = so_prev is not None
    in_specs = [blkspec] * 8 + [st, pl.BlockSpec((LANES, LANES), lambda i, c: (0, 0))]
    args = list(arrs) + [s0_all, cs["bd"]]
    aliases = {}
    if aliased:
        in_specs.append(pl.BlockSpec(memory_space=pl.ANY))
        args.append(so_prev)
        aliases = {10: 1}
    return pl.pallas_call(
        functools.partial(_rwkv_rec_kernel, nbp=nbp, tc=tc, blk=blk, aliased=aliased),
        grid=(nseq // nbp, t // tc),
        in_specs=in_specs,
        out_specs=[blkspec, st],
        out_shape=[jax.ShapeDtypeStruct((nseq, t, w), F32),
                   jax.ShapeDtypeStruct((depth, nseq, RWKV_HEADS, RWKV_HEAD, RWKV_HEAD), F32)],
        scratch_shapes=[pltpu.VMEM((nbp * RWKV_PAIRS, LANES, LANES), F32)],
        input_output_aliases=aliases,
        compiler_params=_cparams(("parallel", "arbitrary")),
        name="rwkv_rec",
    )(*args)


def _rwkv_post_kernel(y_ref, bonus_ref, g_ref, lnw_ref, lnb_ref, e2_ref, o_ref):
    y = y_ref[...]
    e2 = e2_ref[...]
    mean = _seg_sum(y, e2) * (1.0 / RWKV_HEAD)
    yc = y - mean
    var = _seg_sum(yc * yc, e2) * (1.0 / RWKV_HEAD)
    yn = yc * lax.rsqrt(var + RWKV_GN_EPS) * lnw_ref[...] + lnb_ref[...]
    o_ref[...] = (yn + bonus_ref[...]) * g_ref[...]


def _rwkv_post(y, bonus, g, cs, tm=512):
    m, w = y.shape
    tm = min(tm, m)
    row = lambda i: (i, 0)
    whole = lambda i: (0, 0)
    return pl.pallas_call(
        _rwkv_post_kernel,
        grid=(m // tm,),
        in_specs=[pl.BlockSpec((tm, w), row)] * 3 + [pl.BlockSpec((1, w), whole)] * 2
        + [pl.BlockSpec((2 * SEG_CHUNK, SEG_CHUNK), whole)],
        out_specs=pl.BlockSpec((tm, w), row),
        out_shape=jax.ShapeDtypeStruct((m, w), F32),
        compiler_params=_cparams(("parallel",)),
        name="rwkv_post",
    )(y, bonus, g, cs["ln_w"], cs["ln_b"], cs["e2w"])


def _rwkv_consts(mu, w0, w2, a0, a2, g2, k_k, k_a, r_k, ln_w, ln_b):
    w = RWKV_WIDTH
    seg = np.arange(SEG_CHUNK) // RWKV_HEAD
    e_w = (seg[:, None] == seg[None, :]).astype(np.float32)
    seg = np.arange(LANES) // RWKV_HEAD
    bd = (seg[:, None] == seg[None, :]).astype(np.float32)
    z64 = jnp.zeros((RWKV_HEAD, w), F32)
    return dict(
        mu=mu.reshape(1, -1), w0=w0.reshape(1, -1), a0=a0.reshape(1, -1),
        w2=jnp.concatenate([w2, z64], axis=0).astype(BF16),
        a2=jnp.concatenate([z64, a2], axis=0).astype(BF16),
        g2=g2.astype(BF16), k_k=k_k.reshape(1, -1), k_a=k_a.reshape(1, -1), r_k=r_k.reshape(1, -1),
        ln_w=ln_w.reshape(1, -1), ln_b=ln_b.reshape(1, -1),
        e2w=jnp.asarray(np.concatenate([e_w, e_w], axis=0), BF16),
        bd=jnp.asarray(bd, F32))


def _rwkv(p3, shift_l, wkv_all, l, so_prev, cs, nseq, t):
    if t >= RWKV_TM:
        prev = shift_l.reshape(nseq, 1, RWKV_SHIFT_WIDTH)
    else:
        z3 = p3.reshape(nseq, t, N_IN)[:, :, S5_WIDTH:S5_WIDTH + RWKV_SHIFT_WIDTH]
        prev = jnp.concatenate([shift_l[:, None, :], z3[:, :-1]], axis=1).reshape(nseq * t, RWKV_SHIFT_WIDTH)
    outs = _rwkv_prep(p3, prev, cs, nseq, t)
    arrs = [a.reshape(nseq, t, RWKV_WIDTH) for a in outs[:8]]
    y, so = _rwkv_rec(arrs, wkv_all, l, so_prev, cs, nseq, t)
    return _rwkv_post(y.reshape(nseq * t, RWKV_WIDTH), outs[9], outs[8], cs), so


def _hgrn_tables(c):
    t = np.arange(c)
    j = np.arange(c)
    mats = [(j[None, :] <= t[:, None]), (j[None, :] > t[:, None])]
    masks = []
    n = 2
    while n <= c:
        m = n // 2
        bs = (t // n) * n
        hi_half = (t % n) >= m
        mats.append((hi_half[:, None] & (j[None, :] >= (bs + m)[:, None]) & (j[None, :] <= t[:, None]))
                    | ((~hi_half)[:, None] & (j[None, :] > t[:, None]) & (j[None, :] <= (bs + m - 1)[:, None])))
        masks.append(((t[:, None] // n) == (t[None, :] // n)) & hi_half[:, None] & (~hi_half)[None, :])
        n *= 2
    w = np.concatenate(mats, axis=0).astype(np.float32)
    w3 = np.concatenate([w, w, w], axis=1)
    return jnp.asarray(w3, BF16), jnp.asarray(np.stack(masks).astype(np.float32)), len(masks)


def _hgrn_kernel(*refs, rb, c, nlev, chained, aliased, sub):
    q_ref, f_ref, i_ref, g_ref, s0_ref, lb_ref, nw_ref, w3_ref, mk_ref = refs[:9]
    o_ref, so_ref, st_ref = refs[9 + (1 if aliased else 0):]
    cidx = pl.program_id(1)
    hdim = HGRN_DIM

    if chained:
        @pl.when(cidx == 0)
        def _():
            for h in range(HGRN_HEADS):
                st_ref[h] = s0_ref[0, h].T

    w3 = w3_ref[...]
    row_i = lax.broadcasted_iota(jnp.int32, (c, c), 0)
    col_i = lax.broadcasted_iota(jnp.int32, (c, c), 1)
    eye = (row_i == col_i).astype(F32)

    heads = range(HGRN_HEADS)

    def lanes(h):
        return slice(h * hdim, (h + 1) * hdim)

    def block(it, carry):
        rows_s = [pl.ds(pl.multiple_of((it * sub + s) * c, c), c) for s in range(sub)]
        items = [(s, h) for s in range(sub) for h in heads]
        fg = {k: lb_ref[:, lanes(k[1])] + (1.0 - lb_ref[:, lanes(k[1])]) * _sigmoid(f_ref[rows_s[k[0]], lanes(k[1])])
              for k in items}
        kk = {k: 1.0 - fg[k] for k in items}
        qs = {}
        for k in items:
            q = q_ref[rows_s[k[0]], lanes(k[1])]
            qs[k] = q * _sigmoid(q)
        ex = {k: jnp.exp(_dot(w3, jnp.concatenate(_split3(jnp.log(fg[k])), axis=0))) for k in items}
        vb = {k: i_ref[rows_s[k[0]], lanes(k[1])].astype(BF16) for k in items}
        lev = {(k, lv): _dot_nt((qs[k] * ex[k][(2 + lv) * c:(3 + lv) * c]).astype(BF16),
                                (kk[k] * ex[k][(2 + lv) * c:(3 + lv) * c]).astype(BF16))
               for k in items for lv in range(nlev)}
        att = {}
        for k in items:
            a = eye * jnp.sum(qs[k] * kk[k], axis=-1, keepdims=True)
            for lv in range(nlev):
                a = a + jnp.where(mk_ref[lv] > 0.0, lev[(k, lv)], 0.0)
            att[k] = a.astype(BF16)
        intra = {k: _dot(att[k], vb[k]) for k in items}
        kv = {k: _dot_tn(vb[k], (kk[k] * ex[k][c:2 * c]).astype(BF16)) for k in items}
        qe = {k: (qs[k] * ex[k][0:c]).astype(BF16) for k in items}
        st = {h: st_ref[h] for h in heads} if chained else None
        for s in range(sub):
            if not chained:
                st = {h: s0_ref[it * sub + s, h].T for h in heads}
            inter = {h: _dot_nt(qe[(s, h)], st[h].astype(BF16)) for h in heads}
            st = {h: st[h] * ex[(s, h)][c - 1:c, :] + kv[(s, h)] for h in heads}
            if not chained:
                for h in heads:
                    so_ref[it * sub + s, h] = st[h].T
            for h in heads:
                o = inter[h] + intra[(s, h)]
                o = o * lax.rsqrt(jnp.mean(o * o, axis=-1, keepdims=True) + HGRN_EPS)
                gg = g_ref[rows_s[s], lanes(h)]
                o_ref[rows_s[s], lanes(h)] = o * nw_ref[:, lanes(h)] * (gg * _sigmoid(gg))
        if chained:
            for h in heads:
                st_ref[h] = st[h]
        return carry

    lax.fori_loop(0, rb // (c * sub), block, 0)

    if chained:
        @pl.when(cidx == pl.num_programs(1) - 1)
        def _():
            for h in range(HGRN_HEADS):
                so_ref[0, h] = st_ref[h].T


def _hgrn(p3, s0_all, l, so_prev, lb, nw, nseq, t):
    rows = nseq * t
    depth = s0_all.shape[0]
    c = math.gcd(t, 64)
    chained = t > c
    w3, masks, nlev = _hgrn_tables(c)
    w = HGRN_WIDTH
    if chained:
        rb = 256
        nblk = t // rb
        grid = (nseq, nblk)
        col = lambda cb: (lambda n, k: (n * nblk + k, cb))
        out_map = lambda n, k: (n * nblk + k, 0)
        nsb = 1
    else:
        nsb = 16
        rb = nsb * c
        grid = (rows // rb, 1)
        col = lambda cb: (lambda n, k: (n, cb))
        out_map = lambda n, k: (n, 0)
    st_spec = pl.BlockSpec((None, nsb, HGRN_HEADS, HGRN_DIM, HGRN_DIM), lambda n, k: (l, n, 0, 0, 0))
    whole2 = lambda n, k: (0, 0)
    whole3 = lambda n, k: (0, 0, 0)
    aliased = so_prev is not None
    in_specs = [pl.BlockSpec((rb, w), col(4)), pl.BlockSpec((rb, w), col(5)), pl.BlockSpec((rb, w), col(6)),
                pl.BlockSpec((rb, w), col(7)), st_spec,
                pl.BlockSpec((1, w), whole2), pl.BlockSpec((1, w), whole2),
                pl.BlockSpec(w3.shape, whole2), pl.BlockSpec(masks.shape, whole3)]
    args = [p3, p3, p3, p3, s0_all, lb, nw, w3, masks]
    aliases = {}
    if aliased:
        in_specs.append(pl.BlockSpec(memory_space=pl.ANY))
        args.append(so_prev)
        aliases = {9: 1}
    return pl.pallas_call(
        functools.partial(_hgrn_kernel, rb=rb, c=c, nlev=nlev, chained=chained, aliased=aliased,
                          sub=2 if chained else 4),
        grid=grid,
        in_specs=in_specs,
        out_specs=[pl.BlockSpec((rb, w), out_map), st_spec],
        out_shape=[jax.ShapeDtypeStruct((rows, w), F32),
                   jax.ShapeDtypeStruct((depth, nseq, HGRN_HEADS, HGRN_DIM, HGRN_DIM), F32)],
        scratch_shapes=[pltpu.VMEM((HGRN_HEADS, HGRN_DIM, HGRN_DIM), F32)],
        input_output_aliases=aliases,
        compiler_params=_cparams(("parallel", "arbitrary")),
        name="hgrn_chain" if chained else "hgrn_tile",
    )(*args)


def _run(x, nseq, t, states, lw, ffw, norm_final):
    s5_re0, s5_im0, shift0, wkv0, hgrn0 = states
    depth = len(lw)
    s5_re0 = s5_re0.reshape(depth, nseq, S5_LANES)
    s5_im0 = s5_im0.reshape(depth, nseq, S5_LANES)
    s5_re_l, s5_im_l, shift_l = [], [], []
    wkv_out = None
    hgrn_out = None
    cast = not isinstance(ffw["f1g"], list)
    ffb = {k: [] for k in ffw}

    def ffn(x, nw, names, l, fnw=None):
        if not cast:
            return _ffn(x, nw, *(ffw[k][l][None] for k in names), 0, fnw=fnw)
        res = _ffn(x, nw, *(ffw[k] for k in names), l, cast=True, fnw=fnw)
        for k, wb in zip(names, res[1:]):
            ffb[k].append(wb)
        return res[0]

    for l in range(depth):
        w = lw[l]
        x = ffn(x, w["norm_ffn1"], ("f1g", "f1u", "f1d"), l)
        p3 = _inproj(x, w["norm_mix"], w["w_in"], l)
        y_s5, s_re, s_im = _s5(p3, s5_re0, s5_im0, l, w["s5"], nseq, t)
        y_rw, wkv_out = _rwkv(p3, shift0[l], wkv0, l, wkv_out, w["rwkv"], nseq, t)
        y_hg, hgrn_out = _hgrn(p3, hgrn0, l, hgrn_out, w["hgrn_lb"], w["hgrn_nw"], nseq, t)
        x = _outproj(x, y_s5, y_rw, y_hg, w["wo_s5"], w["wo_rw"], w["wo_hg"])
        x = ffn(x, w["norm_ffn2"], ("f2g", "f2u", "f2d"), l,
                fnw=norm_final.reshape(1, -1) if l == depth - 1 else None)
        s5_re_l.append(s_re)
        s5_im_l.append(s_im)
        shift_l.append(p3.reshape(nseq, t, N_IN)[:, t - 1, S5_WIDTH:S5_WIDTH + RWKV_SHIFT_WIDTH])
    return (x,jnp.stack(s5_re_l), jnp.stack(s5_im_l), jnp.stack(shift_l), wkv_out, hgrn_out), ffb


def kernel(x_prompt, x_sample, state_s5_re, state_s5_im, state_rwkv_shift, state_rwkv_wkv, state_hgrn, norm_ffn1, ffn1_w_gate, ffn1_w_up, ffn1_w_down, norm_mix, w_in, s5_a_re, s5_a_im, s5_log_dt, s5_b_re, s5_b_im, s5_c_re, s5_c_im, s5_d, s5_w_glu, s5_b_glu, rwkv_mu, rwkv_w0, rwkv_w2, rwkv_a0, rwkv_a2, rwkv_g2, rwkv_k_k, rwkv_k_a, rwkv_r_k, rwkv_ln_w, rwkv_ln_b, hgrn_lb_raw, hgrn_norm_w, w_out, norm_ffn2, ffn2_w_gate, ffn2_w_up, ffn2_w_down, norm_final):
    depth = w_in.shape[0]
    nb, seq, d = x_prompt.shape
    ns, dseq, _ = x_sample.shape

    p_lb = jax.nn.softmax(hgrn_lb_raw.astype(F32), axis=0)
    lower_bounds = jnp.cumsum(p_lb, axis=0) - p_lb[0]

    w_in_b = w_in.astype(BF16)
    lw = []
    for l in range(depth):
        wo = w_out[l].astype(BF16)
        lw.append(dict(
            w_in=w_in_b,
            norm_ffn1=norm_ffn1[l].reshape(1, -1), norm_mix=norm_mix[l].reshape(1, -1),
            norm_ffn2=norm_ffn2[l].reshape(1, -1),
            wo_s5=wo[:S5_WIDTH], wo_rw=wo[S5_WIDTH:S5_WIDTH + RWKV_WIDTH], wo_hg=wo[S5_WIDTH + RWKV_WIDTH:],
            s5=_s5_consts(s5_a_re[l], s5_a_im[l], s5_log_dt[l], s5_b_re[l], s5_b_im[l], s5_c_re[l], s5_c_im[l],
                          s5_d[l], s5_w_glu[l], s5_b_glu[l]),
            rwkv=_rwkv_consts(rwkv_mu[l], rwkv_w0[l], rwkv_w2[l], rwkv_a0[l], rwkv_a2[l], rwkv_g2[l],
                              rwkv_k_k[l], rwkv_k_a[l], rwkv_r_k[l], rwkv_ln_w[l], rwkv_ln_b[l]),
            hgrn_lb=lower_bounds[l].reshape(1, -1), hgrn_nw=hgrn_norm_w[l].reshape(1, -1),
        ))

    def zeros_like_state(s):
        return jnp.zeros((depth, nb) + s.shape[2:], F32)

    p_states = tuple(zeros_like_state(s) for s in
                     (state_s5_re, state_s5_im, state_rwkv_shift, state_rwkv_wkv, state_hgrn))
    s_states = (state_s5_re, state_s5_im, state_rwkv_shift, state_rwkv_wkv, state_hgrn)
    ffw = dict(f1g=ffn1_w_gate, f1u=ffn1_w_up, f1d=ffn1_w_down, f2g=ffn2_w_gate, f2u=ffn2_w_up, f2d=ffn2_w_down)
    (y_s, s5re_s, s5im_s, shift_s, wkv_s, hgrn_s), ffb = _run(
        x_sample.reshape(ns * dseq, d), ns, dseq, s_states, lw, ffw, norm_final)
    (y_p, s5re_p, s5im_p, shift_p, wkv_p, hgrn_p), _ = _run(
        x_prompt.reshape(nb * seq, d), nb, seq, p_states, lw, ffb, norm_final)
    return (y_p.reshape(nb, seq, d), y_s.reshape(ns, dseq, d), s5re_p, s5im_p, shift_p, wkv_p, hgrn_p,
            s5re_s, s5im_s, shift_s, wkv_s, hgrn_s)
```

```python
import functools
import math

import numpy as np
import jax
import jax.numpy as jnp
from jax import lax
from jax.experimental import pallas as pl
from jax.experimental.pallas import tpu as pltpu

F32 = jnp.float32
BF16 = jnp.bfloat16

NORM_EPS = 1e-6
RWKV_GN_EPS = 64e-5
HGRN_EPS = 1e-5

D_MODEL = 2048
S5_WIDTH = 512
S5_GROUP = 16
S5_GROUPS = 32
S5_STATE = 64
S5_LANES = S5_GROUPS * S5_STATE
S5_GB = 4
S5_TR = 512
S5_SEG = S5_TR // 8
RWKV_WIDTH = 768
RWKV_HEAD = 64
RWKV_HEADS = 12
RWKV_PAIRS = 6
RWKV_SHIFT_WIDTH = 2560
RWKV_ZBLOCKS = RWKV_SHIFT_WIDTH // S5_WIDTH
RWKV_BLOCK = 16
RWKV_TM = 128
SEG_CHUNK = 256
HGRN_WIDTH = 768
HGRN_HEADS = 6
HGRN_DIM = 128
N_IN = 6144

SUBLANES = 8
LANES = 128
VMEM_LIMIT = 60 * 1024 * 1024


def _cparams(sem):
    return pltpu.CompilerParams(dimension_semantics=sem, vmem_limit_bytes=VMEM_LIMIT)


def _split2(x):
    hi = x.astype(BF16)
    lo = (x - hi.astype(F32)).astype(BF16)
    return hi, lo


def _split3(x):
    p1 = x.astype(BF16)
    r1 = x - p1.astype(F32)
    p2 = r1.astype(BF16)
    p3 = (r1 - p2.astype(F32)).astype(BF16)
    return p1, p2, p3


def _dot(a, b):
    return jnp.dot(a, b, preferred_element_type=F32)


def _dot_nt(a, b):
    return lax.dot_general(a, b, (((1,), (1,)), ((), ())), preferred_element_type=F32)


def _dot_tn(a, b):
    return lax.dot_general(a, b, (((0,), (0,)), ((), ())), preferred_element_type=F32)


def _sigmoid(x):
    return 1.0 / (1.0 + jnp.exp(-x))


def _ffn_kernel(*refs, cast, final):
    x_ref, nw_ref, wg_ref, wu_ref, wd_ref = refs[:5]
    refs = refs[5:]
    if final:
        fnw_ref = refs[0]
        refs = refs[1:]
    if cast:
        o_ref, wgb_ref, wub_ref, wdb_ref, h_ref = refs
    else:
        o_ref, h_ref = refs
    j = pl.program_id(1)

    @pl.when(j == 0)
    def _():
        rows = o_ref.shape[0]
        step = min(rows, 256)
        for r0 in range(0, rows, step):
            rs = slice(r0, r0 + step)
            x = x_ref[rs, :]
            ms = jnp.mean(x * x, axis=-1, keepdims=True)
            h_ref[rs, :] = (x * lax.rsqrt(ms + NORM_EPS) * nw_ref[...]).astype(BF16)
        o_ref[...] = jnp.zeros_like(o_ref)

    wg = wg_ref[...].astype(BF16)
    wu = wu_ref[...].astype(BF16)
    wd = wd_ref[...].astype(BF16)
    if cast:
        wgb_ref[...] = wg
        wub_ref[...] = wu
        wdb_ref[...] = wd
    h = h_ref[...]
    g = _dot(h, wg)
    u = _dot(h, wu)
    o_ref[...] += _dot((g * _sigmoid(g) * u).astype(BF16), wd)

    @pl.when(j == pl.num_programs(1) - 1)
    def _():
        rows = o_ref.shape[0]
        step = min(rows, 256)
        for r0 in range(0, rows, step):
            rs = slice(r0, r0 + step)
            o = x_ref[rs, :] + 0.5 * o_ref[rs, :]
            if final:
                o = o * lax.rsqrt(jnp.mean(o * o, axis=-1, keepdims=True) + NORM_EPS) * fnw_ref[...]
            o_ref[rs, :] = o


def _ffn(x, nw, wg, wu, wd, l, cast=False, fnw=None, tm=1024):
    m, d = x.shape
    ff = wg.shape[2]
    tf = 256 if cast else 512
    final = fnw is not None
    if final and not cast:
        tm = tm // 2
    out_specs = [pl.BlockSpec((tm, d), lambda i, j: (i, 0))]
    out_shape = [jax.ShapeDtypeStruct((m, d), F32)]
    if cast:
        assert m == tm
        out_specs += [pl.BlockSpec((d, tf), lambda i, j: (0, j)), pl.BlockSpec((d, tf), lambda i, j: (0, j)),
                      pl.BlockSpec((tf, d), lambda i, j: (j, 0))]
        out_shape += [jax.ShapeDtypeStruct((d, ff), BF16), jax.ShapeDtypeStruct((d, ff), BF16),
                      jax.ShapeDtypeStruct((ff, d), BF16)]
    vec = pl.BlockSpec((1, d), lambda i, j: (0, 0))
    outs = pl.pallas_call(
        functools.partial(_ffn_kernel, cast=cast, final=final),
        grid=(m // tm, ff // tf),
        in_specs=[
            pl.BlockSpec((tm, d), lambda i, j: (i, 0)),
            vec,
            pl.BlockSpec((None, d, tf), lambda i, j: (l, 0, j)),
            pl.BlockSpec((None, d, tf), lambda i, j: (l, 0, j)),
            pl.BlockSpec((None, tf, d), lambda i, j: (l, j, 0)),
        ] + ([vec] if final else []),
        out_specs=out_specs,
        out_shape=out_shape,
        scratch_shapes=[pltpu.VMEM((tm, d), BF16)],
        compiler_params=_cparams(("parallel", "arbitrary")),
        name="ffn_cast" if cast else "ffn",
    )(x, nw, wg, wu, wd, *((fnw,) if final else ()))
    return outs if cast else outs[0]


def _inproj_kernel(x_ref, nw_ref, w_ref, o_ref, h_ref):
    @pl.when(pl.program_id(1) == 0)
    def _():
        x = x_ref[...]
        ms = jnp.mean(x * x, axis=-1, keepdims=True)
        h_ref[...] = (x * lax.rsqrt(ms + NORM_EPS) * nw_ref[...]).astype(BF16)

    o_ref[...] = _dot(h_ref[...], w_ref[...])


def _inproj(x, nw, w, l, tm=1024, tn=1024):
    m, d = x.shape
    n = w.shape[2]
    return pl.pallas_call(
        _inproj_kernel,
        grid=(m // tm, n // tn),
        in_specs=[
            pl.BlockSpec((tm, d), lambda i, j: (i, 0)),
            pl.BlockSpec((1, d), lambda i, j: (0, 0)),
            pl.BlockSpec((None, d, tn), lambda i, j: (l, 0, j)),
        ],
        out_specs=pl.BlockSpec((tm, tn), lambda i, j: (i, j)),
        out_shape=jax.ShapeDtypeStruct((m, n), F32),
        scratch_shapes=[pltpu.VMEM((tm, d), BF16)],
        compiler_params=_cparams(("parallel", "arbitrary")),
        name="inproj",
    )(x, nw, w)


def _outproj_kernel(x_ref, a_ref, y_ref, bonus_ref, g_ref, c_ref, w_ref, lnw_ref, lnb_ref, e2_ref, o_ref):
    y = y_ref[...]
    e2 = e2_ref[...]
    mean = _seg_sum(y, e2) * (1.0 / RWKV_HEAD)
    yc = y - mean
    var = _seg_sum(yc * yc, e2) * (1.0 / RWKV_HEAD)
    b = (yc * lax.rsqrt(var + RWKV_GN_EPS) * lnw_ref[...] + lnb_ref[...] + bonus_ref[...]) * g_ref[...]
    o1 = S5_WIDTH
    o2 = S5_WIDTH + RWKV_WIDTH
    acc = _dot(a_ref[...].astype(BF16), w_ref[0:o1, :])
    acc += _dot(b.astype(BF16), w_ref[o1:o2, :])
    acc += _dot(c_ref[...].astype(BF16), w_ref[o2:, :])
    o_ref[...] = x_ref[...] + acc


def _outproj(x, ya, y_rw, bonus, g, yc, w_all, l, cs, tm=512):
    m, d = x.shape
    row = lambda i: (i, 0)
    whole = lambda i: (0, 0)
    wide = lambda a: pl.BlockSpec((tm, a.shape[1]), row)
    vec = pl.BlockSpec((1, RWKV_WIDTH), whole)
    return pl.pallas_call(
        _outproj_kernel,
        grid=(m // tm,),
        in_specs=[
            pl.BlockSpec((tm, d), row), wide(ya), wide(y_rw), wide(bonus), wide(g), wide(yc),
            pl.BlockSpec((None,) + w_all.shape[1:], lambda i: (l, 0, 0)),
            vec, vec, pl.BlockSpec((2 * SEG_CHUNK, SEG_CHUNK), whole),
        ],
        out_specs=pl.BlockSpec((tm, d), row),
        out_shape=jax.ShapeDtypeStruct((m, d), F32),
        compiler_params=_cparams(("parallel",)),
        name="outproj",
    )(x, ya, y_rw, bonus, g, yc, w_all, cs["ln_w"], cs["ln_b"], cs["e2w"])


def _s5_kernel(u_ref, x0r_ref, x0i_ref, b3r_ref, b3i_ref, cr_ref, ci_ref, cst_ref, d_ref, wglu_ref,
               bglu_ref, y_ref, xlr_ref, xli_ref, xr_ref, xi_ref, cr_s, ci_s, pm_ref, *, tr, chained):
    c = pl.program_id(1)
    nlt = S5_LANES // LANES
    per_gb = nlt // S5_GB

    def load_tile(ref, rows):
        return jnp.concatenate([ref[k, rows, :] for k in range(nlt)], axis=1)

    def store_tile(ref, rows, val):
        for k in range(nlt):
            ref[k, rows, :] = val[:, k * LANES:(k + 1) * LANES]

    u = u_ref[...]
    nut = S5_WIDTH // LANES
    seg = tr // SUBLANES
    if chained:
        for k in range(nut):
            pm_ref[k] = u[:, k * LANES:(k + 1) * LANES]
        u = jnp.concatenate(
            [jnp.concatenate([pm_ref[k, pl.ds(tau, SUBLANES, stride=seg), :] for tau in range(seg)], axis=0)
             for k in range(nut)], axis=1)
    uh, ul = _split2(u)
    for gb in range(S5_GB):
        sl = slice(gb * LANES, (gb + 1) * LANES)
        if chained:
            br = _dot(uh[:, sl], b3r_ref[gb, 0:LANES, :])
            bi = _dot(uh[:, sl], b3i_ref[gb, 0:LANES, :])
        else:
            lhs = jnp.concatenate([uh[:, sl], ul[:, sl], uh[:, sl]], axis=1)
            br = _dot(lhs, b3r_ref[gb])
            bi = _dot(lhs, b3i_ref[gb])
        for k in range(per_gb):
            xr_ref[gb * per_gb + k] = br[:, k * LANES:(k + 1) * LANES]
            xi_ref[gb * per_gb + k] = bi[:, k * LANES:(k + 1) * LANES]

    if chained:
        @pl.when(c == 0)
        def _():
            cr_s[...] = x0r_ref[0]
            ci_s[...] = x0i_ref[0]

    a1r, a1i, a2r, a2i, a4r, a4i, pr, pi = [cst_ref[i] for i in range(8)]

    def tile_scan(xr, xi, x0r, x0i, lv1, lv2, lv4, pw):
        for k, (ar, ai) in ((1, lv1), (2, lv2), (4, lv4)):
            sr = pltpu.roll(xr, k, 0)
            si = pltpu.roll(xi, k, 0)
            xr, xi = xr + ar * sr - ai * si, xi + ar * si + ai * sr
        x0r = jnp.broadcast_to(x0r, xr.shape)
        x0i = jnp.broadcast_to(x0i, xr.shape)
        return xr + pw[0] * x0r - pw[1] * x0i, xi + pw[0] * x0i + pw[1] * x0r

    if chained:
        seg = tr // SUBLANES
        b1r, b1i, b2r, b2i, b4r, b4i, qr64, qi64 = [cst_ref[8 + i] for i in range(8)]
        lt = lambda x, k: x[:, k * LANES:(k + 1) * LANES]
        ar = [jnp.broadcast_to(lt(pr, k)[0:1, :], (SUBLANES, LANES)) for k in range(nlt)]
        ai = [jnp.broadcast_to(lt(pi, k)[0:1, :], (SUBLANES, LANES)) for k in range(nlt)]

        def sweep1(tau, carry):
            rows = pl.ds(pl.multiple_of(tau * SUBLANES, SUBLANES), SUBLANES)
            out = []
            for k in range(nlt):
                xr, xi = carry[2 * k], carry[2 * k + 1]
                xr, xi = (ar[k] * xr - ai[k] * xi + xr_ref[k, rows, :], ar[k] * xi + ai[k] * xr + xi_ref[k, rows, :])
                xr_ref[k, rows, :] = xr
                xi_ref[k, rows, :] = xi
                out += [xr, xi]
            return tuple(out)

        zero = jnp.zeros((SUBLANES, LANES), F32)
        ends = lax.fori_loop(0, seg, sweep1, (zero,) * (2 * nlt))
        er = jnp.concatenate(ends[0::2], axis=1)
        ei = jnp.concatenate(ends[1::2], axis=1)
        x0r = cr_s[...]
        x0i = ci_s[...]
        yr, yi = tile_scan(er, ei, x0r, x0i, (b1r, b1i), (b2r, b2i), (b4r, b4i), (qr64, qi64))
        cr_s[...] = yr[SUBLANES - 1:SUBLANES, :]
        ci_s[...] = yi[SUBLANES - 1:SUBLANES, :]
        rid = lax.broadcasted_iota(jnp.int32, pr.shape, 0)
        sr0 = jnp.where(rid == 0, jnp.broadcast_to(x0r, pr.shape), pltpu.roll(yr, 1, 0))
        si0 = jnp.where(rid == 0, jnp.broadcast_to(x0i, pr.shape), pltpu.roll(yi, 1, 0))

        def sweep2(tau, carry):
            rows = pl.ds(pl.multiple_of(tau * SUBLANES, SUBLANES), SUBLANES)
            out = []
            for k in range(nlt):
                qr, qi = carry[2 * k], carry[2 * k + 1]
                qr, qi = ar[k] * qr - ai[k] * qi, ar[k] * qi + ai[k] * qr
                xr_ref[k, rows, :] = xr_ref[k, rows, :] + qr
                xi_ref[k, rows, :] = xi_ref[k, rows, :] + qi
                out += [qr, qi]
            return tuple(out)

        start = []
        for k in range(nlt):
            start += [lt(sr0, k), lt(si0, k)]
        lax.fori_loop(0, seg, sweep2, tuple(start))

        @pl.when(c == pl.num_programs(1) - 1)
        def _():
            xlr_ref[0] = cr_s[...]
            xli_ref[0] = ci_s[...]
    else:
        def tile(j, carry):
            rows = pl.ds(pl.multiple_of(j * SUBLANES, SUBLANES), SUBLANES)
            xr, xi = tile_scan(load_tile(xr_ref, rows), load_tile(xi_ref, rows),
                               x0r_ref[pl.ds(j, 1), :], x0i_ref[pl.ds(j, 1), :],
                               (a1r, a1i), (a2r, a2i), (a4r, a4i), (pr, pi))
            store_tile(xr_ref, rows, xr)
            store_tile(xi_ref, rows, xi)
            xlr_ref[pl.ds(j, 1), :] = xr[SUBLANES - 1:SUBLANES, :]
            xli_ref[pl.ds(j, 1), :] = xi[SUBLANES - 1:SUBLANES, :]
            return carry

        lax.fori_loop(0, tr // SUBLANES, tile, 0)

    ys = []
    for gb in range(S5_GB):
        xr = jnp.concatenate([xr_ref[gb * per_gb + k] for k in range(per_gb)], axis=1)
        xi = jnp.concatenate([xi_ref[gb * per_gb + k] for k in range(per_gb)], axis=1)
        ys.append(_dot(xr.astype(BF16), cr_ref[gb]) + _dot(xi.astype(BF16), ci_ref[gb]))
    y = jnp.concatenate(ys, axis=1) + d_ref[...] * u
    y = 0.5 * y * (1.0 + jnp.tanh(math.sqrt(2.0 / math.pi) * (y + 0.044715 * (y * y * y))))
    z = _dot(y.astype(BF16), wglu_ref[...]) + bglu_ref[...]
    y = y * _sigmoid(z)
    if chained:
        for k in range(nut):
            pm_ref[k] = y[:, k * LANES:(k + 1) * LANES]
        for m in range(tr // SUBLANES):
            start = (m % SUBLANES) * SUBLANES * SUBLANES + m // SUBLANES
            for k in range(nut):
                y_ref[m * SUBLANES:(m + 1) * SUBLANES, k * LANES:(k + 1) * LANES] = (
                    pm_ref[k, pl.ds(start, SUBLANES, stride=SUBLANES), :])
    else:
        y_ref[...] = y


def _s5(p3, x0r, x0i, l, cs, nseq, t, tr=S5_TR):
    rows = nseq * t
    depth = x0r.shape[0]
    chained = t > SUBLANES
    if chained:
        grid = (nseq, t // tr)
        row_map = lambda n, c: (n * (t // tr) + c, 0)
        out_map = lambda n, c: (n * (t // tr) + c, 0)
        x0r = x0r.reshape(depth, nseq, 1, S5_LANES)
        x0i = x0i.reshape(depth, nseq, 1, S5_LANES)
        in_st = pl.BlockSpec((None, 1, 1, S5_LANES), lambda n, c: (l, n, 0, 0))
        st_spec = pl.BlockSpec((1, 1, S5_LANES), lambda n, c: (n, 0, 0))
        st_shape = jax.ShapeDtypeStruct((nseq, 1, S5_LANES), F32)
    else:
        assert t == SUBLANES
        grid = (rows // tr, 1)
        row_map = lambda n, c: (n, 0)
        out_map = lambda n, c: (n, 0)
        in_st = pl.BlockSpec((None, tr // SUBLANES, S5_LANES), lambda n, c: (l, n, 0))
        st_spec = pl.BlockSpec((tr // SUBLANES, S5_LANES), lambda n, c: (n, 0))
        st_shape = jax.ShapeDtypeStruct((nseq, S5_LANES), F32)
    whole3 = lambda n, c: (0, 0, 0)
    whole2 = lambda n, c: (0, 0)
    y, xlr, xli = pl.pallas_call(
        functools.partial(_s5_kernel, tr=tr, chained=chained),
        grid=grid,
        in_specs=[
            pl.BlockSpec((tr, S5_WIDTH), row_map),
            in_st, in_st,
            pl.BlockSpec(cs["b3r"].shape, whole3),
            pl.BlockSpec(cs["b3i"].shape, whole3),
            pl.BlockSpec(cs["cr"].shape, whole3),
            pl.BlockSpec(cs["ci"].shape, whole3),
            pl.BlockSpec(cs["scan"].shape, whole3),
            pl.BlockSpec((1, S5_WIDTH), whole2),
            pl.BlockSpec((S5_WIDTH, S5_WIDTH), whole2),
            pl.BlockSpec((1, S5_WIDTH), whole2),
        ],
        out_specs=[pl.BlockSpec((tr, S5_WIDTH), out_map), st_spec, st_spec],
        out_shape=[jax.ShapeDtypeStruct((rows, S5_WIDTH), F32), st_shape, st_shape],
        scratch_shapes=[pltpu.VMEM((S5_LANES // LANES, tr, LANES), F32), pltpu.VMEM((S5_LANES // LANES, tr, LANES), F32),
                        pltpu.VMEM((1, S5_LANES), F32), pltpu.VMEM((1, S5_LANES), F32),
                        pltpu.VMEM((S5_WIDTH // LANES, tr, LANES), F32)],
        compiler_params=_cparams(("parallel", "arbitrary")),
        name="s5_chain" if chained else "s5_tile",
    )(p3, x0r, x0i, cs["b3r"], cs["b3i"], cs["cr"], cs["ci"], cs["scan"], cs["d"], cs["wglu"], cs["bglu"])
    return y, xlr.reshape(nseq, S5_GROUPS, S5_STATE), xli.reshape(nseq, S5_GROUPS, S5_STATE)


def _s5_consts(a_re, a_im, log_dt, b_re, b_im, c_re, c_im, d, w_glu, b_glu):
    g, p, h = S5_GROUPS, S5_STATE, S5_GROUP
    dt = jnp.exp(log_dt)[:, None]
    mag = jnp.exp(a_re * dt)
    abr = mag * jnp.cos(a_im * dt)
    abi = mag * jnp.sin(a_im * dt)
    den = a_re * a_re + a_im * a_im
    qr = ((abr - 1.0) * a_re + abi * a_im) / den
    qi = (abi * a_re - (abr - 1.0) * a_im) / den
    bbr = qr[..., None] * b_re - qi[..., None] * b_im
    bbi = qr[..., None] * b_im + qi[..., None] * b_re
    row = jnp.arange(SUBLANES)[:, None]

    def scan_tables(mr, mi):
        pr, pi = [mr], [mi]
        for _ in range(SUBLANES - 1):
            pr, pi = pr + [pr[-1] * mr - pi[-1] * mi], pi + [pr[-1] * mi + pi[-1] * mr]
        pr = jnp.stack(pr).reshape(SUBLANES, g * p)
        pi = jnp.stack(pi).reshape(SUBLANES, g * p)

        def lvl(k):
            return [jnp.where(row >= k, pr[k - 1][None, :], 0.0), jnp.where(row >= k, pi[k - 1][None, :], 0.0)]

        return lvl(1) + lvl(2) + lvl(4) + [pr, pi]

    sr, si = abr, abi
    for _ in range(S5_SEG.bit_length() - 1):
        sr, si = sr * sr - si * si, 2.0 * sr * si
    scan = jnp.stack(scan_tables(abr, abi) + scan_tables(sr, si)).astype(F32)
    eye = jnp.eye(SUBLANES, dtype=F32)

    def in_blocks(x):
        xx = x.reshape(S5_GB, SUBLANES, p, h).transpose(0, 1, 3, 2)
        blk = (xx[:, :, :, None, :] * eye[None, :, None, :, None]).reshape(S5_GB, SUBLANES * h, SUBLANES * p)
        hi = blk.astype(BF16)
        lo = (blk - hi.astype(F32)).astype(BF16)
        return jnp.concatenate([hi, hi, lo], axis=1)

    def out_blocks(x):
        xx = x.reshape(S5_GB, SUBLANES, h, p).transpose(0, 1, 3, 2)
        return (xx[:, :, :, None, :] * eye[None, :, None, :, None]).reshape(
            S5_GB, SUBLANES * p, SUBLANES * h).astype(BF16)

    return dict(b3r=in_blocks(bbr), b3i=in_blocks(bbi),
                cr=out_blocks(c_re), ci=out_blocks(-c_im), scan=scan,
                d=d.reshape(1, -1), wglu=w_glu.astype(BF16), bglu=b_glu.reshape(1, -1))


def _seg_sum(x, e2):
    hi, lo = _split2(x)
    cw = e2.shape[1]
    return jnp.concatenate(
        [_dot(jnp.concatenate([hi[:, c:c + cw], lo[:, c:c + cw]], axis=1), e2) for c in range(0, x.shape[1], cw)],
        axis=1)


def _rwkv_prep_kernel(*refs, nlev, chained, tiles_per_seq):
    nz = RWKV_ZBLOCKS
    z = jnp.concatenate([r[...] for r in refs[:nz]], axis=1)
    if chained:
        prev8 = jnp.concatenate([r[...] for r in refs[nz:2 * nz]], axis=1)
        sh_ref = refs[2 * nz]
        refs = refs[2 * nz + 1:]
    else:
        zp_ref = refs[nz]
        refs = refs[nz + 1:]
    (mu_ref, w0_ref, a0_ref, w2_ref, a2_ref, g2_ref, kk_ref, ka_ref, rk_ref, e2_ref, tril_ref, mk_ref, hm_ref,
     ap_o, rp_o, bt_o, kp_o, cc_o, y0_o, v_o, gm_o, g_o, bonus_o) = refs
    tm = RWKV_TM
    if chained:
        first = (pl.program_id(0) % tiles_per_seq) == 0
        prev_row = jnp.where(first, sh_ref[0], prev8[SUBLANES - 1:SUBLANES, :])
        rid = lax.broadcasted_iota(jnp.int32, z.shape, 0)
        zp = jnp.where(rid == 0, prev_row, pltpu.roll(z, 1, 0))
    else:
        zp = zp_ref[...]
    zm = z + (zp - z) * mu_ref[...]
    w = RWKV_WIDTH
    r = zm[:, 0:w]
    k = zm[:, w:2 * w]
    v = zm[:, 2 * w:3 * w]
    wa = zm[:, 3 * w:3 * w + LANES]
    gi = zm[:, 3 * w + LANES:3 * w + 2 * LANES]
    e2 = e2_ref[...]
    wl = w0_ref[...] + _dot(jnp.tanh(wa).astype(BF16), w2_ref[...])
    sp = jnp.maximum(-wl, 0.0) + jnp.log(1.0 + jnp.exp(-jnp.abs(wl)))
    ld = -jnp.exp(-sp - 0.5)
    a = _sigmoid(a0_ref[...] + _dot(wa.astype(BF16), a2_ref[...]))
    g_o[...] = _dot(_sigmoid(gi).astype(BF16), g2_ref[...])
    kk = k * kk_ref[...]
    kk = kk * lax.rsqrt(jnp.maximum(_seg_sum(kk * kk, e2), 1e-24))
    k2 = k * (1.0 + (a - 1.0) * ka_ref[...])
    b = kk * a
    bonus_o[...] = _seg_sum(r * k2 * rk_ref[...], e2) * v
    v_o[...] = v

    p1, p2, p3 = _split3(ld)
    cum = _dot(tril_ref[...], jnp.concatenate([p1, p2, p3], axis=0))
    gam = jnp.exp(cum)
    ginv = jnp.exp(-cum)
    gm_o[...] = gam
    alpha = kk * jnp.exp(cum - ld)
    beta = b * ginv
    kap = k2 * ginv
    rho = r * gam
    bt_o[...] = beta
    kp_o[...] = kap

    strict = mk_ref[0]
    incl = mk_ref[1]
    eye = mk_ref[2]
    heads = [(j, e) for j in range(RWKV_PAIRS) for e in range(2)]
    nh = len(heads)

    def lanes(j):
        return slice(j * LANES, (j + 1) * LANES)

    al = [alpha[:, lanes(j)] * hm_ref[e] for j, e in heads]
    rh = [rho[:, lanes(j)] * hm_ref[e] for j, e in heads]
    vh = [(v[:, lanes(j)] * hm_ref[e]).astype(BF16) for j, e in heads]
    yk = [jnp.concatenate([beta[:, lanes(j)], kap[:, lanes(j)]], axis=0).astype(BF16) for j in range(RWKV_PAIRS)]
    wu = [_dot_nt(jnp.concatenate(_split2(al[i]), axis=1), jnp.concatenate([yk[j], yk[j]], axis=1))
          for i, (j, e) in enumerate(heads)]
    wl = [_dot_nt(rh[i].astype(BF16), yk[j]) for i, (j, e) in enumerate(heads)]
    mm = [wu[i][:, :tm] * strict for i in range(nh)]
    nn = [wu[i][:, tm:] * strict for i in range(nh)]
    pp = [(wl[i][:, :tm] * incl).astype(BF16) for i in range(nh)]
    qq = [(wl[i][:, tm:] * incl).astype(BF16) for i in range(nh)]
    tinv = [eye - mm[i] * mk_ref[3] for i in range(nh)]
    for lv in range(1, nlev):
        th = [tinv[i].astype(BF16) for i in range(nh)]
        tmp = [_dot(jnp.concatenate([th[i], th[i]], axis=1),
                    jnp.concatenate(_split2(mm[i] * mk_ref[3 + lv]), axis=0)) for i in range(nh)]
        tinv = [tinv[i] - _dot(jnp.concatenate(_split2(tmp[i]), axis=1),
                               jnp.concatenate([th[i], th[i]], axis=0)) for i in range(nh)]
    th = [tinv[i].astype(BF16) for i in range(nh)]
    nv = [_dot(jnp.concatenate(_split2(nn[i]), axis=1), jnp.concatenate([vh[i], vh[i]], axis=0)) for i in range(nh)]
    qv = [_dot(qq[i], vh[i]) for i in range(nh)]
    tac = [_dot(jnp.concatenate([th[i], th[i]], axis=1),
                jnp.concatenate(_split2(jnp.concatenate([al[i], nv[i]], axis=1)), axis=0)) for i in range(nh)]
    pac = [_dot(pp[i], tac[i].astype(BF16)) for i in range(nh)]
    ap = [tac[i][:, :LANES] for i in range(nh)]
    cc = [tac[i][:, LANES:] for i in range(nh)]
    rp = [rh[i] - pac[i][:, :LANES] for i in range(nh)]
    y0 = [qv[i] - pac[i][:, LANES:] for i in range(nh)]
    for j in range(RWKV_PAIRS):
        ap_o[:, lanes(j)] = ap[2 * j] + ap[2 * j + 1]
        rp_o[:, lanes(j)] = rp[2 * j] + rp[2 * j + 1]
        cc_o[:, lanes(j)] = cc[2 * j] + cc[2 * j + 1]
        y0_o[:, lanes(j)] = y0[2 * j] + y0[2 * j + 1]


def _rwkv_tables(tm, blk):
    t = np.arange(tm)
    same = (t[:, None] // blk) == (t[None, :] // blk)
    tril = (same & (t[None, :] <= t[:, None])).astype(np.float32)
    masks = [same & (t[None, :] < t[:, None]), same & (t[None, :] <= t[:, None]), np.eye(tm, dtype=bool)]
    n = 2
    while n <= blk:
        m = n // 2
        masks.append(((t[:, None] // n) == (t[None, :] // n)) & ((t[:, None] % n) >= m) & ((t[None, :] % n) < m))
        n *= 2
    nlev = len(masks) - 3
    hm = np.stack([(np.arange(LANES) < RWKV_HEAD), (np.arange(LANES) >= RWKV_HEAD)]).astype(np.float32)
    return (jnp.asarray(np.concatenate([tril, tril, tril], axis=1), BF16),
            jnp.asarray(np.stack(masks).astype(np.float32)), jnp.asarray(hm.reshape(2, 1, LANES)), nlev)


def _rwkv_prep(p3, prev, cs, nseq, t):
    m = nseq * t
    tm = RWKV_TM
    blk = min(RWKV_BLOCK, t)
    chained = t >= tm
    tril3, masks, hm, nlev = _rwkv_tables(tm, blk)
    w = RWKV_WIDTH
    row = lambda i: (i, 0)
    whole = lambda i: (0, 0)
    whole3 = lambda i: (0, 0, 0)
    vec = pl.BlockSpec((1, w), whole)
    zspecs = [pl.BlockSpec((tm, S5_WIDTH), lambda i, c=c: (i, 1 + c)) for c in range(RWKV_ZBLOCKS)]
    if chained:
        tps = t // tm
        lead = zspecs + [pl.BlockSpec((SUBLANES, S5_WIDTH),
                                      lambda i, c=c: (jnp.maximum(i * (tm // SUBLANES) - 1, 0), 1 + c))
                         for c in range(RWKV_ZBLOCKS)]
        lead.append(pl.BlockSpec((1, 1, RWKV_SHIFT_WIDTH), lambda i: (i // tps, 0, 0)))
        args = (p3,) * (2 * RWKV_ZBLOCKS) + (prev,)
    else:
        tps = 1
        lead = zspecs + [pl.BlockSpec((tm, RWKV_SHIFT_WIDTH), row)]
        args = (p3,) * RWKV_ZBLOCKS + (prev,)
    return pl.pallas_call(
        functools.partial(_rwkv_prep_kernel, nlev=nlev, chained=chained, tiles_per_seq=tps),
        grid=(m // tm,),
        in_specs=lead + [
            pl.BlockSpec((1, RWKV_SHIFT_WIDTH), whole),
            vec, vec,
            pl.BlockSpec((LANES, w), whole), pl.BlockSpec((LANES, w), whole), pl.BlockSpec((LANES, w), whole),
            vec, vec, vec,
            pl.BlockSpec((2 * SEG_CHUNK, SEG_CHUNK), whole),
            pl.BlockSpec(tril3.shape, whole), pl.BlockSpec(masks.shape, whole3), pl.BlockSpec(hm.shape, whole3),
        ],
        out_specs=[pl.BlockSpec((tm, w), row)] * 10,
        out_shape=[jax.ShapeDtypeStruct((m, w), F32)] * 10,
        compiler_params=_cparams(("parallel",)),
        name="rwkv_prep",
    )(*args, cs["mu"], cs["w0"], cs["a0"], cs["w2"], cs["a2"], cs["g2"], cs["k_k"], cs["k_a"], cs["r_k"],
      cs["e2w"], tril3, masks, hm)


def _rwkv_rec_kernel(*refs, nbp, tc, blk, aliased):
    ap_ref, rp_ref, bt_ref, kp_ref, cc_ref, y0_ref, v_ref, gm_ref, s0_ref, bd_ref = refs[:10]
    y_ref, so_ref, s_ref = refs[10 + (1 if aliased else 0):]
    c = pl.program_id(1)
    hd = RWKV_HEAD
    pairs = [(n, j) for n in range(nbp) for j in range(RWKV_PAIRS)]

    @pl.when(c == 0)
    def _():
        zero = jnp.zeros((hd, hd), F32)
        for idx, (n, j) in enumerate(pairs):
            top = jnp.concatenate([s0_ref[n, 2 * j], zero], axis=1)
            bot = jnp.concatenate([zero, s0_ref[n, 2 * j + 1]], axis=1)
            s_ref[idx] = jnp.concatenate([top, bot], axis=0)

    bd = bd_ref[...]

    def block(bi, carry):
        rows = pl.ds(pl.multiple_of(bi * blk, blk), blk)
        gmats = []
        for idx, (n, j) in enumerate(pairs):
            ls = slice(j * LANES, (j + 1) * LANES)
            sh, sl = _split2(s_ref[idx])
            xh = jnp.concatenate([ap_ref[n, rows, ls], rp_ref[n, rows, ls]], axis=0).astype(BF16)
            gmats.append(_dot_nt(jnp.concatenate([xh, xh], axis=1), jnp.concatenate([sh, sl], axis=1)))
        upds = []
        for idx, (n, j) in enumerate(pairs):
            ls = slice(j * LANES, (j + 1) * LANES)
            gmat = gmats[idx]
            e = -gmat[:blk] - cc_ref[n, rows, ls]
            y_ref[n, rows, ls] = gmat[blk:] + y0_ref[n, rows, ls]
            evh, evl = _split2(jnp.concatenate([e, v_ref[n, rows, ls]], axis=0))
            bkh, bkl = _split2(jnp.concatenate([bt_ref[n, rows, ls], kp_ref[n, rows, ls]], axis=0))
            upds.append(_dot_tn(jnp.concatenate([evh, evh, evl], axis=0), jnp.concatenate([bkh, bkl, bkh], axis=0)))
        for idx, (n, j) in enumerate(pairs):
            ls = slice(j * LANES, (j + 1) * LANES)
            gl = gm_ref[n, rows, ls][blk - 1:blk, :]
            s_ref[idx] = (s_ref[idx] + upds[idx] * bd) * gl
        return carry

    lax.fori_loop(0, tc // blk, block, 0)

    @pl.when(c == pl.num_programs(1) - 1)
    def _():
        for idx, (n, j) in enumerate(pairs):
            s = s_ref[idx]
            so_ref[n, 2 * j] = s[:hd, :hd]
            so_ref[n, 2 * j + 1] = s[hd:, hd:]


def _rwkv_rec(arrs, s0_all, l, so_prev, cs, nseq, t, nbp=4, tc=128):
    w = RWKV_WIDTH
    tc = min(tc, t)
    blk = min(RWKV_BLOCK, t)
    depth = s0_all.shape[0]
    seq_map = lambda i, c: (i, c, 0)
    st_map = lambda i, c: (l, i, 0, 0, 0)
    blkspec = pl.BlockSpec((nbp, tc, w), seq_map)
    st = pl.BlockSpec((None, nbp, RWKV_HEADS, RWKV_HEAD, RWKV_HEAD), st_map)
    aliased = so_prev is not None
    in_specs = [blkspec] * 8 + [st, pl.BlockSpec((LANES, LANES), lambda i, c: (0, 0))]
    args = list(arrs) + [s0_all, cs["bd"]]
    aliases = {}
    if aliased:
        in_specs.append(pl.BlockSpec(memory_space=pl.ANY))
        args.append(so_prev)
        aliases = {10: 1}
    return pl.pallas_call(
        functools.partial(_rwkv_rec_kernel, nbp=nbp, tc=tc, blk=blk, aliased=aliased),
        grid=(nseq // nbp, t // tc),
        in_specs=in_specs,
        out_specs=[blkspec, st],
        out_shape=[jax.ShapeDtypeStruct((nseq, t, w), F32),
                   jax.ShapeDtypeStruct((depth, nseq, RWKV_HEADS, RWKV_HEAD, RWKV_HEAD), F32)],
        scratch_shapes=[pltpu.VMEM((nbp * RWKV_PAIRS, LANES, LANES), F32)],
        input_output_aliases=aliases,
        compiler_params=_cparams(("parallel", "arbitrary")),
        name="rwkv_rec",
    )(*args)


def _rwkv_consts(mu, w0, w2, a0, a2, g2, k_k, k_a, r_k, ln_w, ln_b):
    w = RWKV_WIDTH
    seg = np.arange(SEG_CHUNK) // RWKV_HEAD
    e_w = (seg[:, None] == seg[None, :]).astype(np.float32)
    seg = np.arange(LANES) // RWKV_HEAD
    bd = (seg[:, None] == seg[None, :]).astype(np.float32)
    z64 = jnp.zeros((RWKV_HEAD, w), F32)
    return dict(
        mu=mu.reshape(1, -1), w0=w0.reshape(1, -1), a0=a0.reshape(1, -1),
        w2=jnp.concatenate([w2, z64], axis=0).astype(BF16),
        a2=jnp.concatenate([z64, a2], axis=0).astype(BF16),
        g2=g2.astype(BF16), k_k=k_k.reshape(1, -1), k_a=k_a.reshape(1, -1), r_k=r_k.reshape(1, -1),
        ln_w=ln_w.reshape(1, -1), ln_b=ln_b.reshape(1, -1),
        e2w=jnp.asarray(np.concatenate([e_w, e_w], axis=0), BF16),
        bd=jnp.asarray(bd, F32))


def _rwkv(p3, shift_l, wkv_all, l, so_prev, cs, nseq, t):
    if t >= RWKV_TM:
        prev = shift_l.reshape(nseq, 1, RWKV_SHIFT_WIDTH)
    else:
        z3 = p3.reshape(nseq, t, N_IN)[:, :, S5_WIDTH:S5_WIDTH + RWKV_SHIFT_WIDTH]
        prev = jnp.concatenate([shift_l[:, None, :], z3[:, :-1]], axis=1).reshape(nseq * t, RWKV_SHIFT_WIDTH)
    outs = _rwkv_prep(p3, prev, cs, nseq, t)
    arrs = [a.reshape(nseq, t, RWKV_WIDTH) for a in outs[:8]]
    y, so = _rwkv_rec(arrs, wkv_all, l, so_prev, cs, nseq, t)
    return (y.reshape(nseq * t, RWKV_WIDTH), outs[9], outs[8]), so


def _hgrn_tables(c):
    t = np.arange(c)
    j = np.arange(c)
    mats = [(j[None, :] <= t[:, None]), (j[None, :] > t[:, None])]
    masks = []
    n = 2
    while n <= c:
        m = n // 2
        bs = (t // n) * n
        hi_half = (t % n) >= m
        mats.append((hi_half[:, None] & (j[None, :] >= (bs + m)[:, None]) & (j[None, :] <= t[:, None]))
                    | ((~hi_half)[:, None] & (j[None, :] > t[:, None]) & (j[None, :] <= (bs + m - 1)[:, None])))
        masks.append(((t[:, None] // n) == (t[None, :] // n)) & hi_half[:, None] & (~hi_half)[None, :])
        n *= 2
    w = np.concatenate(mats, axis=0).astype(np.float32)
    w3 = np.concatenate([w, w, w], axis=1)
    return jnp.asarray(w3, BF16), jnp.asarray(np.stack(masks).astype(np.float32)), len(masks)


def _hgrn_kernel(*refs, rb, c, nlev, chained, aliased, sub):
    q_ref, f_ref, i_ref, g_ref, s0_ref, lb_ref, nw_ref, w3_ref, mk_ref = refs[:9]
    o_ref, so_ref, st_ref = refs[9 + (1 if aliased else 0):]
    cidx = pl.program_id(1)
    hdim = HGRN_DIM

    if chained:
        @pl.when(cidx == 0)
        def _():
            for h in range(HGRN_HEADS):
                st_ref[h] = s0_ref[0, h].T

    w3 = w3_ref[...]
    row_i = lax.broadcasted_iota(jnp.int32, (c, c), 0)
    col_i = lax.broadcasted_iota(jnp.int32, (c, c), 1)
    eye = (row_i == col_i).astype(F32)

    heads = range(HGRN_HEADS)

    def lanes(h):
        return slice(h * hdim, (h + 1) * hdim)

    def block(it, carry):
        rows_s = [pl.ds(pl.multiple_of((it * sub + s) * c, c), c) for s in range(sub)]
        items = [(s, h) for s in range(sub) for h in heads]
        fg = {k: lb_ref[:, lanes(k[1])] + (1.0 - lb_ref[:, lanes(k[1])]) * _sigmoid(f_ref[rows_s[k[0]], lanes(k[1])])
              for k in items}
        kk = {k: 1.0 - fg[k] for k in items}
        qs = {}
        for k in items:
            q = q_ref[rows_s[k[0]], lanes(k[1])]
            qs[k] = q * _sigmoid(q)
        ex = {k: jnp.exp(_dot(w3, jnp.concatenate(_split3(jnp.log(fg[k])), axis=0))) for k in items}
        vb = {k: i_ref[rows_s[k[0]], lanes(k[1])].astype(BF16) for k in items}
        lev = {(k, lv): _dot_nt((qs[k] * ex[k][(2 + lv) * c:(3 + lv) * c]).astype(BF16),
                                (kk[k] * ex[k][(2 + lv) * c:(3 + lv) * c]).astype(BF16))
               for k in items for lv in range(nlev)}
        att = {}
        for k in items:
            a = eye * jnp.sum(qs[k] * kk[k], axis=-1, keepdims=True)
            for lv in range(nlev):
                a = a + jnp.where(mk_ref[lv] > 0.0, lev[(k, lv)], 0.0)
            att[k] = a.astype(BF16)
        intra = {k: _dot(att[k], vb[k]) for k in items}
        kv = {k: _dot_tn(vb[k], (kk[k] * ex[k][c:2 * c]).astype(BF16)) for k in items}
        qe = {k: (qs[k] * ex[k][0:c]).astype(BF16) for k in items}
        st = {h: st_ref[h] for h in heads} if chained else None
        for s in range(sub):
            if not chained:
                st = {h: s0_ref[it * sub + s, h].T for h in heads}
            inter = {h: _dot_nt(qe[(s, h)], st[h].astype(BF16)) for h in heads}
            st = {h: st[h] * ex[(s, h)][c - 1:c, :] + kv[(s, h)] for h in heads}
            if not chained:
                for h in heads:
                    so_ref[it * sub + s, h] = st[h].T
            for h in heads:
                o = inter[h] + intra[(s, h)]
                o = o * lax.rsqrt(jnp.mean(o * o, axis=-1, keepdims=True) + HGRN_EPS)
                gg = g_ref[rows_s[s], lanes(h)]
                o_ref[rows_s[s], lanes(h)] = o * nw_ref[:, lanes(h)] * (gg * _sigmoid(gg))
        if chained:
            for h in heads:
                st_ref[h] = st[h]
        return carry

    lax.fori_loop(0, rb // (c * sub), block, 0)

    if chained:
        @pl.when(cidx == pl.num_programs(1) - 1)
        def _():
            for h in range(HGRN_HEADS):
                so_ref[0, h] = st_ref[h].T


def _hgrn(p3, s0_all, l, so_prev, lb, nw, nseq, t):
    rows = nseq * t
    depth = s0_all.shape[0]
    c = math.gcd(t, 64)
    chained = t > c
    w3, masks, nlev = _hgrn_tables(c)
    w = HGRN_WIDTH
    if chained:
        rb = 256
        nblk = t // rb
        grid = (nseq, nblk)
        col = lambda cb: (lambda n, k: (n * nblk + k, cb))
        out_map = lambda n, k: (n * nblk + k, 0)
        nsb = 1
    else:
        nsb = 16
        rb = nsb * c
        grid = (rows // rb, 1)
        col = lambda cb: (lambda n, k: (n, cb))
        out_map = lambda n, k: (n, 0)
    st_spec = pl.BlockSpec((None, nsb, HGRN_HEADS, HGRN_DIM, HGRN_DIM), lambda n, k: (l, n, 0, 0, 0))
    whole2 = lambda n, k: (0, 0)
    whole3 = lambda n, k: (0, 0, 0)
    aliased = so_prev is not None
    in_specs = [pl.BlockSpec((rb, w), col(4)), pl.BlockSpec((rb, w), col(5)), pl.BlockSpec((rb, w), col(6)),
                pl.BlockSpec((rb, w), col(7)), st_spec,
                pl.BlockSpec((1, w), whole2), pl.BlockSpec((1, w), whole2),
                pl.BlockSpec(w3.shape, whole2), pl.BlockSpec(masks.shape, whole3)]
    args = [p3, p3, p3, p3, s0_all, lb, nw, w3, masks]
    aliases = {}
    if aliased:
        in_specs.append(pl.BlockSpec(memory_space=pl.ANY))
        args.append(so_prev)
        aliases = {9: 1}
    return pl.pallas_call(
        functools.partial(_hgrn_kernel, rb=rb, c=c, nlev=nlev, chained=chained, aliased=aliased,
                          sub=2 if chained else 4),
        grid=grid,
        in_specs=in_specs,
        out_specs=[pl.BlockSpec((rb, w), out_map), st_spec],
        out_shape=[jax.ShapeDtypeStruct((rows, w), F32),
                   jax.ShapeDtypeStruct((depth, nseq, HGRN_HEADS, HGRN_DIM, HGRN_DIM), F32)],
        scratch_shapes=[pltpu.VMEM((HGRN_HEADS, HGRN_DIM, HGRN_DIM), F32)],
        input_output_aliases=aliases,
        compiler_params=_cparams(("parallel", "arbitrary")),
        name="hgrn_chain" if chained else "hgrn_tile",
    )(*args)


def _run(x, nseq, t, states, lw, ffw, norm_final):
    s5_re0, s5_im0, shift0, wkv0, hgrn0 = states
    depth = len(lw)
    s5_re0 = s5_re0.reshape(depth, nseq, S5_LANES)
    s5_im0 = s5_im0.reshape(depth, nseq, S5_LANES)
    s5_re_l, s5_im_l, shift_l = [], [], []
    wkv_out = None
    hgrn_out = None
    cast = not isinstance(ffw["f1g"], list)
    ffb = {k: [] for k in ffw}

    def ffn(x, nw, names, l, fnw=None):
        if not cast:
            return _ffn(x, nw, *(ffw[k][l][None] for k in names), 0, fnw=fnw)
        res = _ffn(x, nw, *(ffw[k] for k in names), l, cast=True, fnw=fnw)
        for k, wb in zip(names, res[1:]):
            ffb[k].append(wb)
        return res[0]

    for l in range(depth):
        w = lw[l]
        x = ffn(x, w["norm_ffn1"], ("f1g", "f1u", "f1d"), l)
        p3 = _inproj(x, w["norm_mix"], w["w_in"], l)
        y_s5, s_re, s_im = _s5(p3, s5_re0, s5_im0, l, w["s5"], nseq, t)
        (y_rw, bonus, gate), wkv_out = _rwkv(p3, shift0[l], wkv0, l, wkv_out, w["rwkv"], nseq, t)
        y_hg, hgrn_out = _hgrn(p3, hgrn0, l, hgrn_out, w["hgrn_lb"], w["hgrn_nw"], nseq, t)
        x = _outproj(x, y_s5, y_rw, bonus, gate, y_hg, w["w_out"], l, w["rwkv"])
        x = ffn(x, w["norm_ffn2"], ("f2g", "f2u", "f2d"), l,
                fnw=norm_final.reshape(1, -1) if l == depth - 1 else None)
        s5_re_l.append(s_re)
        s5_im_l.append(s_im)
        shift_l.append(p3.reshape(nseq, t, N_IN)[:, t - 1, S5_WIDTH:S5_WIDTH + RWKV_SHIFT_WIDTH])
    return (x,jnp.stack(s5_re_l), jnp.stack(s5_im_l), jnp.stack(shift_l), wkv_out, hgrn_out), ffb


def kernel(x_prompt, x_sample, state_s5_re, state_s5_im, state_rwkv_shift, state_rwkv_wkv, state_hgrn, norm_ffn1, ffn1_w_gate, ffn1_w_up, ffn1_w_down, norm_mix, w_in, s5_a_re, s5_a_im, s5_log_dt, s5_b_re, s5_b_im, s5_c_re, s5_c_im, s5_d, s5_w_glu, s5_b_glu, rwkv_mu, rwkv_w0, rwkv_w2, rwkv_a0, rwkv_a2, rwkv_g2, rwkv_k_k, rwkv_k_a, rwkv_r_k, rwkv_ln_w, rwkv_ln_b, hgrn_lb_raw, hgrn_norm_w, w_out, norm_ffn2, ffn2_w_gate, ffn2_w_up, ffn2_w_down, norm_final):
    depth = w_in.shape[0]
    nb, seq, d = x_prompt.shape
    ns, dseq, _ = x_sample.shape

    p_lb = jax.nn.softmax(hgrn_lb_raw.astype(F32), axis=0)
    lower_bounds = jnp.cumsum(p_lb, axis=0) - p_lb[0]

    w_in_b = w_in.astype(BF16)
    w_out_b = w_out.astype(BF16)
    lw = []
    for l in range(depth):
        lw.append(dict(
            w_in=w_in_b, w_out=w_out_b,
            norm_ffn1=norm_ffn1[l].reshape(1, -1), norm_mix=norm_mix[l].reshape(1, -1),
            norm_ffn2=norm_ffn2[l].reshape(1, -1),
            s5=_s5_consts(s5_a_re[l], s5_a_im[l], s5_log_dt[l], s5_b_re[l], s5_b_im[l], s5_c_re[l], s5_c_im[l],
                          s5_d[l], s5_w_glu[l], s5_b_glu[l]),
            rwkv=_rwkv_consts(rwkv_mu[l], rwkv_w0[l], rwkv_w2[l], rwkv_a0[l], rwkv_a2[l], rwkv_g2[l],
                              rwkv_k_k[l], rwkv_k_a[l], rwkv_r_k[l], rwkv_ln_w[l], rwkv_ln_b[l]),
            hgrn_lb=lower_bounds[l].reshape(1, -1), hgrn_nw=hgrn_norm_w[l].reshape(1, -1),
        ))

    def zeros_like_state(s):
        return jnp.zeros((depth, nb) + s.shape[2:], F32)

    p_states = tuple(zeros_like_state(s) for s in
                     (state_s5_re, state_s5_im, state_rwkv_shift, state_rwkv_wkv, state_hgrn))
    s_states = (state_s5_re, state_s5_im, state_rwkv_shift, state_rwkv_wkv, state_hgrn)
    ffw = dict(f1g=ffn1_w_gate, f1u=ffn1_w_up, f1d=ffn1_w_down, f2g=ffn2_w_gate, f2u=ffn2_w_up, f2d=ffn2_w_down)
    (y_s, s5re_s, s5im_s, shift_s, wkv_s, hgrn_s), ffb = _run(
        x_sample.reshape(ns * dseq, d), ns, dseq, s_states, lw, ffw, norm_final)
    (y_p, s5re_p, s5im_p, shift_p, wkv_p, hgrn_p), _ = _run(
        x_prompt.reshape(nb * seq, d), nb, seq, p_states, lw, ffb, norm_final)
    return (y_p.reshape(nb, seq, d), y_s.reshape(ns, dseq, d), s5re_p, s5im_p, shift_p, wkv_p, hgrn_p,
            s5re_s, s5im_s, shift_s, wkv_s, hgrn_s)
```

```python
import functools
import math

import numpy as np
import jax
import jax.numpy as jnp
from jax import lax
from jax.experimental import pallas as pl
from jax.experimental.pallas import tpu as pltpu

F32 = jnp.float32
BF16 = jnp.bfloat16

NORM_EPS = 1e-6
RWKV_GN_EPS = 64e-5
HGRN_EPS = 1e-5

D_MODEL = 2048
S5_WIDTH = 512
S5_GROUP = 16
S5_GROUPS = 32
S5_STATE = 64
S5_LANES = S5_GROUPS * S5_STATE
S5_GB = 4
S5_TR = 512
S5_SEG = S5_TR // 8
RWKV_WIDTH = 768
RWKV_HEAD = 64
RWKV_HEADS = 12
RWKV_PAIRS = 6
RWKV_SHIFT_WIDTH = 2560
RWKV_ZBLOCKS = RWKV_SHIFT_WIDTH // S5_WIDTH
RWKV_BLOCK = 16
RWKV_TM = 128
SEG_CHUNK = 256
HGRN_WIDTH = 768
HGRN_HEADS = 6
HGRN_DIM = 128
N_IN = 6144

SUBLANES = 8
LANES = 128
VMEM_LIMIT = 60 * 1024 * 1024


def _cparams(sem):
    return pltpu.CompilerParams(dimension_semantics=sem, vmem_limit_bytes=VMEM_LIMIT)


def _split2(x):
    hi = x.astype(BF16)
    lo = (x - hi.astype(F32)).astype(BF16)
    return hi, lo


def _split3(x):
    p1 = x.astype(BF16)
    r1 = x - p1.astype(F32)
    p2 = r1.astype(BF16)
    p3 = (r1 - p2.astype(F32)).astype(BF16)
    return p1, p2, p3


def _dot(a, b):
    return jnp.dot(a, b, preferred_element_type=F32)


def _dot_nt(a, b):
    return lax.dot_general(a, b, (((1,), (1,)), ((), ())), preferred_element_type=F32)


def _dot_tn(a, b):
    return lax.dot_general(a, b, (((0,), (0,)), ((), ())), preferred_element_type=F32)


def _sigmoid(x):
    return 1.0 / (1.0 + jnp.exp(-x))


def _ffn_kernel(*refs, cast, final):
    x_ref, nw_ref, wg_ref, wu_ref, wd_ref = refs[:5]
    refs = refs[5:]
    if final:
        fnw_ref = refs[0]
        refs = refs[1:]
    if cast:
        o_ref, wgb_ref, wub_ref, wdb_ref, h_ref = refs
    else:
        o_ref, h_ref = refs
    j = pl.program_id(1)

    @pl.when(j == 0)
    def _():
        rows = o_ref.shape[0]
        step = min(rows, 256)
        for r0 in range(0, rows, step):
            rs = slice(r0, r0 + step)
            x = x_ref[rs, :]
            ms = jnp.mean(x * x, axis=-1, keepdims=True)
            h_ref[rs, :] = (x * lax.rsqrt(ms + NORM_EPS) * nw_ref[...]).astype(BF16)
        o_ref[...] = jnp.zeros_like(o_ref)

    wg = wg_ref[...].astype(BF16)
    wu = wu_ref[...].astype(BF16)
    wd = wd_ref[...].astype(BF16)
    if cast:
        wgb_ref[...] = wg
        wub_ref[...] = wu
        wdb_ref[...] = wd
    h = h_ref[...]
    g = _dot(h, wg)
    u = _dot(h, wu)
    o_ref[...] += _dot((g * _sigmoid(g) * u).astype(BF16), wd)

    @pl.when(j == pl.num_programs(1) - 1)
    def _():
        rows = o_ref.shape[0]
        step = min(rows, 256)
        for r0 in range(0, rows, step):
            rs = slice(r0, r0 + step)
            o = x_ref[rs, :] + 0.5 * o_ref[rs, :]
            if final:
                o = o * lax.rsqrt(jnp.mean(o * o, axis=-1, keepdims=True) + NORM_EPS) * fnw_ref[...]
            o_ref[rs, :] = o


def _ffn(x, nw, wg, wu, wd, l, cast=False, fnw=None, tm=1024):
    m, d = x.shape
    ff = wg.shape[2]
    tf = 256 if cast else 512
    final = fnw is not None
    if final and not cast:
        tm = tm // 2
    out_specs = [pl.BlockSpec((tm, d), lambda i, j: (i, 0))]
    out_shape = [jax.ShapeDtypeStruct((m, d), F32)]
    if cast:
        assert m == tm
        out_specs += [pl.BlockSpec((d, tf), lambda i, j: (0, j)), pl.BlockSpec((d, tf), lambda i, j: (0, j)),
                      pl.BlockSpec((tf, d), lambda i, j: (j, 0))]
        out_shape += [jax.ShapeDtypeStruct((d, ff), BF16), jax.ShapeDtypeStruct((d, ff), BF16),
                      jax.ShapeDtypeStruct((ff, d), BF16)]
    vec = pl.BlockSpec((1, d), lambda i, j: (0, 0))
    outs = pl.pallas_call(
        functools.partial(_ffn_kernel, cast=cast, final=final),
        grid=(m // tm, ff // tf),
        in_specs=[
            pl.BlockSpec((tm, d), lambda i, j: (i, 0)),
            vec,
            pl.BlockSpec((None, d, tf), lambda i, j: (l, 0, j)),
            pl.BlockSpec((None, d, tf), lambda i, j: (l, 0, j)),
            pl.BlockSpec((None, tf, d), lambda i, j: (l, j, 0)),
        ] + ([vec] if final else []),
        out_specs=out_specs,
        out_shape=out_shape,
        scratch_shapes=[pltpu.VMEM((tm, d), BF16)],
        compiler_params=_cparams(("parallel", "arbitrary")),
        name="ffn_cast" if cast else "ffn",
    )(x, nw, wg, wu, wd, *((fnw,) if final else ()))
    return outs if cast else outs[0]


def _inproj_kernel(x_ref, nw_ref, w_ref, o_ref, h_ref):
    @pl.when(pl.program_id(1) == 0)
    def _():
        x = x_ref[...]
        ms = jnp.mean(x * x, axis=-1, keepdims=True)
        h_ref[...] = (x * lax.rsqrt(ms + NORM_EPS) * nw_ref[...]).astype(BF16)

    o_ref[...] = _dot(h_ref[...], w_ref[...])


def _inproj(x, nw, w, l, tm=1024, tn=1024):
    m, d = x.shape
    n = w.shape[2]
    return pl.pallas_call(
        _inproj_kernel,
        grid=(m // tm, n // tn),
        in_specs=[
            pl.BlockSpec((tm, d), lambda i, j: (i, 0)),
            pl.BlockSpec((1, d), lambda i, j: (0, 0)),
            pl.BlockSpec((None, d, tn), lambda i, j: (l, 0, j)),
        ],
        out_specs=pl.BlockSpec((tm, tn), lambda i, j: (i, j)),
        out_shape=jax.ShapeDtypeStruct((m, n), F32),
        scratch_shapes=[pltpu.VMEM((tm, d), BF16)],
        compiler_params=_cparams(("parallel", "arbitrary")),
        name="inproj",
    )(x, nw, w)


def _outproj_kernel(x_ref, a_ref, y_ref, bonus_ref, g_ref, c_ref, w_ref, lnw_ref, lnb_ref, e2_ref, o_ref):
    y = y_ref[...]
    e2 = e2_ref[...]
    mean = _seg_sum(y, e2) * (1.0 / RWKV_HEAD)
    yc = y - mean
    var = _seg_sum(yc * yc, e2) * (1.0 / RWKV_HEAD)
    b = (yc * lax.rsqrt(var + RWKV_GN_EPS) * lnw_ref[...] + lnb_ref[...] + bonus_ref[...]) * g_ref[...]
    o1 = S5_WIDTH
    o2 = S5_WIDTH + RWKV_WIDTH
    acc = _dot(a_ref[...].astype(BF16), w_ref[0:o1, :])
    acc += _dot(b.astype(BF16), w_ref[o1:o2, :])
    acc += _dot(c_ref[...].astype(BF16), w_ref[o2:, :])
    o_ref[...] = x_ref[...] + acc


def _outproj(x, ya, y_rw, bonus, g, yc, w_all, l, cs, tm=512):
    m, d = x.shape
    row = lambda i: (i, 0)
    whole = lambda i: (0, 0)
    wide = lambda a: pl.BlockSpec((tm, a.shape[1]), row)
    vec = pl.BlockSpec((1, RWKV_WIDTH), whole)
    return pl.pallas_call(
        _outproj_kernel,
        grid=(m // tm,),
        in_specs=[
            pl.BlockSpec((tm, d), row), wide(ya), wide(y_rw), wide(bonus), wide(g), wide(yc),
            pl.BlockSpec((None,) + w_all.shape[1:], lambda i: (l, 0, 0)),
            vec, vec, pl.BlockSpec((2 * SEG_CHUNK, SEG_CHUNK), whole),
        ],
        out_specs=pl.BlockSpec((tm, d), row),
        out_shape=jax.ShapeDtypeStruct((m, d), F32),
        compiler_params=_cparams(("parallel",)),
        name="outproj",
    )(x, ya, y_rw, bonus, g, yc, w_all, cs["ln_w"], cs["ln_b"], cs["e2w"])


def _s5_kernel(u_ref, x0r_ref, x0i_ref, b3r_ref, b3i_ref, cr_ref, ci_ref, cst_ref, d_ref, wglu_ref,
               bglu_ref, y_ref, xlr_ref, xli_ref, xr_ref, xi_ref, cr_s, ci_s, pm_ref, *, tr, chained):
    c = pl.program_id(1)
    nlt = S5_LANES // LANES
    per_gb = nlt // S5_GB

    def load_tile(ref, rows):
        return jnp.concatenate([ref[k, rows, :] for k in range(nlt)], axis=1)

    def store_tile(ref, rows, val):
        for k in range(nlt):
            ref[k, rows, :] = val[:, k * LANES:(k + 1) * LANES]

    u = u_ref[...]
    nut = S5_WIDTH // LANES
    seg = tr // SUBLANES
    if chained:
        for k in range(nut):
            pm_ref[k] = u[:, k * LANES:(k + 1) * LANES]
        u = jnp.concatenate(
            [jnp.concatenate([pm_ref[k, pl.ds(tau, SUBLANES, stride=seg), :] for tau in range(seg)], axis=0)
             for k in range(nut)], axis=1)
    uh, ul = _split2(u)
    for gb in range(S5_GB):
        sl = slice(gb * LANES, (gb + 1) * LANES)
        if chained:
            br = _dot(uh[:, sl], b3r_ref[gb, 0:LANES, :])
            bi = _dot(uh[:, sl], b3i_ref[gb, 0:LANES, :])
        else:
            lhs = jnp.concatenate([uh[:, sl], ul[:, sl], uh[:, sl]], axis=1)
            br = _dot(lhs, b3r_ref[gb])
            bi = _dot(lhs, b3i_ref[gb])
        for k in range(per_gb):
            xr_ref[gb * per_gb + k] = br[:, k * LANES:(k + 1) * LANES]
            xi_ref[gb * per_gb + k] = bi[:, k * LANES:(k + 1) * LANES]

    if chained:
        @pl.when(c == 0)
        def _():
            cr_s[...] = x0r_ref[0]
            ci_s[...] = x0i_ref[0]

    a1r, a1i, a2r, a2i, a4r, a4i, pr, pi = [cst_ref[i] for i in range(8)]

    def tile_scan(xr, xi, x0r, x0i, lv1, lv2, lv4, pw):
        for k, (ar, ai) in ((1, lv1), (2, lv2), (4, lv4)):
            sr = pltpu.roll(xr, k, 0)
            si = pltpu.roll(xi, k, 0)
            xr, xi = xr + ar * sr - ai * si, xi + ar * si + ai * sr
        x0r = jnp.broadcast_to(x0r, xr.shape)
        x0i = jnp.broadcast_to(x0i, xr.shape)
        return xr + pw[0] * x0r - pw[1] * x0i, xi + pw[0] * x0i + pw[1] * x0r

    if chained:
        seg = tr // SUBLANES
        b1r, b1i, b2r, b2i, b4r, b4i, qr64, qi64 = [cst_ref[8 + i] for i in range(8)]
        lt = lambda x, k: x[:, k * LANES:(k + 1) * LANES]
        ar = [jnp.broadcast_to(lt(pr, k)[0:1, :], (SUBLANES, LANES)) for k in range(nlt)]
        ai = [jnp.broadcast_to(lt(pi, k)[0:1, :], (SUBLANES, LANES)) for k in range(nlt)]

        def sweep1(tau, carry):
            rows = pl.ds(pl.multiple_of(tau * SUBLANES, SUBLANES), SUBLANES)
            out = []
            for k in range(nlt):
                xr, xi = carry[2 * k], carry[2 * k + 1]
                xr, xi = (ar[k] * xr - ai[k] * xi + xr_ref[k, rows, :], ar[k] * xi + ai[k] * xr + xi_ref[k, rows, :])
                xr_ref[k, rows, :] = xr
                xi_ref[k, rows, :] = xi
                out += [xr, xi]
            return tuple(out)

        zero = jnp.zeros((SUBLANES, LANES), F32)
        ends = lax.fori_loop(0, seg, sweep1, (zero,) * (2 * nlt))
        er = jnp.concatenate(ends[0::2], axis=1)
        ei = jnp.concatenate(ends[1::2], axis=1)
        x0r = cr_s[...]
        x0i = ci_s[...]
        yr, yi = tile_scan(er, ei, x0r, x0i, (b1r, b1i), (b2r, b2i), (b4r, b4i), (qr64, qi64))
        cr_s[...] = yr[SUBLANES - 1:SUBLANES, :]
        ci_s[...] = yi[SUBLANES - 1:SUBLANES, :]
        rid = lax.broadcasted_iota(jnp.int32, pr.shape, 0)
        sr0 = jnp.where(rid == 0, jnp.broadcast_to(x0r, pr.shape), pltpu.roll(yr, 1, 0))
        si0 = jnp.where(rid == 0, jnp.broadcast_to(x0i, pr.shape), pltpu.roll(yi, 1, 0))

        def sweep2(tau, carry):
            rows = pl.ds(pl.multiple_of(tau * SUBLANES, SUBLANES), SUBLANES)
            out = []
            for k in range(nlt):
                qr, qi = carry[2 * k], carry[2 * k + 1]
                qr, qi = ar[k] * qr - ai[k] * qi, ar[k] * qi + ai[k] * qr
                xr_ref[k, rows, :] = xr_ref[k, rows, :] + qr
                xi_ref[k, rows, :] = xi_ref[k, rows, :] + qi
                out += [qr, qi]
            return tuple(out)

        start = []
        for k in range(nlt):
            start += [lt(sr0, k), lt(si0, k)]
        lax.fori_loop(0, seg, sweep2, tuple(start))

        @pl.when(c == pl.num_programs(1) - 1)
        def _():
            xlr_ref[0] = cr_s[...]
            xli_ref[0] = ci_s[...]
    else:
        def tile(j, carry):
            rows = pl.ds(pl.multiple_of(j * SUBLANES, SUBLANES), SUBLANES)
            xr, xi = tile_scan(load_tile(xr_ref, rows), load_tile(xi_ref, rows),
                               x0r_ref[pl.ds(j, 1), :], x0i_ref[pl.ds(j, 1), :],
                               (a1r, a1i), (a2r, a2i), (a4r, a4i), (pr, pi))
            store_tile(xr_ref, rows, xr)
            store_tile(xi_ref, rows, xi)
            xlr_ref[pl.ds(j, 1), :] = xr[SUBLANES - 1:SUBLANES, :]
            xli_ref[pl.ds(j, 1), :] = xi[SUBLANES - 1:SUBLANES, :]
            return carry

        lax.fori_loop(0, tr // SUBLANES, tile, 0)

    ys = []
    for gb in range(S5_GB):
        xr = jnp.concatenate([xr_ref[gb * per_gb + k] for k in range(per_gb)], axis=1)
        xi = jnp.concatenate([xi_ref[gb * per_gb + k] for k in range(per_gb)], axis=1)
        ys.append(_dot(xr.astype(BF16), cr_ref[gb]) + _dot(xi.astype(BF16), ci_ref[gb]))
    y = jnp.concatenate(ys, axis=1) + d_ref[...] * u
    y = 0.5 * y * (1.0 + jnp.tanh(math.sqrt(2.0 / math.pi) * (y + 0.044715 * (y * y * y))))
    z = _dot(y.astype(BF16), wglu_ref[...]) + bglu_ref[...]
    y = y * _sigmoid(z)
    if chained:
        for k in range(nut):
            pm_ref[k] = y[:, k * LANES:(k + 1) * LANES]
        for m in range(tr // SUBLANES):
            start = (m % SUBLANES) * SUBLANES * SUBLANES + m // SUBLANES
            for k in range(nut):
                y_ref[m * SUBLANES:(m + 1) * SUBLANES, k * LANES:(k + 1) * LANES] = (
                    pm_ref[k, pl.ds(start, SUBLANES, stride=SUBLANES), :])
    else:
        y_ref[...] = y


def _s5(p3, x0r, x0i, l, cs, nseq, t, tr=S5_TR):
    rows = nseq * t
    depth = x0r.shape[0]
    chained = t > SUBLANES
    if chained:
        grid = (nseq, t // tr)
        row_map = lambda n, c: (n * (t // tr) + c, 0)
        out_map = lambda n, c: (n * (t // tr) + c, 0)
        x0r = x0r.reshape(depth, nseq, 1, S5_LANES)
        x0i = x0i.reshape(depth, nseq, 1, S5_LANES)
        in_st = pl.BlockSpec((None, 1, 1, S5_LANES), lambda n, c: (l, n, 0, 0))
        st_spec = pl.BlockSpec((1, 1, S5_LANES), lambda n, c: (n, 0, 0))
        st_shape = jax.ShapeDtypeStruct((nseq, 1, S5_LANES), F32)
    else:
        assert t == SUBLANES
        grid = (rows // tr, 1)
        row_map = lambda n, c: (n, 0)
        out_map = lambda n, c: (n, 0)
        in_st = pl.BlockSpec((None, tr // SUBLANES, S5_LANES), lambda n, c: (l, n, 0))
        st_spec = pl.BlockSpec((tr // SUBLANES, S5_LANES), lambda n, c: (n, 0))
        st_shape = jax.ShapeDtypeStruct((nseq, S5_LANES), F32)
    whole3 = lambda n, c: (0, 0, 0)
    whole2 = lambda n, c: (0, 0)
    y, xlr, xli = pl.pallas_call(
        functools.partial(_s5_kernel, tr=tr, chained=chained),
        grid=grid,
        in_specs=[
            pl.BlockSpec((tr, S5_WIDTH), row_map),
            in_st, in_st,
            pl.BlockSpec(cs["b3r"].shape, whole3),
            pl.BlockSpec(cs["b3i"].shape, whole3),
            pl.BlockSpec(cs["cr"].shape, whole3),
            pl.BlockSpec(cs["ci"].shape, whole3),
            pl.BlockSpec(cs["scan"].shape, whole3),
            pl.BlockSpec((1, S5_WIDTH), whole2),
            pl.BlockSpec((S5_WIDTH, S5_WIDTH), whole2),
            pl.BlockSpec((1, S5_WIDTH), whole2),
        ],
        out_specs=[pl.BlockSpec((tr, S5_WIDTH), out_map), st_spec, st_spec],
        out_shape=[jax.ShapeDtypeStruct((rows, S5_WIDTH), F32), st_shape, st_shape],
        scratch_shapes=[pltpu.VMEM((S5_LANES // LANES, tr, LANES), F32), pltpu.VMEM((S5_LANES // LANES, tr, LANES), F32),
                        pltpu.VMEM((1, S5_LANES), F32), pltpu.VMEM((1, S5_LANES), F32),
                        pltpu.VMEM((S5_WIDTH // LANES, tr, LANES), F32)],
        compiler_params=_cparams(("parallel", "arbitrary")),
        name="s5_chain" if chained else "s5_tile",
    )(p3, x0r, x0i, cs["b3r"], cs["b3i"], cs["cr"], cs["ci"], cs["scan"], cs["d"], cs["wglu"], cs["bglu"])
    return y, xlr.reshape(nseq, S5_GROUPS, S5_STATE), xli.reshape(nseq, S5_GROUPS, S5_STATE)


def _s5_consts(a_re, a_im, log_dt, b_re, b_im, c_re, c_im, d, w_glu, b_glu):
    g, p, h = S5_GROUPS, S5_STATE, S5_GROUP
    dt = jnp.exp(log_dt)[:, None]
    mag = jnp.exp(a_re * dt)
    abr = mag * jnp.cos(a_im * dt)
    abi = mag * jnp.sin(a_im * dt)
    den = a_re * a_re + a_im * a_im
    qr = ((abr - 1.0) * a_re + abi * a_im) / den
    qi = (abi * a_re - (abr - 1.0) * a_im) / den
    bbr = qr[..., None] * b_re - qi[..., None] * b_im
    bbi = qr[..., None] * b_im + qi[..., None] * b_re
    row = jnp.arange(SUBLANES)[:, None]

    def scan_tables(mr, mi):
        pr, pi = [mr], [mi]
        for _ in range(SUBLANES - 1):
            pr, pi = pr + [pr[-1] * mr - pi[-1] * mi], pi + [pr[-1] * mi + pi[-1] * mr]
        pr = jnp.stack(pr).reshape(SUBLANES, g * p)
        pi = jnp.stack(pi).reshape(SUBLANES, g * p)

        def lvl(k):
            return [jnp.where(row >= k, pr[k - 1][None, :], 0.0), jnp.where(row >= k, pi[k - 1][None, :], 0.0)]

        return lvl(1) + lvl(2) + lvl(4) + [pr, pi]

    sr, si = abr, abi
    for _ in range(S5_SEG.bit_length() - 1):
        sr, si = sr * sr - si * si, 2.0 * sr * si
    scan = jnp.stack(scan_tables(abr, abi) + scan_tables(sr, si)).astype(F32)
    eye = jnp.eye(SUBLANES, dtype=F32)

    def in_blocks(x):
        xx = x.reshape(S5_GB, SUBLANES, p, h).transpose(0, 1, 3, 2)
        blk = (xx[:, :, :, None, :] * eye[None, :, None, :, None]).reshape(S5_GB, SUBLANES * h, SUBLANES * p)
        hi = blk.astype(BF16)
        lo = (blk - hi.astype(F32)).astype(BF16)
        return jnp.concatenate([hi, hi, lo], axis=1)

    def out_blocks(x):
        xx = x.reshape(S5_GB, SUBLANES, h, p).transpose(0, 1, 3, 2)
        return (xx[:, :, :, None, :] * eye[None, :, None, :, None]).reshape(
            S5_GB, SUBLANES * p, SUBLANES * h).astype(BF16)

    return dict(b3r=in_blocks(bbr), b3i=in_blocks(bbi),
                cr=out_blocks(c_re), ci=out_blocks(-c_im), scan=scan,
                d=d.reshape(1, -1), wglu=w_glu.astype(BF16), bglu=b_glu.reshape(1, -1))


def _seg_sum(x, e2):
    hi, lo = _split2(x)
    cw = e2.shape[1]
    return jnp.concatenate(
        [_dot(jnp.concatenate([hi[:, c:c + cw], lo[:, c:c + cw]], axis=1), e2) for c in range(0, x.shape[1], cw)],
        axis=1)


def _rwkv_prep_kernel(*refs, nlev, chained, tiles_per_seq):
    nz = RWKV_ZBLOCKS
    z = jnp.concatenate([r[...] for r in refs[:nz]], axis=1)
    if chained:
        prev8 = jnp.concatenate([r[...] for r in refs[nz:2 * nz]], axis=1)
        sh_ref = refs[2 * nz]
        refs = refs[2 * nz + 1:]
    else:
        zp_ref = refs[nz]
        refs = refs[nz + 1:]
    (mu_ref, w0_ref, a0_ref, w2_ref, a2_ref, g2_ref, kk_ref, ka_ref, rk_ref, e2_ref, tril_ref, mk_ref, hm_ref,
     ap_o, rp_o, bt_o, kp_o, cc_o, y0_o, v_o, gm_o, g_o, bonus_o) = refs
    tm = RWKV_TM
    if chained:
        first = (pl.program_id(0) % tiles_per_seq) == 0
        prev_row = jnp.where(first, sh_ref[0], prev8[SUBLANES - 1:SUBLANES, :])
        rid = lax.broadcasted_iota(jnp.int32, z.shape, 0)
        zp = jnp.where(rid == 0, prev_row, pltpu.roll(z, 1, 0))
    else:
        zp = zp_ref[...]
    zm = z + (zp - z) * mu_ref[...]
    w = RWKV_WIDTH
    r = zm[:, 0:w]
    k = zm[:, w:2 * w]
    v = zm[:, 2 * w:3 * w]
    wa = zm[:, 3 * w:3 * w + LANES]
    gi = zm[:, 3 * w + LANES:3 * w + 2 * LANES]
    e2 = e2_ref[...]
    wl = w0_ref[...] + _dot(jnp.tanh(wa).astype(BF16), w2_ref[...])
    sp = jnp.maximum(-wl, 0.0) + jnp.log(1.0 + jnp.exp(-jnp.abs(wl)))
    ld = -jnp.exp(-sp - 0.5)
    a = _sigmoid(a0_ref[...] + _dot(wa.astype(BF16), a2_ref[...]))
    g_o[...] = _dot(_sigmoid(gi).astype(BF16), g2_ref[...])
    kk = k * kk_ref[...]
    kk = kk * lax.rsqrt(jnp.maximum(_seg_sum(kk * kk, e2), 1e-24))
    k2 = k * (1.0 + (a - 1.0) * ka_ref[...])
    b = kk * a
    bonus_o[...] = _seg_sum(r * k2 * rk_ref[...], e2) * v
    v_o[...] = v

    p1, p2, p3 = _split3(ld)
    cum = _dot(tril_ref[...], jnp.concatenate([p1, p2, p3], axis=0))
    gam = jnp.exp(cum)
    ginv = jnp.exp(-cum)
    gm_o[...] = gam
    alpha = kk * jnp.exp(cum - ld)
    beta = b * ginv
    kap = k2 * ginv
    rho = r * gam
    bt_o[...] = beta
    kp_o[...] = kap

    strict = mk_ref[0]
    incl = mk_ref[1]
    eye = mk_ref[2]
    heads = [(j, e) for j in range(RWKV_PAIRS) for e in range(2)]
    nh = len(heads)

    def lanes(j):
        return slice(j * LANES, (j + 1) * LANES)

    al = [alpha[:, lanes(j)] * hm_ref[e] for j, e in heads]
    rh = [rho[:, lanes(j)] * hm_ref[e] for j, e in heads]
    vh = [(v[:, lanes(j)] * hm_ref[e]).astype(BF16) for j, e in heads]
    yk = [jnp.concatenate([beta[:, lanes(j)], kap[:, lanes(j)]], axis=0).astype(BF16) for j in range(RWKV_PAIRS)]
    wu = [_dot_nt(jnp.concatenate(_split2(al[i]), axis=1), jnp.concatenate([yk[j], yk[j]], axis=1))
          for i, (j, e) in enumerate(heads)]
    wl = [_dot_nt(rh[i].astype(BF16), yk[j]) for i, (j, e) in enumerate(heads)]
    mm = [wu[i][:, :tm] * strict for i in range(nh)]
    nn = [wu[i][:, tm:] * strict for i in range(nh)]
    pp = [(wl[i][:, :tm] * incl).astype(BF16) for i in range(nh)]
    qq = [(wl[i][:, tm:] * incl).astype(BF16) for i in range(nh)]
    tinv = [eye - mm[i] * mk_ref[3] for i in range(nh)]
    for lv in range(1, nlev):
        th = [tinv[i].astype(BF16) for i in range(nh)]
        tmp = [_dot(th[i], (mm[i] * mk_ref[3 + lv]).astype(BF16)) for i in range(nh)]
        tinv = [tinv[i] - _dot(tmp[i].astype(BF16), th[i]) for i in range(nh)]
    th = [tinv[i].astype(BF16) for i in range(nh)]
    nv = [_dot(nn[i].astype(BF16), vh[i]) for i in range(nh)]
    qv = [_dot(qq[i], vh[i]) for i in range(nh)]
    tac = [_dot(th[i], jnp.concatenate([al[i], nv[i]], axis=1).astype(BF16)) for i in range(nh)]
    pac = [_dot(pp[i], tac[i].astype(BF16)) for i in range(nh)]
    ap = [tac[i][:, :LANES] for i in range(nh)]
    cc = [tac[i][:, LANES:] for i in range(nh)]
    rp = [rh[i] - pac[i][:, :LANES] for i in range(nh)]
    y0 = [qv[i] - pac[i][:, LANES:] for i in range(nh)]
    for j in range(RWKV_PAIRS):
        ap_o[:, lanes(j)] = ap[2 * j] + ap[2 * j + 1]
        rp_o[:, lanes(j)] = rp[2 * j] + rp[2 * j + 1]
        cc_o[:, lanes(j)] = cc[2 * j] + cc[2 * j + 1]
        y0_o[:, lanes(j)] = y0[2 * j] + y0[2 * j + 1]


def _rwkv_tables(tm, blk):
    t = np.arange(tm)
    same = (t[:, None] // blk) == (t[None, :] // blk)
    tril = (same & (t[None, :] <= t[:, None])).astype(np.float32)
    masks = [same & (t[None, :] < t[:, None]), same & (t[None, :] <= t[:, None]), np.eye(tm, dtype=bool)]
    n = 2
    while n <= blk:
        m = n // 2
        masks.append(((t[:, None] // n) == (t[None, :] // n)) & ((t[:, None] % n) >= m) & ((t[None, :] % n) < m))
        n *= 2
    nlev = len(masks) - 3
    hm = np.stack([(np.arange(LANES) < RWKV_HEAD), (np.arange(LANES) >= RWKV_HEAD)]).astype(np.float32)
    return (jnp.asarray(np.concatenate([tril, tril, tril], axis=1), BF16),
            jnp.asarray(np.stack(masks).astype(np.float32)), jnp.asarray(hm.reshape(2, 1, LANES)), nlev)


def _rwkv_prep(p3, prev, cs, nseq, t):
    m = nseq * t
    tm = RWKV_TM
    blk = min(RWKV_BLOCK, t)
    chained = t >= tm
    tril3, masks, hm, nlev = _rwkv_tables(tm, blk)
    w = RWKV_WIDTH
    row = lambda i: (i, 0)
    whole = lambda i: (0, 0)
    whole3 = lambda i: (0, 0, 0)
    vec = pl.BlockSpec((1, w), whole)
    zspecs = [pl.BlockSpec((tm, S5_WIDTH), lambda i, c=c: (i, 1 + c)) for c in range(RWKV_ZBLOCKS)]
    if chained:
        tps = t // tm
        lead = zspecs + [pl.BlockSpec((SUBLANES, S5_WIDTH),
                                      lambda i, c=c: (jnp.maximum(i * (tm // SUBLANES) - 1, 0), 1 + c))
                         for c in range(RWKV_ZBLOCKS)]
        lead.append(pl.BlockSpec((1, 1, RWKV_SHIFT_WIDTH), lambda i: (i // tps, 0, 0)))
        args = (p3,) * (2 * RWKV_ZBLOCKS) + (prev,)
    else:
        tps = 1
        lead = zspecs + [pl.BlockSpec((tm, RWKV_SHIFT_WIDTH), row)]
        args = (p3,) * RWKV_ZBLOCKS + (prev,)
    return pl.pallas_call(
        functools.partial(_rwkv_prep_kernel, nlev=nlev, chained=chained, tiles_per_seq=tps),
        grid=(m // tm,),
        in_specs=lead + [
            pl.BlockSpec((1, RWKV_SHIFT_WIDTH), whole),
            vec, vec,
            pl.BlockSpec((LANES, w), whole), pl.BlockSpec((LANES, w), whole), pl.BlockSpec((LANES, w), whole),
            vec, vec, vec,
            pl.BlockSpec((2 * SEG_CHUNK, SEG_CHUNK), whole),
            pl.BlockSpec(tril3.shape, whole), pl.BlockSpec(masks.shape, whole3), pl.BlockSpec(hm.shape, whole3),
        ],
        out_specs=[pl.BlockSpec((tm, w), row)] * 10,
        out_shape=[jax.ShapeDtypeStruct((m, w), F32)] * 10,
        compiler_params=_cparams(("parallel",)),
        name="rwkv_prep",
    )(*args, cs["mu"], cs["w0"], cs["a0"], cs["w2"], cs["a2"], cs["g2"], cs["k_k"], cs["k_a"], cs["r_k"],
      cs["e2w"], tril3, masks, hm)


def _rwkv_rec_kernel(*refs, nbp, tc, blk, aliased):
    ap_ref, rp_ref, bt_ref, kp_ref, cc_ref, y0_ref, v_ref, gm_ref, s0_ref, bd_ref = refs[:10]
    y_ref, so_ref, s_ref = refs[10 + (1 if aliased else 0):]
    c = pl.program_id(1)
    hd = RWKV_HEAD
    pairs = [(n, j) for n in range(nbp) for j in range(RWKV_PAIRS)]

    @pl.when(c == 0)
    def _():
        zero = jnp.zeros((hd, hd), F32)
        for idx, (n, j) in enumerate(pairs):
            top = jnp.concatenate([s0_ref[n, 2 * j], zero], axis=1)
            bot = jnp.concatenate([zero, s0_ref[n, 2 * j + 1]], axis=1)
            s_ref[idx] = jnp.concatenate([top, bot], axis=0)

    bd = bd_ref[...]

    def block(bi, carry):
        rows = pl.ds(pl.multiple_of(bi * blk, blk), blk)
        gmats = []
        for idx, (n, j) in enumerate(pairs):
            ls = slice(j * LANES, (j + 1) * LANES)
            sh, sl = _split2(s_ref[idx])
            xh = jnp.concatenate([ap_ref[n, rows, ls], rp_ref[n, rows, ls]], axis=0).astype(BF16)
            gmats.append(_dot_nt(jnp.concatenate([xh, xh], axis=1), jnp.concatenate([sh, sl], axis=1)))
        upds = []
        for idx, (n, j) in enumerate(pairs):
            ls = slice(j * LANES, (j + 1) * LANES)
            gmat = gmats[idx]
            e = -gmat[:blk] - cc_ref[n, rows, ls]
            y_ref[n, rows, ls] = gmat[blk:] + y0_ref[n, rows, ls]
            evh, evl = _split2(jnp.concatenate([e, v_ref[n, rows, ls]], axis=0))
            bkh, bkl = _split2(jnp.concatenate([bt_ref[n, rows, ls], kp_ref[n, rows, ls]], axis=0))
            upds.append(_dot_tn(jnp.concatenate([evh, evh, evl], axis=0), jnp.concatenate([bkh, bkl, bkh], axis=0)))
        for idx, (n, j) in enumerate(pairs):
            ls = slice(j * LANES, (j + 1) * LANES)
            gl = gm_ref[n, rows, ls][blk - 1:blk, :]
            s_ref[idx] = (s_ref[idx] + upds[idx] * bd) * gl
        return carry

    lax.fori_loop(0, tc // blk, block, 0)

    @pl.when(c == pl.num_programs(1) - 1)
    def _():
        for idx, (n, j) in enumerate(pairs):
            s = s_ref[idx]
            so_ref[n, 2 * j] = s[:hd, :hd]
            so_ref[n, 2 * j + 1] = s[hd:, hd:]


def _rwkv_rec(arrs, s0_all, l, so_prev, cs, nseq, t, nbp=4, tc=128):
    w = RWKV_WIDTH
    tc = min(tc, t)
    blk = min(RWKV_BLOCK, t)
    depth = s0_all.shape[0]
    seq_map = lambda i, c: (i, c, 0)
    st_map = lambda i, c: (l, i, 0, 0, 0)
    blkspec = pl.BlockSpec((nbp, tc, w), seq_map)
    st = pl.BlockSpec((None, nbp, RWKV_HEADS, RWKV_HEAD, RWKV_HEAD), st_map)
    aliased = so_prev is not None
    in_specs = [blkspec] * 8 + [st, pl.BlockSpec((LANES, LANES), lambda i, c: (0, 0))]
    args = list(arrs) + [s0_all, cs["bd"]]
    aliases = {}
    if aliased:
        in_specs.append(pl.BlockSpec(memory_space=pl.ANY))
        args.append(so_prev)
        aliases = {10: 1}
    return pl.pallas_call(
        functools.partial(_rwkv_rec_kernel, nbp=nbp, tc=tc, blk=blk, aliased=aliased),
        grid=(nseq // nbp, t // tc),
        in_specs=in_specs,
        out_specs=[blkspec, st],
        out_shape=[jax.ShapeDtypeStruct((nseq, t, w), F32),
                   jax.ShapeDtypeStruct((depth, nseq, RWKV_HEADS, RWKV_HEAD, RWKV_HEAD), F32)],
        scratch_shapes=[pltpu.VMEM((nbp * RWKV_PAIRS, LANES, LANES), F32)],
        input_output_aliases=aliases,
        compiler_params=_cparams(("parallel", "arbitrary")),
        name="rwkv_rec",
    )(*args)


def _rwkv_consts(mu, w0, w2, a0, a2, g2, k_k, k_a, r_k, ln_w, ln_b):
    w = RWKV_WIDTH
    seg = np.arange(SEG_CHUNK) // RWKV_HEAD
    e_w = (seg[:, None] == seg[None, :]).astype(np.float32)
    seg = np.arange(LANES) // RWKV_HEAD
    bd = (seg[:, None] == seg[None, :]).astype(np.float32)
    z64 = jnp.zeros((RWKV_HEAD, w), F32)
    return dict(
        mu=mu.reshape(1, -1), w0=w0.reshape(1, -1), a0=a0.reshape(1, -1),
        w2=jnp.concatenate([w2, z64], axis=0).astype(BF16),
        a2=jnp.concatenate([z64, a2], axis=0).astype(BF16),
        g2=g2.astype(BF16), k_k=k_k.reshape(1, -1), k_a=k_a.reshape(1, -1), r_k=r_k.reshape(1, -1),
        ln_w=ln_w.reshape(1, -1), ln_b=ln_b.reshape(1, -1),
        e2w=jnp.asarray(np.concatenate([e_w, e_w], axis=0), BF16),
        bd=jnp.asarray(bd, F32))


def _rwkv(p3, shift_l, wkv_all, l, so_prev, cs, nseq, t):
    if t >= RWKV_TM:
        prev = shift_l.reshape(nseq, 1, RWKV_SHIFT_WIDTH)
    else:
        z3 = p3.reshape(nseq, t, N_IN)[:, :, S5_WIDTH:S5_WIDTH + RWKV_SHIFT_WIDTH]
        prev = jnp.concatenate([shift_l[:, None, :], z3[:, :-1]], axis=1).reshape(nseq * t, RWKV_SHIFT_WIDTH)
    outs = _rwkv_prep(p3, prev, cs, nseq, t)
    arrs = [a.reshape(nseq, t, RWKV_WIDTH) for a in outs[:8]]
    y, so = _rwkv_rec(arrs, wkv_all, l, so_prev, cs, nseq, t)
    return (y.reshape(nseq * t, RWKV_WIDTH), outs[9], outs[8]), so


def _hgrn_tables(c):
    t = np.arange(c)
    j = np.arange(c)
    mats = [(j[None, :] <= t[:, None]), (j[None, :] > t[:, None])]
    masks = []
    n = 2
    while n <= c:
        m = n // 2
        bs = (t // n) * n
        hi_half = (t % n) >= m
        mats.append((hi_half[:, None] & (j[None, :] >= (bs + m)[:, None]) & (j[None, :] <= t[:, None]))
                    | ((~hi_half)[:, None] & (j[None, :] > t[:, None]) & (j[None, :] <= (bs + m - 1)[:, None])))
        masks.append(((t[:, None] // n) == (t[None, :] // n)) & hi_half[:, None] & (~hi_half)[None, :])
        n *= 2
    w = np.concatenate(mats, axis=0).astype(np.float32)
    w3 = np.concatenate([w, w, w], axis=1)
    return jnp.asarray(w3, BF16), jnp.asarray(np.stack(masks).astype(np.float32)), len(masks)


def _hgrn_kernel(*refs, rb, c, nlev, chained, aliased, sub):
    q_ref, f_ref, i_ref, g_ref, s0_ref, lb_ref, nw_ref, w3_ref, mk_ref = refs[:9]
    o_ref, so_ref, st_ref = refs[9 + (1 if aliased else 0):]
    cidx = pl.program_id(1)
    hdim = HGRN_DIM

    if chained:
        @pl.when(cidx == 0)
        def _():
            for h in range(HGRN_HEADS):
                st_ref[h] = s0_ref[0, h].T

    w3 = w3_ref[...]
    row_i = lax.broadcasted_iota(jnp.int32, (c, c), 0)
    col_i = lax.broadcasted_iota(jnp.int32, (c, c), 1)
    eye = (row_i == col_i).astype(F32)

    heads = range(HGRN_HEADS)

    def lanes(h):
        return slice(h * hdim, (h + 1) * hdim)

    def block(it, carry):
        rows_s = [pl.ds(pl.multiple_of((it * sub + s) * c, c), c) for s in range(sub)]
        items = [(s, h) for s in range(sub) for h in heads]
        fg = {k: lb_ref[:, lanes(k[1])] + (1.0 - lb_ref[:, lanes(k[1])]) * _sigmoid(f_ref[rows_s[k[0]], lanes(k[1])])
              for k in items}
        kk = {k: 1.0 - fg[k] for k in items}
        qs = {}
        for k in items:
            q = q_ref[rows_s[k[0]], lanes(k[1])]
            qs[k] = q * _sigmoid(q)
        ex = {k: jnp.exp(_dot(w3, jnp.concatenate(_split3(jnp.log(fg[k])), axis=0))) for k in items}
        vb = {k: i_ref[rows_s[k[0]], lanes(k[1])].astype(BF16) for k in items}
        lev = {(k, lv): _dot_nt((qs[k] * ex[k][(2 + lv) * c:(3 + lv) * c]).astype(BF16),
                                (kk[k] * ex[k][(2 + lv) * c:(3 + lv) * c]).astype(BF16))
               for k in items for lv in range(nlev)}
        att = {}
        for k in items:
            a = eye * jnp.sum(qs[k] * kk[k], axis=-1, keepdims=True)
            for lv in range(nlev):
                a = a + jnp.where(mk_ref[lv] > 0.0, lev[(k, lv)], 0.0)
            att[k] = a.astype(BF16)
        intra = {k: _dot(att[k], vb[k]) for k in items}
        kv = {k: _dot_tn(vb[k], (kk[k] * ex[k][c:2 * c]).astype(BF16)) for k in items}
        qe = {k: (qs[k] * ex[k][0:c]).astype(BF16) for k in items}
        st = {h: st_ref[h] for h in heads} if chained else None
        for s in range(sub):
            if not chained:
                st = {h: s0_ref[it * sub + s, h].T for h in heads}
            inter = {h: _dot_nt(qe[(s, h)], st[h].astype(BF16)) for h in heads}
            st = {h: st[h] * ex[(s, h)][c - 1:c, :] + kv[(s, h)] for h in heads}
            if not chained:
                for h in heads:
                    so_ref[it * sub + s, h] = st[h].T
            for h in heads:
                o = inter[h] + intra[(s, h)]
                o = o * lax.rsqrt(jnp.mean(o * o, axis=-1, keepdims=True) + HGRN_EPS)
                gg = g_ref[rows_s[s], lanes(h)]
                o_ref[rows_s[s], lanes(h)] = o * nw_ref[:, lanes(h)] * (gg * _sigmoid(gg))
        if chained:
            for h in heads:
                st_ref[h] = st[h]
        return carry

    lax.fori_loop(0, rb // (c * sub), block, 0)

    if chained:
        @pl.when(cidx == pl.num_programs(1) - 1)
        def _():
            for h in range(HGRN_HEADS):
                so_ref[0, h] = st_ref[h].T


def _hgrn(p3, s0_all, l, so_prev, lb, nw, nseq, t):
    rows = nseq * t
    depth = s0_all.shape[0]
    c = math.gcd(t, 64)
    chained = t > c
    w3, masks, nlev = _hgrn_tables(c)
    w = HGRN_WIDTH
    if chained:
        rb = 256
        nblk = t // rb
        grid = (nseq, nblk)
        col = lambda cb: (lambda n, k: (n * nblk + k, cb))
        out_map = lambda n, k: (n * nblk + k, 0)
        nsb = 1
    else:
        nsb = 16
        rb = nsb * c
        grid = (rows // rb, 1)
        col = lambda cb: (lambda n, k: (n, cb))
        out_map = lambda n, k: (n, 0)
    st_spec = pl.BlockSpec((None, nsb, HGRN_HEADS, HGRN_DIM, HGRN_DIM), lambda n, k: (l, n, 0, 0, 0))
    whole2 = lambda n, k: (0, 0)
    whole3 = lambda n, k: (0, 0, 0)
    aliased = so_prev is not None
    in_specs = [pl.BlockSpec((rb, w), col(4)), pl.BlockSpec((rb, w), col(5)), pl.BlockSpec((rb, w), col(6)),
                pl.BlockSpec((rb, w), col(7)), st_spec,
                pl.BlockSpec((1, w), whole2), pl.BlockSpec((1, w), whole2),
                pl.BlockSpec(w3.shape, whole2), pl.BlockSpec(masks.shape, whole3)]
    args = [p3, p3, p3, p3, s0_all, lb, nw, w3, masks]
    aliases = {}
    if aliased:
        in_specs.append(pl.BlockSpec(memory_space=pl.ANY))
        args.append(so_prev)
        aliases = {9: 1}
    return pl.pallas_call(
        functools.partial(_hgrn_kernel, rb=rb, c=c, nlev=nlev, chained=chained, aliased=aliased,
                          sub=2 if chained else 4),
        grid=grid,
        in_specs=in_specs,
        out_specs=[pl.BlockSpec((rb, w), out_map), st_spec],
        out_shape=[jax.ShapeDtypeStruct((rows, w), F32),
                   jax.ShapeDtypeStruct((depth, nseq, HGRN_HEADS, HGRN_DIM, HGRN_DIM), F32)],
        scratch_shapes=[pltpu.VMEM((HGRN_HEADS, HGRN_DIM, HGRN_DIM), F32)],
        input_output_aliases=aliases,
        compiler_params=_cparams(("parallel", "arbitrary")),
        name="hgrn_chain" if chained else "hgrn_tile",
    )(*args)


def _run(x, nseq, t, states, lw, ffw, norm_final):
    s5_re0, s5_im0, shift0, wkv0, hgrn0 = states
    depth = len(lw)
    s5_re0 = s5_re0.reshape(depth, nseq, S5_LANES)
    s5_im0 = s5_im0.reshape(depth, nseq, S5_LANES)
    s5_re_l, s5_im_l, shift_l = [], [], []
    wkv_out = None
    hgrn_out = None
    cast = not isinstance(ffw["f1g"], list)
    ffb = {k: [] for k in ffw}

    def ffn(x, nw, names, l, fnw=None):
        if not cast:
            return _ffn(x, nw, *(ffw[k][l][None] for k in names), 0, fnw=fnw)
        res = _ffn(x, nw, *(ffw[k] for k in names), l, cast=True, fnw=fnw)
        for k, wb in zip(names, res[1:]):
            ffb[k].append(wb)
        return res[0]

    for l in range(depth):
        w = lw[l]
        x = ffn(x, w["norm_ffn1"], ("f1g", "f1u", "f1d"), l)
        p3 = _inproj(x, w["norm_mix"], w["w_in"], l)
        y_s5, s_re, s_im = _s5(p3, s5_re0, s5_im0, l, w["s5"], nseq, t)
        (y_rw, bonus, gate), wkv_out = _rwkv(p3, shift0[l], wkv0, l, wkv_out, w["rwkv"], nseq, t)
        y_hg, hgrn_out = _hgrn(p3, hgrn0, l, hgrn_out, w["hgrn_lb"], w["hgrn_nw"], nseq, t)
        x = _outproj(x, y_s5, y_rw, bonus, gate, y_hg, w["w_out"], l, w["rwkv"])
        x = ffn(x, w["norm_ffn2"], ("f2g", "f2u", "f2d"), l,
                fnw=norm_final.reshape(1, -1) if l == depth - 1 else None)
        s5_re_l.append(s_re)
        s5_im_l.append(s_im)
        shift_l.append(p3.reshape(nseq, t, N_IN)[:, t - 1, S5_WIDTH:S5_WIDTH + RWKV_SHIFT_WIDTH])
    return (x,jnp.stack(s5_re_l), jnp.stack(s5_im_l), jnp.stack(shift_l), wkv_out, hgrn_out), ffb


def kernel(x_prompt, x_sample, state_s5_re, state_s5_im, state_rwkv_shift, state_rwkv_wkv, state_hgrn, norm_ffn1, ffn1_w_gate, ffn1_w_up, ffn1_w_down, norm_mix, w_in, s5_a_re, s5_a_im, s5_log_dt, s5_b_re, s5_b_im, s5_c_re, s5_c_im, s5_d, s5_w_glu, s5_b_glu, rwkv_mu, rwkv_w0, rwkv_w2, rwkv_a0, rwkv_a2, rwkv_g2, rwkv_k_k, rwkv_k_a, rwkv_r_k, rwkv_ln_w, rwkv_ln_b, hgrn_lb_raw, hgrn_norm_w, w_out, norm_ffn2, ffn2_w_gate, ffn2_w_up, ffn2_w_down, norm_final):
    depth = w_in.shape[0]
    nb, seq, d = x_prompt.shape
    ns, dseq, _ = x_sample.shape

    p_lb = jax.nn.softmax(hgrn_lb_raw.astype(F32), axis=0)
    lower_bounds = jnp.cumsum(p_lb, axis=0) - p_lb[0]

    w_in_b = w_in.astype(BF16)
    w_out_b = w_out.astype(BF16)
    lw = []
    for l in range(depth):
        lw.append(dict(
            w_in=w_in_b, w_out=w_out_b,
            norm_ffn1=norm_ffn1[l].reshape(1, -1), norm_mix=norm_mix[l].reshape(1, -1),
            norm_ffn2=norm_ffn2[l].reshape(1, -1),
            s5=_s5_consts(s5_a_re[l], s5_a_im[l], s5_log_dt[l], s5_b_re[l], s5_b_im[l], s5_c_re[l], s5_c_im[l],
                          s5_d[l], s5_w_glu[l], s5_b_glu[l]),
            rwkv=_rwkv_consts(rwkv_mu[l], rwkv_w0[l], rwkv_w2[l], rwkv_a0[l], rwkv_a2[l], rwkv_g2[l],
                              rwkv_k_k[l], rwkv_k_a[l], rwkv_r_k[l], rwkv_ln_w[l], rwkv_ln_b[l]),
            hgrn_lb=lower_bounds[l].reshape(1, -1), hgrn_nw=hgrn_norm_w[l].reshape(1, -1),
        ))

    def zeros_like_state(s):
        return jnp.zeros((depth, nb) + s.shape[2:], F32)

    p_states = tuple(zeros_like_state(s) for s in
                     (state_s5_re, state_s5_im, state_rwkv_shift, state_rwkv_wkv, state_hgrn))
    s_states = (state_s5_re, state_s5_im, state_rwkv_shift, state_rwkv_wkv, state_hgrn)
    ffw = dict(f1g=ffn1_w_gate, f1u=ffn1_w_up, f1d=ffn1_w_down, f2g=ffn2_w_gate, f2u=ffn2_w_up, f2d=ffn2_w_down)
    (y_s, s5re_s, s5im_s, shift_s, wkv_s, hgrn_s), ffb = _run(
        x_sample.reshape(ns * dseq, d), ns, dseq, s_states, lw, ffw, norm_final)
    (y_p, s5re_p, s5im_p, shift_p, wkv_p, hgrn_p), _ = _run(
        x_prompt.reshape(nb * seq, d), nb, seq, p_states, lw, ffb, norm_final)
    return (y_p.reshape(nb, seq, d), y_s.reshape(ns, dseq, d), s5re_p, s5im_p, shift_p, wkv_p, hgrn_p,
            s5re_s, s5im_s, shift_s, wkv_s, hgrn_s)
```

```python
import functools
import math

import numpy as np
import jax
import jax.numpy as jnp
from jax import lax
from jax.experimental import pallas as pl
from jax.experimental.pallas import tpu as pltpu

F32 = jnp.float32
BF16 = jnp.bfloat16

NORM_EPS = 1e-6
RWKV_GN_EPS = 64e-5
HGRN_EPS = 1e-5

D_MODEL = 2048
S5_WIDTH = 512
S5_GROUP = 16
S5_GROUPS = 32
S5_STATE = 64
S5_LANES = S5_GROUPS * S5_STATE
S5_GB = 4
S5_TR = 512
S5_SEG = S5_TR // 8
RWKV_WIDTH = 768
RWKV_HEAD = 64
RWKV_HEADS = 12
RWKV_PAIRS = 6
RWKV_SHIFT_WIDTH = 2560
RWKV_ZBLOCKS = RWKV_SHIFT_WIDTH // S5_WIDTH
RWKV_BLOCK = 16
RWKV_TM = 128
SEG_CHUNK = 256
HGRN_WIDTH = 768
HGRN_HEADS = 6
HGRN_DIM = 128
N_IN = 6144

SUBLANES = 8
LANES = 128
VMEM_LIMIT = 60 * 1024 * 1024


def _cparams(sem):
    return pltpu.CompilerParams(dimension_semantics=sem, vmem_limit_bytes=VMEM_LIMIT)


def _split2(x):
    hi = x.astype(BF16)
    lo = (x - hi.astype(F32)).astype(BF16)
    return hi, lo


def _split3(x):
    p1 = x.astype(BF16)
    r1 = x - p1.astype(F32)
    p2 = r1.astype(BF16)
    p3 = (r1 - p2.astype(F32)).astype(BF16)
    return p1, p2, p3


def _dot(a, b):
    return jnp.dot(a, b, preferred_element_type=F32)


def _dot_nt(a, b):
    return lax.dot_general(a, b, (((1,), (1,)), ((), ())), preferred_element_type=F32)


def _dot_tn(a, b):
    return lax.dot_general(a, b, (((0,), (0,)), ((), ())), preferred_element_type=F32)


def _sigmoid(x):
    return 1.0 / (1.0 + jnp.exp(-x))


def _ffn_kernel(*refs, cast, final):
    x_ref, nw_ref, wg_ref, wu_ref, wd_ref = refs[:5]
    refs = refs[5:]
    if final:
        fnw_ref = refs[0]
        refs = refs[1:]
    if cast:
        o_ref, wgb_ref, wub_ref, wdb_ref, h_ref = refs
    else:
        o_ref, h_ref = refs
    j = pl.program_id(1)

    @pl.when(j == 0)
    def _():
        rows = o_ref.shape[0]
        step = min(rows, 256)
        for r0 in range(0, rows, step):
            rs = slice(r0, r0 + step)
            x = x_ref[rs, :]
            ms = jnp.mean(x * x, axis=-1, keepdims=True)
            h_ref[rs, :] = (x * lax.rsqrt(ms + NORM_EPS) * nw_ref[...]).astype(BF16)
        o_ref[...] = jnp.zeros_like(o_ref)

    wg = wg_ref[...].astype(BF16)
    wu = wu_ref[...].astype(BF16)
    wd = wd_ref[...].astype(BF16)
    if cast:
        wgb_ref[...] = wg
        wub_ref[...] = wu
        wdb_ref[...] = wd
    h = h_ref[...]
    g = _dot(h, wg)
    u = _dot(h, wu)
    o_ref[...] += _dot((g * _sigmoid(g) * u).astype(BF16), wd)

    @pl.when(j == pl.num_programs(1) - 1)
    def _():
        rows = o_ref.shape[0]
        step = min(rows, 256)
        for r0 in range(0, rows, step):
            rs = slice(r0, r0 + step)
            o = x_ref[rs, :] + 0.5 * o_ref[rs, :]
            if final:
                o = o * lax.rsqrt(jnp.mean(o * o, axis=-1, keepdims=True) + NORM_EPS) * fnw_ref[...]
            o_ref[rs, :] = o


def _ffn(x, nw, wg, wu, wd, l, cast=False, fnw=None, tm=1024):
    m, d = x.shape
    ff = wg.shape[2]
    tf = 256 if cast else 512
    final = fnw is not None
    if final and not cast:
        tm = tm // 2
    out_specs = [pl.BlockSpec((tm, d), lambda i, j: (i, 0))]
    out_shape = [jax.ShapeDtypeStruct((m, d), F32)]
    if cast:
        assert m == tm
        out_specs += [pl.BlockSpec((d, tf), lambda i, j: (0, j)), pl.BlockSpec((d, tf), lambda i, j: (0, j)),
                      pl.BlockSpec((tf, d), lambda i, j: (j, 0))]
        out_shape += [jax.ShapeDtypeStruct((d, ff), BF16), jax.ShapeDtypeStruct((d, ff), BF16),
                      jax.ShapeDtypeStruct((ff, d), BF16)]
    vec = pl.BlockSpec((1, d), lambda i, j: (0, 0))
    outs = pl.pallas_call(
        functools.partial(_ffn_kernel, cast=cast, final=final),
        grid=(m // tm, ff // tf),
        in_specs=[
            pl.BlockSpec((tm, d), lambda i, j: (i, 0)),
            vec,
            pl.BlockSpec((None, d, tf), lambda i, j: (l, 0, j)),
            pl.BlockSpec((None, d, tf), lambda i, j: (l, 0, j)),
            pl.BlockSpec((None, tf, d), lambda i, j: (l, j, 0)),
        ] + ([vec] if final else []),
        out_specs=out_specs,
        out_shape=out_shape,
        scratch_shapes=[pltpu.VMEM((tm, d), BF16)],
        compiler_params=_cparams(("parallel", "arbitrary")),
        name="ffn_cast" if cast else "ffn",
    )(x, nw, wg, wu, wd, *((fnw,) if final else ()))
    return outs if cast else outs[0]


def _inproj_kernel(x_ref, nw_ref, w_ref, o_ref, h_ref):
    @pl.when(pl.program_id(1) == 0)
    def _():
        x = x_ref[...]
        ms = jnp.mean(x * x, axis=-1, keepdims=True)
        h_ref[...] = (x * lax.rsqrt(ms + NORM_EPS) * nw_ref[...]).astype(BF16)

    o_ref[...] = _dot(h_ref[...], w_ref[...])


def _inproj(x, nw, w, l, tm=1024, tn=1024):
    m, d = x.shape
    n = w.shape[2]
    return pl.pallas_call(
        _inproj_kernel,
        grid=(m // tm, n // tn),
        in_specs=[
            pl.BlockSpec((tm, d), lambda i, j: (i, 0)),
            pl.BlockSpec((1, d), lambda i, j: (0, 0)),
            pl.BlockSpec((None, d, tn), lambda i, j: (l, 0, j)),
        ],
        out_specs=pl.BlockSpec((tm, tn), lambda i, j: (i, j)),
        out_shape=jax.ShapeDtypeStruct((m, n), F32),
        scratch_shapes=[pltpu.VMEM((tm, d), BF16)],
        compiler_params=_cparams(("parallel", "arbitrary")),
        name="inproj",
    )(x, nw, w)


def _outproj_kernel(x_ref, a_ref, y_ref, bonus_ref, g_ref, c_ref, w_ref, lnw_ref, lnb_ref, e2_ref, o_ref):
    y = y_ref[...]
    e2 = e2_ref[...]
    mean = _seg_sum(y, e2) * (1.0 / RWKV_HEAD)
    yc = y - mean
    var = _seg_sum(yc * yc, e2) * (1.0 / RWKV_HEAD)
    b = (yc * lax.rsqrt(var + RWKV_GN_EPS) * lnw_ref[...] + lnb_ref[...] + bonus_ref[...]) * g_ref[...]
    o1 = S5_WIDTH
    o2 = S5_WIDTH + RWKV_WIDTH
    acc = _dot(a_ref[...].astype(BF16), w_ref[0:o1, :])
    acc += _dot(b.astype(BF16), w_ref[o1:o2, :])
    acc += _dot(c_ref[...].astype(BF16), w_ref[o2:, :])
    o_ref[...] = x_ref[...] + acc


def _outproj(x, ya, y_rw, bonus, g, yc, w_all, l, cs, tm=512):
    m, d = x.shape
    row = lambda i: (i, 0)
    whole = lambda i: (0, 0)
    wide = lambda a: pl.BlockSpec((tm, a.shape[1]), row)
    vec = pl.BlockSpec((1, RWKV_WIDTH), whole)
    return pl.pallas_call(
        _outproj_kernel,
        grid=(m // tm,),
        in_specs=[
            pl.BlockSpec((tm, d), row), wide(ya), wide(y_rw), wide(bonus), wide(g), wide(yc),
            pl.BlockSpec((None,) + w_all.shape[1:], lambda i: (l, 0, 0)),
            vec, vec, pl.BlockSpec((2 * SEG_CHUNK, SEG_CHUNK), whole),
        ],
        out_specs=pl.BlockSpec((tm, d), row),
        out_shape=jax.ShapeDtypeStruct((m, d), F32),
        compiler_params=_cparams(("parallel",)),
        name="outproj",
    )(x, ya, y_rw, bonus, g, yc, w_all, cs["ln_w"], cs["ln_b"], cs["e2w"])


def _s5_kernel(u_ref, x0r_ref, x0i_ref, b3r_ref, b3i_ref, cr_ref, ci_ref, cst_ref, d_ref, wglu_ref,
               bglu_ref, y_ref, xlr_ref, xli_ref, xr_ref, xi_ref, cr_s, ci_s, pm_ref, *, tr, chained):
    c = pl.program_id(1)
    nlt = S5_LANES // LANES
    per_gb = nlt // S5_GB

    def load_tile(ref, rows):
        return jnp.concatenate([ref[k, rows, :] for k in range(nlt)], axis=1)

    def store_tile(ref, rows, val):
        for k in range(nlt):
            ref[k, rows, :] = val[:, k * LANES:(k + 1) * LANES]

    u = u_ref[...]
    nut = S5_WIDTH // LANES
    seg = tr // SUBLANES
    if chained:
        for k in range(nut):
            pm_ref[k] = u[:, k * LANES:(k + 1) * LANES]
        u = jnp.concatenate(
            [jnp.concatenate([pm_ref[k, pl.ds(tau, SUBLANES, stride=seg), :] for tau in range(seg)], axis=0)
             for k in range(nut)], axis=1)
    uh, ul = _split2(u)
    for gb in range(S5_GB):
        sl = slice(gb * LANES, (gb + 1) * LANES)
        if chained:
            br = _dot(uh[:, sl], b3r_ref[gb, 0:LANES, :])
            bi = _dot(uh[:, sl], b3i_ref[gb, 0:LANES, :])
        else:
            lhs = jnp.concatenate([uh[:, sl], ul[:, sl], uh[:, sl]], axis=1)
            br = _dot(lhs, b3r_ref[gb])
            bi = _dot(lhs, b3i_ref[gb])
        for k in range(per_gb):
            xr_ref[gb * per_gb + k] = br[:, k * LANES:(k + 1) * LANES]
            xi_ref[gb * per_gb + k] = bi[:, k * LANES:(k + 1) * LANES]

    if chained:
        @pl.when(c == 0)
        def _():
            cr_s[...] = x0r_ref[0]
            ci_s[...] = x0i_ref[0]

    a1r, a1i, a2r, a2i, a4r, a4i, pr, pi = [cst_ref[i] for i in range(8)]

    def tile_scan(xr, xi, x0r, x0i, lv1, lv2, lv4, pw):
        for k, (ar, ai) in ((1, lv1), (2, lv2), (4, lv4)):
            sr = pltpu.roll(xr, k, 0)
            si = pltpu.roll(xi, k, 0)
            xr, xi = xr + ar * sr - ai * si, xi + ar * si + ai * sr
        x0r = jnp.broadcast_to(x0r, xr.shape)
        x0i = jnp.broadcast_to(x0i, xr.shape)
        return xr + pw[0] * x0r - pw[1] * x0i, xi + pw[0] * x0i + pw[1] * x0r

    if chained:
        seg = tr // SUBLANES
        b1r, b1i, b2r, b2i, b4r, b4i, qr64, qi64 = [cst_ref[8 + i] for i in range(8)]
        lt = lambda x, k: x[:, k * LANES:(k + 1) * LANES]
        ar = [jnp.broadcast_to(lt(pr, k)[0:1, :], (SUBLANES, LANES)) for k in range(nlt)]
        ai = [jnp.broadcast_to(lt(pi, k)[0:1, :], (SUBLANES, LANES)) for k in range(nlt)]

        def sweep1(tau, carry):
            rows = pl.ds(pl.multiple_of(tau * SUBLANES, SUBLANES), SUBLANES)
            out = []
            for k in range(nlt):
                xr, xi = carry[2 * k], carry[2 * k + 1]
                xr, xi = (ar[k] * xr - ai[k] * xi + xr_ref[k, rows, :], ar[k] * xi + ai[k] * xr + xi_ref[k, rows, :])
                xr_ref[k, rows, :] = xr
                xi_ref[k, rows, :] = xi
                out += [xr, xi]
            return tuple(out)

        zero = jnp.zeros((SUBLANES, LANES), F32)
        ends = lax.fori_loop(0, seg, sweep1, (zero,) * (2 * nlt))
        er = jnp.concatenate(ends[0::2], axis=1)
        ei = jnp.concatenate(ends[1::2], axis=1)
        x0r = cr_s[...]
        x0i = ci_s[...]
        yr, yi = tile_scan(er, ei, x0r, x0i, (b1r, b1i), (b2r, b2i), (b4r, b4i), (qr64, qi64))
        cr_s[...] = yr[SUBLANES - 1:SUBLANES, :]
        ci_s[...] = yi[SUBLANES - 1:SUBLANES, :]
        rid = lax.broadcasted_iota(jnp.int32, pr.shape, 0)
        sr0 = jnp.where(rid == 0, jnp.broadcast_to(x0r, pr.shape), pltpu.roll(yr, 1, 0))
        si0 = jnp.where(rid == 0, jnp.broadcast_to(x0i, pr.shape), pltpu.roll(yi, 1, 0))

        def sweep2(tau, carry):
            rows = pl.ds(pl.multiple_of(tau * SUBLANES, SUBLANES), SUBLANES)
            out = []
            for k in range(nlt):
                qr, qi = carry[2 * k], carry[2 * k + 1]
                qr, qi = ar[k] * qr - ai[k] * qi, ar[k] * qi + ai[k] * qr
                xr_ref[k, rows, :] = xr_ref[k, rows, :] + qr
                xi_ref[k, rows, :] = xi_ref[k, rows, :] + qi
                out += [qr, qi]
            return tuple(out)

        start = []
        for k in range(nlt):
            start += [lt(sr0, k), lt(si0, k)]
        lax.fori_loop(0, seg, sweep2, tuple(start))

        @pl.when(c == pl.num_programs(1) - 1)
        def _():
            xlr_ref[0] = cr_s[...]
            xli_ref[0] = ci_s[...]
    else:
        def tile(j, carry):
            rows = pl.ds(pl.multiple_of(j * SUBLANES, SUBLANES), SUBLANES)
            xr, xi = tile_scan(load_tile(xr_ref, rows), load_tile(xi_ref, rows),
                               x0r_ref[pl.ds(j, 1), :], x0i_ref[pl.ds(j, 1), :],
                               (a1r, a1i), (a2r, a2i), (a4r, a4i), (pr, pi))
            store_tile(xr_ref, rows, xr)
            store_tile(xi_ref, rows, xi)
            xlr_ref[pl.ds(j, 1), :] = xr[SUBLANES - 1:SUBLANES, :]
            xli_ref[pl.ds(j, 1), :] = xi[SUBLANES - 1:SUBLANES, :]
            return carry

        lax.fori_loop(0, tr // SUBLANES, tile, 0)

    ys = []
    for gb in range(S5_GB):
        xr = jnp.concatenate([xr_ref[gb * per_gb + k] for k in range(per_gb)], axis=1)
        xi = jnp.concatenate([xi_ref[gb * per_gb + k] for k in range(per_gb)], axis=1)
        ys.append(_dot(xr.astype(BF16), cr_ref[gb]) + _dot(xi.astype(BF16), ci_ref[gb]))
    y = jnp.concatenate(ys, axis=1) + d_ref[...] * u
    y = 0.5 * y * (1.0 + jnp.tanh(math.sqrt(2.0 / math.pi) * (y + 0.044715 * (y * y * y))))
    z = _dot(y.astype(BF16), wglu_ref[...]) + bglu_ref[...]
    y = y * _sigmoid(z)
    if chained:
        for k in range(nut):
            pm_ref[k] = y[:, k * LANES:(k + 1) * LANES]
        for m in range(tr // SUBLANES):
            start = (m % SUBLANES) * SUBLANES * SUBLANES + m // SUBLANES
            for k in range(nut):
                y_ref[m * SUBLANES:(m + 1) * SUBLANES, k * LANES:(k + 1) * LANES] = (
                    pm_ref[k, pl.ds(start, SUBLANES, stride=SUBLANES), :])
    else:
        y_ref[...] = y


def _s5(p3, x0r, x0i, l, cs, nseq, t, tr=S5_TR):
    rows = nseq * t
    depth = x0r.shape[0]
    chained = t > SUBLANES
    if chained:
        grid = (nseq, t // tr)
        row_map = lambda n, c: (n * (t // tr) + c, 0)
        out_map = lambda n, c: (n * (t // tr) + c, 0)
        x0r = x0r.reshape(depth, nseq, 1, S5_LANES)
        x0i = x0i.reshape(depth, nseq, 1, S5_LANES)
        in_st = pl.BlockSpec((None, 1, 1, S5_LANES), lambda n, c: (l, n, 0, 0))
        st_spec = pl.BlockSpec((1, 1, S5_LANES), lambda n, c: (n, 0, 0))
        st_shape = jax.ShapeDtypeStruct((nseq, 1, S5_LANES), F32)
    else:
        assert t == SUBLANES
        grid = (rows // tr, 1)
        row_map = lambda n, c: (n, 0)
        out_map = lambda n, c: (n, 0)
        in_st = pl.BlockSpec((None, tr // SUBLANES, S5_LANES), lambda n, c: (l, n, 0))
        st_spec = pl.BlockSpec((tr // SUBLANES, S5_LANES), lambda n, c: (n, 0))
        st_shape = jax.ShapeDtypeStruct((nseq, S5_LANES), F32)
    names = ("b3r", "b3i", "cr", "ci", "scan", "d", "wglu", "bglu")
    layer = lambda a: pl.BlockSpec((None,) + a.shape[1:], lambda n, c: (l,) + (0,) * (a.ndim - 1))
    y, xlr, xli = pl.pallas_call(
        functools.partial(_s5_kernel, tr=tr, chained=chained),
        grid=grid,
        in_specs=[pl.BlockSpec((tr, S5_WIDTH), row_map), in_st, in_st] + [layer(cs[k]) for k in names],
        out_specs=[pl.BlockSpec((tr, S5_WIDTH), out_map), st_spec, st_spec],
        out_shape=[jax.ShapeDtypeStruct((rows, S5_WIDTH), F32), st_shape, st_shape],
        scratch_shapes=[pltpu.VMEM((S5_LANES // LANES, tr, LANES), F32), pltpu.VMEM((S5_LANES // LANES, tr, LANES), F32),
                        pltpu.VMEM((1, S5_LANES), F32), pltpu.VMEM((1, S5_LANES), F32),
                        pltpu.VMEM((S5_WIDTH // LANES, tr, LANES), F32)],
        compiler_params=_cparams(("parallel", "arbitrary")),
        name="s5_chain" if chained else "s5_tile",
    )(p3, x0r, x0i, *(cs[k] for k in names))
    return y, xlr.reshape(nseq, S5_GROUPS, S5_STATE), xli.reshape(nseq, S5_GROUPS, S5_STATE)


def _s5_consts(a_re, a_im, log_dt, b_re, b_im, c_re, c_im, d, w_glu, b_glu):
    g, p, h = S5_GROUPS, S5_STATE, S5_GROUP
    dt = jnp.exp(log_dt)[:, None]
    mag = jnp.exp(a_re * dt)
    abr = mag * jnp.cos(a_im * dt)
    abi = mag * jnp.sin(a_im * dt)
    den = a_re * a_re + a_im * a_im
    qr = ((abr - 1.0) * a_re + abi * a_im) / den
    qi = (abi * a_re - (abr - 1.0) * a_im) / den
    bbr = qr[..., None] * b_re - qi[..., None] * b_im
    bbi = qr[..., None] * b_im + qi[..., None] * b_re
    row = jnp.arange(SUBLANES)[:, None]

    def scan_tables(mr, mi):
        pr, pi = [mr], [mi]
        for _ in range(SUBLANES - 1):
            pr, pi = pr + [pr[-1] * mr - pi[-1] * mi], pi + [pr[-1] * mi + pi[-1] * mr]
        pr = jnp.stack(pr).reshape(SUBLANES, g * p)
        pi = jnp.stack(pi).reshape(SUBLANES, g * p)

        def lvl(k):
            return [jnp.where(row >= k, pr[k - 1][None, :], 0.0), jnp.where(row >= k, pi[k - 1][None, :], 0.0)]

        return lvl(1) + lvl(2) + lvl(4) + [pr, pi]

    sr, si = abr, abi
    for _ in range(S5_SEG.bit_length() - 1):
        sr, si = sr * sr - si * si, 2.0 * sr * si
    scan = jnp.stack(scan_tables(abr, abi) + scan_tables(sr, si)).astype(F32)
    eye = jnp.eye(SUBLANES, dtype=F32)

    def in_blocks(x):
        xx = x.reshape(S5_GB, SUBLANES, p, h).transpose(0, 1, 3, 2)
        blk = (xx[:, :, :, None, :] * eye[None, :, None, :, None]).reshape(S5_GB, SUBLANES * h, SUBLANES * p)
        hi = blk.astype(BF16)
        lo = (blk - hi.astype(F32)).astype(BF16)
        return jnp.concatenate([hi, hi, lo], axis=1)

    def out_blocks(x):
        xx = x.reshape(S5_GB, SUBLANES, h, p).transpose(0, 1, 3, 2)
        return (xx[:, :, :, None, :] * eye[None, :, None, :, None]).reshape(
            S5_GB, SUBLANES * p, SUBLANES * h).astype(BF16)

    return dict(b3r=in_blocks(bbr), b3i=in_blocks(bbi),
                cr=out_blocks(c_re), ci=out_blocks(-c_im), scan=scan,
                d=d.reshape(1, -1), wglu=w_glu.astype(BF16), bglu=b_glu.reshape(1, -1))


def _seg_sum(x, e2):
    hi, lo = _split2(x)
    cw = e2.shape[1]
    return jnp.concatenate(
        [_dot(jnp.concatenate([hi[:, c:c + cw], lo[:, c:c + cw]], axis=1), e2) for c in range(0, x.shape[1], cw)],
        axis=1)


def _rwkv_prep_kernel(*refs, nlev, chained, tiles_per_seq):
    nz = RWKV_ZBLOCKS
    z = jnp.concatenate([r[...] for r in refs[:nz]], axis=1)
    if chained:
        prev8 = jnp.concatenate([r[...] for r in refs[nz:2 * nz]], axis=1)
        sh_ref = refs[2 * nz]
        refs = refs[2 * nz + 1:]
    else:
        zp_ref = refs[nz]
        refs = refs[nz + 1:]
    (mu_ref, w0_ref, a0_ref, w2_ref, a2_ref, g2_ref, kk_ref, ka_ref, rk_ref, e2_ref, tril_ref, mk_ref, hm_ref,
     ap_o, rp_o, bt_o, kp_o, cc_o, y0_o, v_o, gm_o, g_o, bonus_o) = refs
    tm = RWKV_TM
    if chained:
        first = (pl.program_id(0) % tiles_per_seq) == 0
        prev_row = jnp.where(first, sh_ref[0], prev8[SUBLANES - 1:SUBLANES, :])
        rid = lax.broadcasted_iota(jnp.int32, z.shape, 0)
        zp = jnp.where(rid == 0, prev_row, pltpu.roll(z, 1, 0))
    else:
        zp = zp_ref[...]
    zm = z + (zp - z) * mu_ref[...]
    w = RWKV_WIDTH
    r = zm[:, 0:w]
    k = zm[:, w:2 * w]
    v = zm[:, 2 * w:3 * w]
    wa = zm[:, 3 * w:3 * w + LANES]
    gi = zm[:, 3 * w + LANES:3 * w + 2 * LANES]
    e2 = e2_ref[...]
    wl = w0_ref[...] + _dot(jnp.tanh(wa).astype(BF16), w2_ref[...])
    sp = jnp.maximum(-wl, 0.0) + jnp.log(1.0 + jnp.exp(-jnp.abs(wl)))
    ld = -jnp.exp(-sp - 0.5)
    a = _sigmoid(a0_ref[...] + _dot(wa.astype(BF16), a2_ref[...]))
    g_o[...] = _dot(_sigmoid(gi).astype(BF16), g2_ref[...])
    kk = k * kk_ref[...]
    kk = kk * lax.rsqrt(jnp.maximum(_seg_sum(kk * kk, e2), 1e-24))
    k2 = k * (1.0 + (a - 1.0) * ka_ref[...])
    b = kk * a
    bonus_o[...] = _seg_sum(r * k2 * rk_ref[...], e2) * v
    v_o[...] = v

    p1, p2, p3 = _split3(ld)
    cum = _dot(tril_ref[...], jnp.concatenate([p1, p2, p3], axis=0))
    gam = jnp.exp(cum)
    ginv = jnp.exp(-cum)
    gm_o[...] = gam
    alpha = kk * jnp.exp(cum - ld)
    beta = b * ginv
    kap = k2 * ginv
    rho = r * gam
    bt_o[...] = beta
    kp_o[...] = kap

    strict = mk_ref[0]
    incl = mk_ref[1]
    eye = mk_ref[2]
    heads = [(j, e) for j in range(RWKV_PAIRS) for e in range(2)]
    nh = len(heads)

    def lanes(j):
        return slice(j * LANES, (j + 1) * LANES)

    al = [alpha[:, lanes(j)] * hm_ref[e] for j, e in heads]
    rh = [rho[:, lanes(j)] * hm_ref[e] for j, e in heads]
    vh = [(v[:, lanes(j)] * hm_ref[e]).astype(BF16) for j, e in heads]
    yk = [jnp.concatenate([beta[:, lanes(j)], kap[:, lanes(j)]], axis=0).astype(BF16) for j in range(RWKV_PAIRS)]
    wu = [_dot_nt(jnp.concatenate(_split2(al[i]), axis=1), jnp.concatenate([yk[j], yk[j]], axis=1))
          for i, (j, e) in enumerate(heads)]
    wl = [_dot_nt(rh[i].astype(BF16), yk[j]) for i, (j, e) in enumerate(heads)]
    mm = [wu[i][:, :tm] * strict for i in range(nh)]
    nn = [wu[i][:, tm:] * strict for i in range(nh)]
    pp = [(wl[i][:, :tm] * incl).astype(BF16) for i in range(nh)]
    qq = [(wl[i][:, tm:] * incl).astype(BF16) for i in range(nh)]
    tinv = [eye - mm[i] * mk_ref[3] for i in range(nh)]
    for lv in range(1, nlev):
        th = [tinv[i].astype(BF16) for i in range(nh)]
        tmp = [_dot(th[i], (mm[i] * mk_ref[3 + lv]).astype(BF16)) for i in range(nh)]
        tinv = [tinv[i] - _dot(tmp[i].astype(BF16), th[i]) for i in range(nh)]
    th = [tinv[i].astype(BF16) for i in range(nh)]
    nv = [_dot(nn[i].astype(BF16), vh[i]) for i in range(nh)]
    qv = [_dot(qq[i], vh[i]) for i in range(nh)]
    tac = [_dot(th[i], jnp.concatenate([al[i], nv[i]], axis=1).astype(BF16)) for i in range(nh)]
    pac = [_dot(pp[i], tac[i].astype(BF16)) for i in range(nh)]
    ap = [tac[i][:, :LANES] for i in range(nh)]
    cc = [tac[i][:, LANES:] for i in range(nh)]
    rp = [rh[i] - pac[i][:, :LANES] for i in range(nh)]
    y0 = [qv[i] - pac[i][:, LANES:] for i in range(nh)]
    for j in range(RWKV_PAIRS):
        ap_o[:, lanes(j)] = ap[2 * j] + ap[2 * j + 1]
        rp_o[:, lanes(j)] = rp[2 * j] + rp[2 * j + 1]
        cc_o[:, lanes(j)] = cc[2 * j] + cc[2 * j + 1]
        y0_o[:, lanes(j)] = y0[2 * j] + y0[2 * j + 1]


def _rwkv_tables(tm, blk):
    t = np.arange(tm)
    same = (t[:, None] // blk) == (t[None, :] // blk)
    tril = (same & (t[None, :] <= t[:, None])).astype(np.float32)
    masks = [same & (t[None, :] < t[:, None]), same & (t[None, :] <= t[:, None]), np.eye(tm, dtype=bool)]
    n = 2
    while n <= blk:
        m = n // 2
        masks.append(((t[:, None] // n) == (t[None, :] // n)) & ((t[:, None] % n) >= m) & ((t[None, :] % n) < m))
        n *= 2
    nlev = len(masks) - 3
    hm = np.stack([(np.arange(LANES) < RWKV_HEAD), (np.arange(LANES) >= RWKV_HEAD)]).astype(np.float32)
    return (jnp.asarray(np.concatenate([tril, tril, tril], axis=1), BF16),
            jnp.asarray(np.stack(masks).astype(np.float32)), jnp.asarray(hm.reshape(2, 1, LANES)), nlev)


def _rwkv_prep(p3, prev, cs, nseq, t):
    m = nseq * t
    tm = RWKV_TM
    blk = min(RWKV_BLOCK, t)
    chained = t >= tm
    tril3, masks, hm, nlev = _rwkv_tables(tm, blk)
    w = RWKV_WIDTH
    row = lambda i: (i, 0)
    whole = lambda i: (0, 0)
    whole3 = lambda i: (0, 0, 0)
    vec = pl.BlockSpec((1, w), whole)
    zspecs = [pl.BlockSpec((tm, S5_WIDTH), lambda i, c=c: (i, 1 + c)) for c in range(RWKV_ZBLOCKS)]
    if chained:
        tps = t // tm
        lead = zspecs + [pl.BlockSpec((SUBLANES, S5_WIDTH),
                                      lambda i, c=c: (jnp.maximum(i * (tm // SUBLANES) - 1, 0), 1 + c))
                         for c in range(RWKV_ZBLOCKS)]
        lead.append(pl.BlockSpec((1, 1, RWKV_SHIFT_WIDTH), lambda i: (i // tps, 0, 0)))
        args = (p3,) * (2 * RWKV_ZBLOCKS) + (prev,)
    else:
        tps = 1
        lead = zspecs + [pl.BlockSpec((tm, RWKV_SHIFT_WIDTH), row)]
        args = (p3,) * RWKV_ZBLOCKS + (prev,)
    return pl.pallas_call(
        functools.partial(_rwkv_prep_kernel, nlev=nlev, chained=chained, tiles_per_seq=tps),
        grid=(m // tm,),
        in_specs=lead + [
            pl.BlockSpec((1, RWKV_SHIFT_WIDTH), whole),
            vec, vec,
            pl.BlockSpec((LANES, w), whole), pl.BlockSpec((LANES, w), whole), pl.BlockSpec((LANES, w), whole),
            vec, vec, vec,
            pl.BlockSpec((2 * SEG_CHUNK, SEG_CHUNK), whole),
            pl.BlockSpec(tril3.shape, whole), pl.BlockSpec(masks.shape, whole3), pl.BlockSpec(hm.shape, whole3),
        ],
        out_specs=[pl.BlockSpec((tm, w), row)] * 10,
        out_shape=[jax.ShapeDtypeStruct((m, w), F32)] * 10,
        compiler_params=_cparams(("parallel",)),
        name="rwkv_prep",
    )(*args, cs["mu"], cs["w0"], cs["a0"], cs["w2"], cs["a2"], cs["g2"], cs["k_k"], cs["k_a"], cs["r_k"],
      cs["e2w"], tril3, masks, hm)


def _rwkv_rec_kernel(*refs, nbp, tc, blk, aliased):
    ap_ref, rp_ref, bt_ref, kp_ref, cc_ref, y0_ref, v_ref, gm_ref, s0_ref, bd_ref = refs[:10]
    y_ref, so_ref, s_ref = refs[10 + (1 if aliased else 0):]
    c = pl.program_id(1)
    hd = RWKV_HEAD
    pairs = [(n, j) for n in range(nbp) for j in range(RWKV_PAIRS)]

    @pl.when(c == 0)
    def _():
        zero = jnp.zeros((hd, hd), F32)
        for idx, (n, j) in enumerate(pairs):
            top = jnp.concatenate([s0_ref[n, 2 * j], zero], axis=1)
            bot = jnp.concatenate([zero, s0_ref[n, 2 * j + 1]], axis=1)
            s_ref[idx] = jnp.concatenate([top, bot], axis=0)

    bd = bd_ref[...]

    def block(bi, carry):
        rows = pl.ds(pl.multiple_of(bi * blk, blk), blk)
        gmats = []
        for idx, (n, j) in enumerate(pairs):
            ls = slice(j * LANES, (j + 1) * LANES)
            sh, sl = _split2(s_ref[idx])
            xh = jnp.concatenate([ap_ref[n, rows, ls], rp_ref[n, rows, ls]], axis=0).astype(BF16)
            gmats.append(_dot_nt(jnp.concatenate([xh, xh], axis=1), jnp.concatenate([sh, sl], axis=1)))
        upds = []
        for idx, (n, j) in enumerate(pairs):
            ls = slice(j * LANES, (j + 1) * LANES)
            gmat = gmats[idx]
            e = -gmat[:blk] - cc_ref[n, rows, ls]
            y_ref[n, rows, ls] = gmat[blk:] + y0_ref[n, rows, ls]
            evh, evl = _split2(jnp.concatenate([e, v_ref[n, rows, ls]], axis=0))
            bkh, bkl = _split2(jnp.concatenate([bt_ref[n, rows, ls], kp_ref[n, rows, ls]], axis=0))
            upds.append(_dot_tn(jnp.concatenate([evh, evh, evl], axis=0), jnp.concatenate([bkh, bkl, bkh], axis=0)))
        for idx, (n, j) in enumerate(pairs):
            ls = slice(j * LANES, (j + 1) * LANES)
            gl = gm_ref[n, rows, ls][blk - 1:blk, :]
            s_ref[idx] = (s_ref[idx] + upds[idx] * bd) * gl
        return carry

    lax.fori_loop(0, tc // blk, block, 0)

    @pl.when(c == pl.num_programs(1) - 1)
    def _():
        for idx, (n, j) in enumerate(pairs):
            s = s_ref[idx]
            so_ref[n, 2 * j] = s[:hd, :hd]
            so_ref[n, 2 * j + 1] = s[hd:, hd:]


def _rwkv_rec(arrs, s0_all, l, so_prev, cs, nseq, t, nbp=4, tc=128):
    w = RWKV_WIDTH
    tc = min(tc, t)
    blk = min(RWKV_BLOCK, t)
    depth = s0_all.shape[0]
    seq_map = lambda i, c: (i, c, 0)
    st_map = lambda i, c: (l, i, 0, 0, 0)
    blkspec = pl.BlockSpec((nbp, tc, w), seq_map)
    st = pl.BlockSpec((None, nbp, RWKV_HEADS, RWKV_HEAD, RWKV_HEAD), st_map)
    aliased = so_prev is not None
    in_specs = [blkspec] * 8 + [st, pl.BlockSpec((LANES, LANES), lambda i, c: (0, 0))]
    args = list(arrs) + [s0_all, cs["bd"]]
    aliases = {}
    if aliased:
        in_specs.append(pl.BlockSpec(memory_space=pl.ANY))
        args.append(so_prev)
        aliases = {10: 1}
    return pl.pallas_call(
        functools.partial(_rwkv_rec_kernel, nbp=nbp, tc=tc, blk=blk, aliased=aliased),
        grid=(nseq // nbp, t // tc),
        in_specs=in_specs,
        out_specs=[blkspec, st],
        out_shape=[jax.ShapeDtypeStruct((nseq, t, w), F32),
                   jax.ShapeDtypeStruct((depth, nseq, RWKV_HEADS, RWKV_HEAD, RWKV_HEAD), F32)],
        scratch_shapes=[pltpu.VMEM((nbp * RWKV_PAIRS, LANES, LANES), F32)],
        input_output_aliases=aliases,
        compiler_params=_cparams(("parallel", "arbitrary")),
        name="rwkv_rec",
    )(*args)


def _rwkv_consts(mu, w0, w2, a0, a2, g2, k_k, k_a, r_k, ln_w, ln_b):
    w = RWKV_WIDTH
    seg = np.arange(SEG_CHUNK) // RWKV_HEAD
    e_w = (seg[:, None] == seg[None, :]).astype(np.float32)
    seg = np.arange(LANES) // RWKV_HEAD
    bd = (seg[:, None] == seg[None, :]).astype(np.float32)
    z64 = jnp.zeros((RWKV_HEAD, w), F32)
    return dict(
        mu=mu.reshape(1, -1), w0=w0.reshape(1, -1), a0=a0.reshape(1, -1),
        w2=jnp.concatenate([w2, z64], axis=0).astype(BF16),
        a2=jnp.concatenate([z64, a2], axis=0).astype(BF16),
        g2=g2.astype(BF16), k_k=k_k.reshape(1, -1), k_a=k_a.reshape(1, -1), r_k=r_k.reshape(1, -1),
        ln_w=ln_w.reshape(1, -1), ln_b=ln_b.reshape(1, -1),
        e2w=jnp.asarray(np.concatenate([e_w, e_w], axis=0), BF16),
        bd=jnp.asarray(bd, F32))


def _rwkv(p3, shift_l, wkv_all, l, so_prev, cs, nseq, t):
    if t >= RWKV_TM:
        prev = shift_l.reshape(nseq, 1, RWKV_SHIFT_WIDTH)
    else:
        z3 = p3.reshape(nseq, t, N_IN)[:, :, S5_WIDTH:S5_WIDTH + RWKV_SHIFT_WIDTH]
        prev = jnp.concatenate([shift_l[:, None, :], z3[:, :-1]], axis=1).reshape(nseq * t, RWKV_SHIFT_WIDTH)
    outs = _rwkv_prep(p3, prev, cs, nseq, t)
    arrs = [a.reshape(nseq, t, RWKV_WIDTH) for a in outs[:8]]
    y, so = _rwkv_rec(arrs, wkv_all, l, so_prev, cs, nseq, t)
    return (y.reshape(nseq * t, RWKV_WIDTH), outs[9], outs[8]), so


def _hgrn_tables(c):
    t = np.arange(c)
    j = np.arange(c)
    mats = [(j[None, :] <= t[:, None]), (j[None, :] > t[:, None])]
    masks = []
    n = 2
    while n <= c:
        m = n // 2
        bs = (t // n) * n
        hi_half = (t % n) >= m
        mats.append((hi_half[:, None] & (j[None, :] >= (bs + m)[:, None]) & (j[None, :] <= t[:, None]))
                    | ((~hi_half)[:, None] & (j[None, :] > t[:, None]) & (j[None, :] <= (bs + m - 1)[:, None])))
        masks.append(((t[:, None] // n) == (t[None, :] // n)) & hi_half[:, None] & (~hi_half)[None, :])
        n *= 2
    w = np.concatenate(mats, axis=0).astype(np.float32)
    w3 = np.concatenate([w, w, w], axis=1)
    return jnp.asarray(w3, BF16), jnp.asarray(np.stack(masks).astype(np.float32)), len(masks)


def _hgrn_kernel(*refs, rb, c, nlev, chained, aliased, sub):
    q_ref, f_ref, i_ref, g_ref, s0_ref, lb_ref, nw_ref, w3_ref, mk_ref = refs[:9]
    o_ref, so_ref, st_ref = refs[9 + (1 if aliased else 0):]
    cidx = pl.program_id(1)
    hdim = HGRN_DIM

    if chained:
        @pl.when(cidx == 0)
        def _():
            for h in range(HGRN_HEADS):
                st_ref[h] = s0_ref[0, h].T

    w3 = w3_ref[...]
    row_i = lax.broadcasted_iota(jnp.int32, (c, c), 0)
    col_i = lax.broadcasted_iota(jnp.int32, (c, c), 1)
    eye = (row_i == col_i).astype(F32)

    heads = range(HGRN_HEADS)

    def lanes(h):
        return slice(h * hdim, (h + 1) * hdim)

    def block(it, carry):
        rows_s = [pl.ds(pl.multiple_of((it * sub + s) * c, c), c) for s in range(sub)]
        items = [(s, h) for s in range(sub) for h in heads]
        fg = {k: lb_ref[:, lanes(k[1])] + (1.0 - lb_ref[:, lanes(k[1])]) * _sigmoid(f_ref[rows_s[k[0]], lanes(k[1])])
              for k in items}
        kk = {k: 1.0 - fg[k] for k in items}
        qs = {}
        for k in items:
            q = q_ref[rows_s[k[0]], lanes(k[1])]
            qs[k] = q * _sigmoid(q)
        ex = {k: jnp.exp(_dot(w3, jnp.concatenate(_split3(jnp.log(fg[k])), axis=0))) for k in items}
        vb = {k: i_ref[rows_s[k[0]], lanes(k[1])].astype(BF16) for k in items}
        lev = {(k, lv): _dot_nt((qs[k] * ex[k][(2 + lv) * c:(3 + lv) * c]).astype(BF16),
                                (kk[k] * ex[k][(2 + lv) * c:(3 + lv) * c]).astype(BF16))
               for k in items for lv in range(nlev)}
        att = {}
        for k in items:
            a = eye * jnp.sum(qs[k] * kk[k], axis=-1, keepdims=True)
            for lv in range(nlev):
                a = a + jnp.where(mk_ref[lv] > 0.0, lev[(k, lv)], 0.0)
            att[k] = a.astype(BF16)
        intra = {k: _dot(att[k], vb[k]) for k in items}
        kv = {k: _dot_tn(vb[k], (kk[k] * ex[k][c:2 * c]).astype(BF16)) for k in items}
        qe = {k: (qs[k] * ex[k][0:c]).astype(BF16) for k in items}
        st = {h: st_ref[h] for h in heads} if chained else None
        for s in range(sub):
            if not chained:
                st = {h: s0_ref[it * sub + s, h].T for h in heads}
            inter = {h: _dot_nt(qe[(s, h)], st[h].astype(BF16)) for h in heads}
            st = {h: st[h] * ex[(s, h)][c - 1:c, :] + kv[(s, h)] for h in heads}
            if not chained:
                for h in heads:
                    so_ref[it * sub + s, h] = st[h].T
            for h in heads:
                o = inter[h] + intra[(s, h)]
                o = o * lax.rsqrt(jnp.mean(o * o, axis=-1, keepdims=True) + HGRN_EPS)
                gg = g_ref[rows_s[s], lanes(h)]
                o_ref[rows_s[s], lanes(h)] = o * nw_ref[:, lanes(h)] * (gg * _sigmoid(gg))
        if chained:
            for h in heads:
                st_ref[h] = st[h]
        return carry

    lax.fori_loop(0, rb // (c * sub), block, 0)

    if chained:
        @pl.when(cidx == pl.num_programs(1) - 1)
        def _():
            for h in range(HGRN_HEADS):
                so_ref[0, h] = st_ref[h].T


def _hgrn(p3, s0_all, l, so_prev, lb, nw, nseq, t):
    rows = nseq * t
    depth = s0_all.shape[0]
    c = math.gcd(t, 64)
    chained = t > c
    w3, masks, nlev = _hgrn_tables(c)
    w = HGRN_WIDTH
    if chained:
        rb = 256
        nblk = t // rb
        grid = (nseq, nblk)
        col = lambda cb: (lambda n, k: (n * nblk + k, cb))
        out_map = lambda n, k: (n * nblk + k, 0)
        nsb = 1
    else:
        nsb = 16
        rb = nsb * c
        grid = (rows // rb, 1)
        col = lambda cb: (lambda n, k: (n, cb))
        out_map = lambda n, k: (n, 0)
    st_spec = pl.BlockSpec((None, nsb, HGRN_HEADS, HGRN_DIM, HGRN_DIM), lambda n, k: (l, n, 0, 0, 0))
    whole2 = lambda n, k: (0, 0)
    whole3 = lambda n, k: (0, 0, 0)
    aliased = so_prev is not None
    in_specs = [pl.BlockSpec((rb, w), col(4)), pl.BlockSpec((rb, w), col(5)), pl.BlockSpec((rb, w), col(6)),
                pl.BlockSpec((rb, w), col(7)), st_spec,
                pl.BlockSpec((1, w), whole2), pl.BlockSpec((1, w), whole2),
                pl.BlockSpec(w3.shape, whole2), pl.BlockSpec(masks.shape, whole3)]
    args = [p3, p3, p3, p3, s0_all, lb, nw, w3, masks]
    aliases = {}
    if aliased:
        in_specs.append(pl.BlockSpec(memory_space=pl.ANY))
        args.append(so_prev)
        aliases = {9: 1}
    return pl.pallas_call(
        functools.partial(_hgrn_kernel, rb=rb, c=c, nlev=nlev, chained=chained, aliased=aliased,
                          sub=2 if chained else 4),
        grid=grid,
        in_specs=in_specs,
        out_specs=[pl.BlockSpec((rb, w), out_map), st_spec],
        out_shape=[jax.ShapeDtypeStruct((rows, w), F32),
                   jax.ShapeDtypeStruct((depth, nseq, HGRN_HEADS, HGRN_DIM, HGRN_DIM), F32)],
        scratch_shapes=[pltpu.VMEM((HGRN_HEADS, HGRN_DIM, HGRN_DIM), F32)],
        input_output_aliases=aliases,
        compiler_params=_cparams(("parallel", "arbitrary")),
        name="hgrn_chain" if chained else "hgrn_tile",
    )(*args)


def _run(x, nseq, t, states, lw, ffw, norm_final):
    s5_re0, s5_im0, shift0, wkv0, hgrn0 = states
    depth = len(lw)
    s5_re0 = s5_re0.reshape(depth, nseq, S5_LANES)
    s5_im0 = s5_im0.reshape(depth, nseq, S5_LANES)
    s5_re_l, s5_im_l, shift_l = [], [], []
    wkv_out = None
    hgrn_out = None
    cast = not isinstance(ffw["f1g"], list)
    ffb = {k: [] for k in ffw}

    def ffn(x, nw, names, l, fnw=None):
        if not cast:
            return _ffn(x, nw, *(ffw[k][l][None] for k in names), 0, fnw=fnw)
        res = _ffn(x, nw, *(ffw[k] for k in names), l, cast=True, fnw=fnw)
        for k, wb in zip(names, res[1:]):
            ffb[k].append(wb)
        return res[0]

    for l in range(depth):
        w = lw[l]
        x = ffn(x, w["norm_ffn1"], ("f1g", "f1u", "f1d"), l)
        p3 = _inproj(x, w["norm_mix"], w["w_in"], l)
        y_s5, s_re, s_im = _s5(p3, s5_re0, s5_im0, l, w["s5"], nseq, t)
        (y_rw, bonus, gate), wkv_out = _rwkv(p3, shift0[l], wkv0, l, wkv_out, w["rwkv"], nseq, t)
        y_hg, hgrn_out = _hgrn(p3, hgrn0, l, hgrn_out, w["hgrn_lb"], w["hgrn_nw"], nseq, t)
        x = _outproj(x, y_s5, y_rw, bonus, gate, y_hg, w["w_out"], l, w["rwkv"])
        x = ffn(x, w["norm_ffn2"], ("f2g", "f2u", "f2d"), l,
                fnw=norm_final.reshape(1, -1) if l == depth - 1 else None)
        s5_re_l.append(s_re)
        s5_im_l.append(s_im)
        shift_l.append(p3.reshape(nseq, t, N_IN)[:, t - 1, S5_WIDTH:S5_WIDTH + RWKV_SHIFT_WIDTH])
    return (x,jnp.stack(s5_re_l), jnp.stack(s5_im_l), jnp.stack(shift_l), wkv_out, hgrn_out), ffb


def kernel(x_prompt, x_sample, state_s5_re, state_s5_im, state_rwkv_shift, state_rwkv_wkv, state_hgrn, norm_ffn1, ffn1_w_gate, ffn1_w_up, ffn1_w_down, norm_mix, w_in, s5_a_re, s5_a_im, s5_log_dt, s5_b_re, s5_b_im, s5_c_re, s5_c_im, s5_d, s5_w_glu, s5_b_glu, rwkv_mu, rwkv_w0, rwkv_w2, rwkv_a0, rwkv_a2, rwkv_g2, rwkv_k_k, rwkv_k_a, rwkv_r_k, rwkv_ln_w, rwkv_ln_b, hgrn_lb_raw, hgrn_norm_w, w_out, norm_ffn2, ffn2_w_gate, ffn2_w_up, ffn2_w_down, norm_final):
    depth = w_in.shape[0]
    nb, seq, d = x_prompt.shape
    ns, dseq, _ = x_sample.shape

    p_lb = jax.nn.softmax(hgrn_lb_raw.astype(F32), axis=0)
    lower_bounds = jnp.cumsum(p_lb, axis=0) - p_lb[0]

    w_in_b = w_in.astype(BF16)
    w_out_b = w_out.astype(BF16)
    s5_all = jax.vmap(_s5_consts)(s5_a_re, s5_a_im, s5_log_dt, s5_b_re, s5_b_im, s5_c_re, s5_c_im,
                                  s5_d, s5_w_glu, s5_b_glu)
    lw = []
    for l in range(depth):
        lw.append(dict(
            w_in=w_in_b, w_out=w_out_b,
            norm_ffn1=norm_ffn1[l].reshape(1, -1), norm_mix=norm_mix[l].reshape(1, -1),
            norm_ffn2=norm_ffn2[l].reshape(1, -1),
            s5=s5_all,
            rwkv=_rwkv_consts(rwkv_mu[l], rwkv_w0[l], rwkv_w2[l], rwkv_a0[l], rwkv_a2[l], rwkv_g2[l],
                              rwkv_k_k[l], rwkv_k_a[l], rwkv_r_k[l], rwkv_ln_w[l], rwkv_ln_b[l]),
            hgrn_lb=lower_bounds[l].reshape(1, -1), hgrn_nw=hgrn_norm_w[l].reshape(1, -1),
        ))

    def zeros_like_state(s):
        return jnp.zeros((depth, nb) + s.shape[2:], F32)

    p_states = tuple(zeros_like_state(s) for s in
                     (state_s5_re, state_s5_im, state_rwkv_shift, state_rwkv_wkv, state_hgrn))
    s_states = (state_s5_re, state_s5_im, state_rwkv_shift, state_rwkv_wkv, state_hgrn)
    ffw = dict(f1g=ffn1_w_gate, f1u=ffn1_w_up, f1d=ffn1_w_down, f2g=ffn2_w_gate, f2u=ffn2_w_up, f2d=ffn2_w_down)
    (y_s, s5re_s, s5im_s, shift_s, wkv_s, hgrn_s), ffb = _run(
        x_sample.reshape(ns * dseq, d), ns, dseq, s_states, lw, ffw, norm_final)
    (y_p, s5re_p, s5im_p, shift_p, wkv_p, hgrn_p), _ = _run(
        x_prompt.reshape(nb * seq, d), nb, seq, p_states, lw, ffb, norm_final)
    return (y_p.reshape(nb, seq, d), y_s.reshape(ns, dseq, d), s5re_p, s5im_p, shift_p, wkv_p, hgrn_p,
            s5re_s, s5im_s, shift_s, wkv_s, hgrn_s)
```

```python
import functools
import math

import numpy as np
import jax
import jax.numpy as jnp
from jax import lax
from jax.experimental import pallas as pl
from jax.experimental.pallas import tpu as pltpu

F32 = jnp.float32
BF16 = jnp.bfloat16

NORM_EPS = 1e-6
RWKV_GN_EPS = 64e-5
HGRN_EPS = 1e-5

D_MODEL = 2048
S5_WIDTH = 512
S5_GROUP = 16
S5_GROUPS = 32
S5_STATE = 64
S5_LANES = S5_GROUPS * S5_STATE
S5_GB = 4
S5_TR = 512
S5_SEG = S5_TR // 8
RWKV_WIDTH = 768
RWKV_HEAD = 64
RWKV_HEADS = 12
RWKV_PAIRS = 6
RWKV_SHIFT_WIDTH = 2560
RWKV_ZBLOCKS = RWKV_SHIFT_WIDTH // S5_WIDTH
RWKV_BLOCK = 16
RWKV_TM = 128
SEG_CHUNK = 256
HGRN_WIDTH = 768
HGRN_HEADS = 6
HGRN_DIM = 128
N_IN = 6144

SUBLANES = 8
LANES = 128
VMEM_LIMIT = 60 * 1024 * 1024


def _cparams(sem):
    return pltpu.CompilerParams(dimension_semantics=sem, vmem_limit_bytes=VMEM_LIMIT)


def _split2(x):
    hi = x.astype(BF16)
    lo = (x - hi.astype(F32)).astype(BF16)
    return hi, lo


def _split3(x):
    p1 = x.astype(BF16)
    r1 = x - p1.astype(F32)
    p2 = r1.astype(BF16)
    p3 = (r1 - p2.astype(F32)).astype(BF16)
    return p1, p2, p3


def _dot(a, b):
    return jnp.dot(a, b, preferred_element_type=F32)


def _dot_nt(a, b):
    return lax.dot_general(a, b, (((1,), (1,)), ((), ())), preferred_element_type=F32)


def _dot_tn(a, b):
    return lax.dot_general(a, b, (((0,), (0,)), ((), ())), preferred_element_type=F32)


def _sigmoid(x):
    return 1.0 / (1.0 + jnp.exp(-x))


def _ffn_kernel(*refs, cast, final):
    x_ref, nw_ref, wg_ref, wu_ref, wd_ref = refs[:5]
    refs = refs[5:]
    if final:
        fnw_ref = refs[0]
        refs = refs[1:]
    if cast:
        o_ref, wgb_ref, wub_ref, wdb_ref, h_ref = refs
    else:
        o_ref, h_ref = refs
    j = pl.program_id(1)

    @pl.when(j == 0)
    def _():
        rows = o_ref.shape[0]
        step = min(rows, 256)
        for r0 in range(0, rows, step):
            rs = slice(r0, r0 + step)
            x = x_ref[rs, :]
            ms = jnp.mean(x * x, axis=-1, keepdims=True)
            h_ref[rs, :] = (x * lax.rsqrt(ms + NORM_EPS) * nw_ref[...]).astype(BF16)
        o_ref[...] = jnp.zeros_like(o_ref)

    wg = wg_ref[...].astype(BF16)
    wu = wu_ref[...].astype(BF16)
    wd = wd_ref[...].astype(BF16)
    if cast:
        wgb_ref[...] = wg
        wub_ref[...] = wu
        wdb_ref[...] = wd
    h = h_ref[...]
    g = _dot(h, wg)
    u = _dot(h, wu)
    o_ref[...] += _dot((g * _sigmoid(g) * u).astype(BF16), wd)

    @pl.when(j == pl.num_programs(1) - 1)
    def _():
        rows = o_ref.shape[0]
        step = min(rows, 256)
        for r0 in range(0, rows, step):
            rs = slice(r0, r0 + step)
            o = x_ref[rs, :] + 0.5 * o_ref[rs, :]
            if final:
                o = o * lax.rsqrt(jnp.mean(o * o, axis=-1, keepdims=True) + NORM_EPS) * fnw_ref[...]
            o_ref[rs, :] = o


def _ffn(x, nw, wg, wu, wd, l, cast=False, fnw=None, tm=1024):
    m, d = x.shape
    ff = wg.shape[2]
    tf = 256 if cast else 512
    final = fnw is not None
    if final and not cast:
        tm = tm // 2
    out_specs = [pl.BlockSpec((tm, d), lambda i, j: (i, 0))]
    out_shape = [jax.ShapeDtypeStruct((m, d), F32)]
    if cast:
        assert m == tm
        out_specs += [pl.BlockSpec((d, tf), lambda i, j: (0, j)), pl.BlockSpec((d, tf), lambda i, j: (0, j)),
                      pl.BlockSpec((tf, d), lambda i, j: (j, 0))]
        out_shape += [jax.ShapeDtypeStruct((d, ff), BF16), jax.ShapeDtypeStruct((d, ff), BF16),
                      jax.ShapeDtypeStruct((ff, d), BF16)]
    vec = pl.BlockSpec((1, d), lambda i, j: (0, 0))
    outs = pl.pallas_call(
        functools.partial(_ffn_kernel, cast=cast, final=final),
        grid=(m // tm, ff // tf),
        in_specs=[
            pl.BlockSpec((tm, d), lambda i, j: (i, 0)),
            vec,
            pl.BlockSpec((None, d, tf), lambda i, j: (l, 0, j)),
            pl.BlockSpec((None, d, tf), lambda i, j: (l, 0, j)),
            pl.BlockSpec((None, tf, d), lambda i, j: (l, j, 0)),
        ] + ([vec] if final else []),
        out_specs=out_specs,
        out_shape=out_shape,
        scratch_shapes=[pltpu.VMEM((tm, d), BF16)],
        compiler_params=_cparams(("parallel", "arbitrary")),
        name="ffn_cast" if cast else "ffn",
    )(x, nw, wg, wu, wd, *((fnw,) if final else ()))
    return outs if cast else outs[0]


def _inproj_kernel(*refs, cast):
    if cast:
        x_ref, nw_ref, w_ref, o_ref, wb_ref, h_ref = refs
    else:
        x_ref, nw_ref, w_ref, o_ref, h_ref = refs

    @pl.when(pl.program_id(1) == 0)
    def _():
        x = x_ref[...]
        ms = jnp.mean(x * x, axis=-1, keepdims=True)
        h_ref[...] = (x * lax.rsqrt(ms + NORM_EPS) * nw_ref[...]).astype(BF16)

    w = w_ref[...].astype(BF16)
    if cast:
        wb_ref[...] = w
    o_ref[...] = _dot(h_ref[...], w)


def _inproj(x, nw, w, l, cast=False, tm=1024):
    m, d = x.shape
    n = w.shape[2]
    tn = 512 if cast else 1024
    out_specs = [pl.BlockSpec((tm, tn), lambda i, j: (i, j))]
    out_shape = [jax.ShapeDtypeStruct((m, n), F32)]
    if cast:
        assert m == tm
        out_specs.append(pl.BlockSpec((d, tn), lambda i, j: (0, j)))
        out_shape.append(jax.ShapeDtypeStruct((d, n), BF16))
    outs = pl.pallas_call(
        functools.partial(_inproj_kernel, cast=cast),
        grid=(m // tm, n // tn),
        in_specs=[
            pl.BlockSpec((tm, d), lambda i, j: (i, 0)),
            pl.BlockSpec((1, d), lambda i, j: (0, 0)),
            pl.BlockSpec((None, d, tn), lambda i, j: (l, 0, j)),
        ],
        out_specs=out_specs,
        out_shape=out_shape,
        scratch_shapes=[pltpu.VMEM((tm, d), BF16)],
        compiler_params=_cparams(("parallel", "arbitrary")),
        name="inproj_cast" if cast else "inproj",
    )(x, nw, w)
    return outs if cast else outs[0]


def _outproj_kernel(x_ref, a_ref, y_ref, bonus_ref, g_ref, c_ref, w_ref, lnw_ref, lnb_ref, e2_ref, o_ref):
    y = y_ref[...]
    e2 = e2_ref[...]
    mean = _seg_sum(y, e2) * (1.0 / RWKV_HEAD)
    yc = y - mean
    var = _seg_sum(yc * yc, e2) * (1.0 / RWKV_HEAD)
    b = (yc * lax.rsqrt(var + RWKV_GN_EPS) * lnw_ref[...] + lnb_ref[...] + bonus_ref[...]) * g_ref[...]
    o1 = S5_WIDTH
    o2 = S5_WIDTH + RWKV_WIDTH
    acc = _dot(a_ref[...].astype(BF16), w_ref[0:o1, :])
    acc += _dot(b.astype(BF16), w_ref[o1:o2, :])
    acc += _dot(c_ref[...].astype(BF16), w_ref[o2:, :])
    o_ref[...] = x_ref[...] + acc


def _outproj(x, ya, y_rw, bonus, g, yc, w_all, l, cs, tm=512):
    m, d = x.shape
    row = lambda i: (i, 0)
    whole = lambda i: (0, 0)
    wide = lambda a: pl.BlockSpec((tm, a.shape[1]), row)
    vec = pl.BlockSpec((1, RWKV_WIDTH), whole)
    return pl.pallas_call(
        _outproj_kernel,
        grid=(m // tm,),
        in_specs=[
            pl.BlockSpec((tm, d), row), wide(ya), wide(y_rw), wide(bonus), wide(g), wide(yc),
            pl.BlockSpec((None,) + w_all.shape[1:], lambda i: (l, 0, 0)),
            vec, vec, pl.BlockSpec((2 * SEG_CHUNK, SEG_CHUNK), whole),
        ],
        out_specs=pl.BlockSpec((tm, d), row),
        out_shape=jax.ShapeDtypeStruct((m, d), F32),
        compiler_params=_cparams(("parallel",)),
        name="outproj",
    )(x, ya, y_rw, bonus, g, yc, w_all, cs["ln_w"], cs["ln_b"], cs["e2w"])


def _s5_kernel(u_ref, x0r_ref, x0i_ref, b3r_ref, b3i_ref, cr_ref, ci_ref, cst_ref, d_ref, wglu_ref,
               bglu_ref, y_ref, xlr_ref, xli_ref, xr_ref, xi_ref, cr_s, ci_s, pm_ref, *, tr, chained):
    c = pl.program_id(1)
    nlt = S5_LANES // LANES
    per_gb = nlt // S5_GB

    def load_tile(ref, rows):
        return jnp.concatenate([ref[k, rows, :] for k in range(nlt)], axis=1)

    def store_tile(ref, rows, val):
        for k in range(nlt):
            ref[k, rows, :] = val[:, k * LANES:(k + 1) * LANES]

    u = u_ref[...]
    nut = S5_WIDTH // LANES
    seg = tr // SUBLANES
    if chained:
        for k in range(nut):
            pm_ref[k] = u[:, k * LANES:(k + 1) * LANES]
        u = jnp.concatenate(
            [jnp.concatenate([pm_ref[k, pl.ds(tau, SUBLANES, stride=seg), :] for tau in range(seg)], axis=0)
             for k in range(nut)], axis=1)
    uh, ul = _split2(u)
    for gb in range(S5_GB):
        sl = slice(gb * LANES, (gb + 1) * LANES)
        if chained:
            br = _dot(uh[:, sl], b3r_ref[gb, 0:LANES, :])
            bi = _dot(uh[:, sl], b3i_ref[gb, 0:LANES, :])
        else:
            lhs = jnp.concatenate([uh[:, sl], ul[:, sl], uh[:, sl]], axis=1)
            br = _dot(lhs, b3r_ref[gb])
            bi = _dot(lhs, b3i_ref[gb])
        for k in range(per_gb):
            xr_ref[gb * per_gb + k] = br[:, k * LANES:(k + 1) * LANES]
            xi_ref[gb * per_gb + k] = bi[:, k * LANES:(k + 1) * LANES]

    if chained:
        @pl.when(c == 0)
        def _():
            cr_s[...] = x0r_ref[0]
            ci_s[...] = x0i_ref[0]

    a1r, a1i, a2r, a2i, a4r, a4i, pr, pi = [cst_ref[i] for i in range(8)]

    def tile_scan(xr, xi, x0r, x0i, lv1, lv2, lv4, pw):
        for k, (ar, ai) in ((1, lv1), (2, lv2), (4, lv4)):
            sr = pltpu.roll(xr, k, 0)
            si = pltpu.roll(xi, k, 0)
            xr, xi = xr + ar * sr - ai * si, xi + ar * si + ai * sr
        x0r = jnp.broadcast_to(x0r, xr.shape)
        x0i = jnp.broadcast_to(x0i, xr.shape)
        return xr + pw[0] * x0r - pw[1] * x0i, xi + pw[0] * x0i + pw[1] * x0r

    if chained:
        seg = tr // SUBLANES
        b1r, b1i, b2r, b2i, b4r, b4i, qr64, qi64 = [cst_ref[8 + i] for i in range(8)]
        lt = lambda x, k: x[:, k * LANES:(k + 1) * LANES]
        ar = [jnp.broadcast_to(lt(pr, k)[0:1, :], (SUBLANES, LANES)) for k in range(nlt)]
        ai = [jnp.broadcast_to(lt(pi, k)[0:1, :], (SUBLANES, LANES)) for k in range(nlt)]

        def sweep1(tau, carry):
            rows = pl.ds(pl.multiple_of(tau * SUBLANES, SUBLANES), SUBLANES)
            out = []
            for k in range(nlt):
                xr, xi = carry[2 * k], carry[2 * k + 1]
                xr, xi = (ar[k] * xr - ai[k] * xi + xr_ref[k, rows, :], ar[k] * xi + ai[k] * xr + xi_ref[k, rows, :])
                xr_ref[k, rows, :] = xr
                xi_ref[k, rows, :] = xi
                out += [xr, xi]
            return tuple(out)

        zero = jnp.zeros((SUBLANES, LANES), F32)
        ends = lax.fori_loop(0, seg, sweep1, (zero,) * (2 * nlt))
        er = jnp.concatenate(ends[0::2], axis=1)
        ei = jnp.concatenate(ends[1::2], axis=1)
        x0r = cr_s[...]
        x0i = ci_s[...]
        yr, yi = tile_scan(er, ei, x0r, x0i, (b1r, b1i), (b2r, b2i), (b4r, b4i), (qr64, qi64))
        cr_s[...] = yr[SUBLANES - 1:SUBLANES, :]
        ci_s[...] = yi[SUBLANES - 1:SUBLANES, :]
        rid = lax.broadcasted_iota(jnp.int32, pr.shape, 0)
        sr0 = jnp.where(rid == 0, jnp.broadcast_to(x0r, pr.shape), pltpu.roll(yr, 1, 0))
        si0 = jnp.where(rid == 0, jnp.broadcast_to(x0i, pr.shape), pltpu.roll(yi, 1, 0))

        def sweep2(tau, carry):
            rows = pl.ds(pl.multiple_of(tau * SUBLANES, SUBLANES), SUBLANES)
            out = []
            for k in range(nlt):
                qr, qi = carry[2 * k], carry[2 * k + 1]
                qr, qi = ar[k] * qr - ai[k] * qi, ar[k] * qi + ai[k] * qr
                xr_ref[k, rows, :] = xr_ref[k, rows, :] + qr
                xi_ref[k, rows, :] = xi_ref[k, rows, :] + qi
                out += [qr, qi]
            return tuple(out)

        start = []
        for k in range(nlt):
            start += [lt(sr0, k), lt(si0, k)]
        lax.fori_loop(0, seg, sweep2, tuple(start))

        @pl.when(c == pl.num_programs(1) - 1)
        def _():
            xlr_ref[0] = cr_s[...]
            xli_ref[0] = ci_s[...]
    else:
        def tile(j, carry):
            rows = pl.ds(pl.multiple_of(j * SUBLANES, SUBLANES), SUBLANES)
            xr, xi = tile_scan(load_tile(xr_ref, rows), load_tile(xi_ref, rows),
                               x0r_ref[pl.ds(j, 1), :], x0i_ref[pl.ds(j, 1), :],
                               (a1r, a1i), (a2r, a2i), (a4r, a4i), (pr, pi))
            store_tile(xr_ref, rows, xr)
            store_tile(xi_ref, rows, xi)
            xlr_ref[pl.ds(j, 1), :] = xr[SUBLANES - 1:SUBLANES, :]
            xli_ref[pl.ds(j, 1), :] = xi[SUBLANES - 1:SUBLANES, :]
            return carry

        lax.fori_loop(0, tr // SUBLANES, tile, 0)

    ys = []
    for gb in range(S5_GB):
        xr = jnp.concatenate([xr_ref[gb * per_gb + k] for k in range(per_gb)], axis=1)
        xi = jnp.concatenate([xi_ref[gb * per_gb + k] for k in range(per_gb)], axis=1)
        ys.append(_dot(xr.astype(BF16), cr_ref[gb]) + _dot(xi.astype(BF16), ci_ref[gb]))
    y = jnp.concatenate(ys, axis=1) + d_ref[...] * u
    y = 0.5 * y * (1.0 + jnp.tanh(math.sqrt(2.0 / math.pi) * (y + 0.044715 * (y * y * y))))
    z = _dot(y.astype(BF16), wglu_ref[...]) + bglu_ref[...]
    y = y * _sigmoid(z)
    if chained:
        for k in range(nut):
            pm_ref[k] = y[:, k * LANES:(k + 1) * LANES]
        for m in range(tr // SUBLANES):
            start = (m % SUBLANES) * SUBLANES * SUBLANES + m // SUBLANES
            for k in range(nut):
                y_ref[m * SUBLANES:(m + 1) * SUBLANES, k * LANES:(k + 1) * LANES] = (
                    pm_ref[k, pl.ds(start, SUBLANES, stride=SUBLANES), :])
    else:
        y_ref[...] = y


def _s5(p3, x0r, x0i, l, cs, nseq, t, tr=S5_TR):
    rows = nseq * t
    depth = x0r.shape[0]
    chained = t > SUBLANES
    if chained:
        grid = (nseq, t // tr)
        row_map = lambda n, c: (n * (t // tr) + c, 0)
        out_map = lambda n, c: (n * (t // tr) + c, 0)
        x0r = x0r.reshape(depth, nseq, 1, S5_LANES)
        x0i = x0i.reshape(depth, nseq, 1, S5_LANES)
        in_st = pl.BlockSpec((None, 1, 1, S5_LANES), lambda n, c: (l, n, 0, 0))
        st_spec = pl.BlockSpec((1, 1, S5_LANES), lambda n, c: (n, 0, 0))
        st_shape = jax.ShapeDtypeStruct((nseq, 1, S5_LANES), F32)
    else:
        assert t == SUBLANES
        grid = (rows // tr, 1)
        row_map = lambda n, c: (n, 0)
        out_map = lambda n, c: (n, 0)
        in_st = pl.BlockSpec((None, tr // SUBLANES, S5_LANES), lambda n, c: (l, n, 0))
        st_spec = pl.BlockSpec((tr // SUBLANES, S5_LANES), lambda n, c: (n, 0))
        st_shape = jax.ShapeDtypeStruct((nseq, S5_LANES), F32)
    names = ("b3r", "b3i", "cr", "ci", "scan", "d", "wglu", "bglu")
    layer = lambda a: pl.BlockSpec((None,) + a.shape[1:], lambda n, c: (l,) + (0,) * (a.ndim - 1))
    y, xlr, xli = pl.pallas_call(
        functools.partial(_s5_kernel, tr=tr, chained=chained),
        grid=grid,
        in_specs=[pl.BlockSpec((tr, S5_WIDTH), row_map), in_st, in_st] + [layer(cs[k]) for k in names],
        out_specs=[pl.BlockSpec((tr, S5_WIDTH), out_map), st_spec, st_spec],
        out_shape=[jax.ShapeDtypeStruct((rows, S5_WIDTH), F32), st_shape, st_shape],
        scratch_shapes=[pltpu.VMEM((S5_LANES // LANES, tr, LANES), F32), pltpu.VMEM((S5_LANES // LANES, tr, LANES), F32),
                        pltpu.VMEM((1, S5_LANES), F32), pltpu.VMEM((1, S5_LANES), F32),
                        pltpu.VMEM((S5_WIDTH // LANES, tr, LANES), F32)],
        compiler_params=_cparams(("parallel", "arbitrary")),
        name="s5_chain" if chained else "s5_tile",
    )(p3, x0r, x0i, *(cs[k] for k in names))
    return y, xlr.reshape(nseq, S5_GROUPS, S5_STATE), xli.reshape(nseq, S5_GROUPS, S5_STATE)


def _s5_consts(a_re, a_im, log_dt, b_re, b_im, c_re, c_im, d, w_glu, b_glu):
    g, p, h = S5_GROUPS, S5_STATE, S5_GROUP
    dt = jnp.exp(log_dt)[:, None]
    mag = jnp.exp(a_re * dt)
    abr = mag * jnp.cos(a_im * dt)
    abi = mag * jnp.sin(a_im * dt)
    den = a_re * a_re + a_im * a_im
    qr = ((abr - 1.0) * a_re + abi * a_im) / den
    qi = (abi * a_re - (abr - 1.0) * a_im) / den
    bbr = qr[..., None] * b_re - qi[..., None] * b_im
    bbi = qr[..., None] * b_im + qi[..., None] * b_re
    row = jnp.arange(SUBLANES)[:, None]

    def scan_tables(mr, mi):
        pr, pi = [mr], [mi]
        for _ in range(SUBLANES - 1):
            pr, pi = pr + [pr[-1] * mr - pi[-1] * mi], pi + [pr[-1] * mi + pi[-1] * mr]
        pr = jnp.stack(pr).reshape(SUBLANES, g * p)
        pi = jnp.stack(pi).reshape(SUBLANES, g * p)

        def lvl(k):
            return [jnp.where(row >= k, pr[k - 1][None, :], 0.0), jnp.where(row >= k, pi[k - 1][None, :], 0.0)]

        return lvl(1) + lvl(2) + lvl(4) + [pr, pi]

    sr, si = abr, abi
    for _ in range(S5_SEG.bit_length() - 1):
        sr, si = sr * sr - si * si, 2.0 * sr * si
    scan = jnp.stack(scan_tables(abr, abi) + scan_tables(sr, si)).astype(F32)
    eye = jnp.eye(SUBLANES, dtype=F32)

    def in_blocks(x):
        xx = x.reshape(S5_GB, SUBLANES, p, h).transpose(0, 1, 3, 2)
        blk = (xx[:, :, :, None, :] * eye[None, :, None, :, None]).reshape(S5_GB, SUBLANES * h, SUBLANES * p)
        hi = blk.astype(BF16)
        lo = (blk - hi.astype(F32)).astype(BF16)
        return jnp.concatenate([hi, hi, lo], axis=1)

    def out_blocks(x):
        xx = x.reshape(S5_GB, SUBLANES, h, p).transpose(0, 1, 3, 2)
        return (xx[:, :, :, None, :] * eye[None, :, None, :, None]).reshape(
            S5_GB, SUBLANES * p, SUBLANES * h).astype(BF16)

    return dict(b3r=in_blocks(bbr), b3i=in_blocks(bbi),
                cr=out_blocks(c_re), ci=out_blocks(-c_im), scan=scan,
                d=d.reshape(1, -1), wglu=w_glu.astype(BF16), bglu=b_glu.reshape(1, -1))


def _seg_sum(x, e2):
    hi, lo = _split2(x)
    cw = e2.shape[1]
    return jnp.concatenate(
        [_dot(jnp.concatenate([hi[:, c:c + cw], lo[:, c:c + cw]], axis=1), e2) for c in range(0, x.shape[1], cw)],
        axis=1)


def _rwkv_prep_kernel(*refs, nlev, chained, tiles_per_seq):
    nz = RWKV_ZBLOCKS
    z = jnp.concatenate([r[...] for r in refs[:nz]], axis=1)
    if chained:
        prev8 = jnp.concatenate([r[...] for r in refs[nz:2 * nz]], axis=1)
        sh_ref = refs[2 * nz]
        refs = refs[2 * nz + 1:]
    else:
        zp_ref = refs[nz]
        refs = refs[nz + 1:]
    (mu_ref, w0_ref, a0_ref, w2_ref, a2_ref, g2_ref, kk_ref, ka_ref, rk_ref, e2_ref, tril_ref, mk_ref, hm_ref,
     ap_o, rp_o, bt_o, kp_o, cc_o, y0_o, v_o, gm_o, g_o, bonus_o) = refs
    tm = RWKV_TM
    if chained:
        first = (pl.program_id(0) % tiles_per_seq) == 0
        prev_row = jnp.where(first, sh_ref[0], prev8[SUBLANES - 1:SUBLANES, :])
        rid = lax.broadcasted_iota(jnp.int32, z.shape, 0)
        zp = jnp.where(rid == 0, prev_row, pltpu.roll(z, 1, 0))
    else:
        zp = zp_ref[...]
    zm = z + (zp - z) * mu_ref[...]
    w = RWKV_WIDTH
    r = zm[:, 0:w]
    k = zm[:, w:2 * w]
    v = zm[:, 2 * w:3 * w]
    wa = zm[:, 3 * w:3 * w + LANES]
    gi = zm[:, 3 * w + LANES:3 * w + 2 * LANES]
    e2 = e2_ref[...]
    wl = w0_ref[...] + _dot(jnp.tanh(wa).astype(BF16), w2_ref[...])
    sp = jnp.maximum(-wl, 0.0) + jnp.log(1.0 + jnp.exp(-jnp.abs(wl)))
    ld = -jnp.exp(-sp - 0.5)
    a = _sigmoid(a0_ref[...] + _dot(wa.astype(BF16), a2_ref[...]))
    g_o[...] = _dot(_sigmoid(gi).astype(BF16), g2_ref[...])
    kk = k * kk_ref[...]
    kk = kk * lax.rsqrt(jnp.maximum(_seg_sum(kk * kk, e2), 1e-24))
    k2 = k * (1.0 + (a - 1.0) * ka_ref[...])
    b = kk * a
    bonus_o[...] = _seg_sum(r * k2 * rk_ref[...], e2) * v
    v_o[...] = v

    p1, p2, p3 = _split3(ld)
    cum = _dot(tril_ref[...], jnp.concatenate([p1, p2, p3], axis=0))
    gam = jnp.exp(cum)
    ginv = jnp.exp(-cum)
    gm_o[...] = gam
    alpha = kk * jnp.exp(cum - ld)
    beta = b * ginv
    kap = k2 * ginv
    rho = r * gam
    bt_o[...] = beta
    kp_o[...] = kap

    strict = mk_ref[0]
    incl = mk_ref[1]
    eye = mk_ref[2]
    heads = [(j, e) for j in range(RWKV_PAIRS) for e in range(2)]
    nh = len(heads)

    def lanes(j):
        return slice(j * LANES, (j + 1) * LANES)

    al = [alpha[:, lanes(j)] * hm_ref[e] for j, e in heads]
    rh = [rho[:, lanes(j)] * hm_ref[e] for j, e in heads]
    vh = [(v[:, lanes(j)] * hm_ref[e]).astype(BF16) for j, e in heads]
    yk = [jnp.concatenate([beta[:, lanes(j)], kap[:, lanes(j)]], axis=0).astype(BF16) for j in range(RWKV_PAIRS)]
    wu = [_dot_nt(jnp.concatenate(_split2(al[i]), axis=1), jnp.concatenate([yk[j], yk[j]], axis=1))
          for i, (j, e) in enumerate(heads)]
    wl = [_dot_nt(rh[i].astype(BF16), yk[j]) for i, (j, e) in enumerate(heads)]
    mm = [wu[i][:, :tm] * strict for i in range(nh)]
    nn = [wu[i][:, tm:] * strict for i in range(nh)]
    pp = [(wl[i][:, :tm] * incl).astype(BF16) for i in range(nh)]
    qq = [(wl[i][:, tm:] * incl).astype(BF16) for i in range(nh)]
    tinv = [eye - mm[i] * mk_ref[3] for i in range(nh)]
    for lv in range(1, nlev):
        th = [tinv[i].astype(BF16) for i in range(nh)]
        tmp = [_dot(th[i], (mm[i] * mk_ref[3 + lv]).astype(BF16)) for i in range(nh)]
        tinv = [tinv[i] - _dot(tmp[i].astype(BF16), th[i]) for i in range(nh)]
    th = [tinv[i].astype(BF16) for i in range(nh)]
    nv = [_dot(nn[i].astype(BF16), vh[i]) for i in range(nh)]
    qv = [_dot(qq[i], vh[i]) for i in range(nh)]
    tac = [_dot(th[i], jnp.concatenate([al[i], nv[i]], axis=1).astype(BF16)) for i in range(nh)]
    pac = [_dot(pp[i], tac[i].astype(BF16)) for i in range(nh)]
    ap = [tac[i][:, :LANES] for i in range(nh)]
    cc = [tac[i][:, LANES:] for i in range(nh)]
    rp = [rh[i] - pac[i][:, :LANES] for i in range(nh)]
    y0 = [qv[i] - pac[i][:, LANES:] for i in range(nh)]
    for j in range(RWKV_PAIRS):
        ap_o[:, lanes(j)] = ap[2 * j] + ap[2 * j + 1]
        rp_o[:, lanes(j)] = rp[2 * j] + rp[2 * j + 1]
        cc_o[:, lanes(j)] = cc[2 * j] + cc[2 * j + 1]
        y0_o[:, lanes(j)] = y0[2 * j] + y0[2 * j + 1]


def _rwkv_tables(tm, blk):
    t = np.arange(tm)
    same = (t[:, None] // blk) == (t[None, :] // blk)
    tril = (same & (t[None, :] <= t[:, None])).astype(np.float32)
    masks = [same & (t[None, :] < t[:, None]), same & (t[None, :] <= t[:, None]), np.eye(tm, dtype=bool)]
    n = 2
    while n <= blk:
        m = n // 2
        masks.append(((t[:, None] // n) == (t[None, :] // n)) & ((t[:, None] % n) >= m) & ((t[None, :] % n) < m))
        n *= 2
    nlev = len(masks) - 3
    hm = np.stack([(np.arange(LANES) < RWKV_HEAD), (np.arange(LANES) >= RWKV_HEAD)]).astype(np.float32)
    return (jnp.asarray(np.concatenate([tril, tril, tril], axis=1), BF16),
            jnp.asarray(np.stack(masks).astype(np.float32)), jnp.asarray(hm.reshape(2, 1, LANES)), nlev)


def _rwkv_prep(p3, prev, cs, nseq, t):
    m = nseq * t
    tm = RWKV_TM
    blk = min(RWKV_BLOCK, t)
    chained = t >= tm
    tril3, masks, hm, nlev = _rwkv_tables(tm, blk)
    w = RWKV_WIDTH
    row = lambda i: (i, 0)
    whole = lambda i: (0, 0)
    whole3 = lambda i: (0, 0, 0)
    vec = pl.BlockSpec((1, w), whole)
    zspecs = [pl.BlockSpec((tm, S5_WIDTH), lambda i, c=c: (i, 1 + c)) for c in range(RWKV_ZBLOCKS)]
    if chained:
        tps = t // tm
        lead = zspecs + [pl.BlockSpec((SUBLANES, S5_WIDTH),
                                      lambda i, c=c: (jnp.maximum(i * (tm // SUBLANES) - 1, 0), 1 + c))
                         for c in range(RWKV_ZBLOCKS)]
        lead.append(pl.BlockSpec((1, 1, RWKV_SHIFT_WIDTH), lambda i: (i // tps, 0, 0)))
        args = (p3,) * (2 * RWKV_ZBLOCKS) + (prev,)
    else:
        tps = 1
        lead = zspecs + [pl.BlockSpec((tm, RWKV_SHIFT_WIDTH), row)]
        args = (p3,) * RWKV_ZBLOCKS + (prev,)
    return pl.pallas_call(
        functools.partial(_rwkv_prep_kernel, nlev=nlev, chained=chained, tiles_per_seq=tps),
        grid=(m // tm,),
        in_specs=lead + [
            pl.BlockSpec((1, RWKV_SHIFT_WIDTH), whole),
            vec, vec,
            pl.BlockSpec((LANES, w), whole), pl.BlockSpec((LANES, w), whole), pl.BlockSpec((LANES, w), whole),
            vec, vec, vec,
            pl.BlockSpec((2 * SEG_CHUNK, SEG_CHUNK), whole),
            pl.BlockSpec(tril3.shape, whole), pl.BlockSpec(masks.shape, whole3), pl.BlockSpec(hm.shape, whole3),
        ],
        out_specs=[pl.BlockSpec((tm, w), row)] * 10,
        out_shape=[jax.ShapeDtypeStruct((m, w), F32)] * 10,
        compiler_params=_cparams(("parallel",)),
        name="rwkv_prep",
    )(*args, cs["mu"], cs["w0"], cs["a0"], cs["w2"], cs["a2"], cs["g2"], cs["k_k"], cs["k_a"], cs["r_k"],
      cs["e2w"], tril3, masks, hm)


def _rwkv_rec_kernel(*refs, nbp, tc, blk, aliased):
    ap_ref, rp_ref, bt_ref, kp_ref, cc_ref, y0_ref, v_ref, gm_ref, s0_ref, bd_ref = refs[:10]
    y_ref, so_ref, s_ref = refs[10 + (1 if aliased else 0):]
    c = pl.program_id(1)
    hd = RWKV_HEAD
    pairs = [(n, j) for n in range(nbp) for j in range(RWKV_PAIRS)]

    @pl.when(c == 0)
    def _():
        zero = jnp.zeros((hd, hd), F32)
        for idx, (n, j) in enumerate(pairs):
            top = jnp.concatenate([s0_ref[n, 2 * j], zero], axis=1)
            bot = jnp.concatenate([zero, s0_ref[n, 2 * j + 1]], axis=1)
            s_ref[idx] = jnp.concatenate([top, bot], axis=0)

    bd = bd_ref[...]

    def block(bi, carry):
        rows = pl.ds(pl.multiple_of(bi * blk, blk), blk)
        gmats = []
        for idx, (n, j) in enumerate(pairs):
            ls = slice(j * LANES, (j + 1) * LANES)
            sh, sl = _split2(s_ref[idx])
            xh = jnp.concatenate([ap_ref[n, rows, ls], rp_ref[n, rows, ls]], axis=0).astype(BF16)
            gmats.append(_dot_nt(jnp.concatenate([xh, xh], axis=1), jnp.concatenate([sh, sl], axis=1)))
        upds = []
        for idx, (n, j) in enumerate(pairs):
            ls = slice(j * LANES, (j + 1) * LANES)
            gmat = gmats[idx]
            e = -gmat[:blk] - cc_ref[n, rows, ls]
            y_ref[n, rows, ls] = gmat[blk:] + y0_ref[n, rows, ls]
            evh, evl = _split2(jnp.concatenate([e, v_ref[n, rows, ls]], axis=0))
            bkh, bkl = _split2(jnp.concatenate([bt_ref[n, rows, ls], kp_ref[n, rows, ls]], axis=0))
            upds.append(_dot_tn(jnp.concatenate([evh, evh, evl], axis=0), jnp.concatenate([bkh, bkl, bkh], axis=0)))
        for idx, (n, j) in enumerate(pairs):
            ls = slice(j * LANES, (j + 1) * LANES)
            gl = gm_ref[n, rows, ls][blk - 1:blk, :]
            s_ref[idx] = (s_ref[idx] + upds[idx] * bd) * gl
        return carry

    lax.fori_loop(0, tc // blk, block, 0)

    @pl.when(c == pl.num_programs(1) - 1)
    def _():
        for idx, (n, j) in enumerate(pairs):
            s = s_ref[idx]
            so_ref[n, 2 * j] = s[:hd, :hd]
            so_ref[n, 2 * j + 1] = s[hd:, hd:]


def _rwkv_rec(arrs, s0_all, l, so_prev, cs, nseq, t, nbp=4, tc=128):
    w = RWKV_WIDTH
    tc = min(tc, t)
    blk = min(RWKV_BLOCK, t)
    depth = s0_all.shape[0]
    seq_map = lambda i, c: (i, c, 0)
    st_map = lambda i, c: (l, i, 0, 0, 0)
    blkspec = pl.BlockSpec((nbp, tc, w), seq_map)
    st = pl.BlockSpec((None, nbp, RWKV_HEADS, RWKV_HEAD, RWKV_HEAD), st_map)
    aliased = so_prev is not None
    in_specs = [blkspec] * 8 + [st, pl.BlockSpec((LANES, LANES), lambda i, c: (0, 0))]
    args = list(arrs) + [s0_all, cs["bd"]]
    aliases = {}
    if aliased:
        in_specs.append(pl.BlockSpec(memory_space=pl.ANY))
        args.append(so_prev)
        aliases = {10: 1}
    return pl.pallas_call(
        functools.partial(_rwkv_rec_kernel, nbp=nbp, tc=tc, blk=blk, aliased=aliased),
        grid=(nseq // nbp, t // tc),
        in_specs=in_specs,
        out_specs=[blkspec, st],
        out_shape=[jax.ShapeDtypeStruct((nseq, t, w), F32),
                   jax.ShapeDtypeStruct((depth, nseq, RWKV_HEADS, RWKV_HEAD, RWKV_HEAD), F32)],
        scratch_shapes=[pltpu.VMEM((nbp * RWKV_PAIRS, LANES, LANES), F32)],
        input_output_aliases=aliases,
        compiler_params=_cparams(("parallel", "arbitrary")),
        name="rwkv_rec",
    )(*args)


def _rwkv_consts(mu, w0, w2, a0, a2, g2, k_k, k_a, r_k, ln_w, ln_b):
    w = RWKV_WIDTH
    seg = np.arange(SEG_CHUNK) // RWKV_HEAD
    e_w = (seg[:, None] == seg[None, :]).astype(np.float32)
    seg = np.arange(LANES) // RWKV_HEAD
    bd = (seg[:, None] == seg[None, :]).astype(np.float32)
    z64 = jnp.zeros((RWKV_HEAD, w), F32)
    return dict(
        mu=mu.reshape(1, -1), w0=w0.reshape(1, -1), a0=a0.reshape(1, -1),
        w2=jnp.concatenate([w2, z64], axis=0).astype(BF16),
        a2=jnp.concatenate([z64, a2], axis=0).astype(BF16),
        g2=g2.astype(BF16), k_k=k_k.reshape(1, -1), k_a=k_a.reshape(1, -1), r_k=r_k.reshape(1, -1),
        ln_w=ln_w.reshape(1, -1), ln_b=ln_b.reshape(1, -1),
        e2w=jnp.asarray(np.concatenate([e_w, e_w], axis=0), BF16),
        bd=jnp.asarray(bd, F32))


def _rwkv(p3, shift_l, wkv_all, l, so_prev, cs, nseq, t):
    if t >= RWKV_TM:
        prev = shift_l.reshape(nseq, 1, RWKV_SHIFT_WIDTH)
    else:
        z3 = p3.reshape(nseq, t, N_IN)[:, :, S5_WIDTH:S5_WIDTH + RWKV_SHIFT_WIDTH]
        prev = jnp.concatenate([shift_l[:, None, :], z3[:, :-1]], axis=1).reshape(nseq * t, RWKV_SHIFT_WIDTH)
    outs = _rwkv_prep(p3, prev, cs, nseq, t)
    arrs = [a.reshape(nseq, t, RWKV_WIDTH) for a in outs[:8]]
    y, so = _rwkv_rec(arrs, wkv_all, l, so_prev, cs, nseq, t)
    return (y.reshape(nseq * t, RWKV_WIDTH), outs[9], outs[8]), so


def _hgrn_tables(c):
    t = np.arange(c)
    j = np.arange(c)
    mats = [(j[None, :] <= t[:, None]), (j[None, :] > t[:, None])]
    masks = []
    n = 2
    while n <= c:
        m = n // 2
        bs = (t // n) * n
        hi_half = (t % n) >= m
        mats.append((hi_half[:, None] & (j[None, :] >= (bs + m)[:, None]) & (j[None, :] <= t[:, None]))
                    | ((~hi_half)[:, None] & (j[None, :] > t[:, None]) & (j[None, :] <= (bs + m - 1)[:, None])))
        masks.append(((t[:, None] // n) == (t[None, :] // n)) & hi_half[:, None] & (~hi_half)[None, :])
        n *= 2
    w = np.concatenate(mats, axis=0).astype(np.float32)
    w3 = np.concatenate([w, w, w], axis=1)
    return jnp.asarray(w3, BF16), jnp.asarray(np.stack(masks).astype(np.float32)), len(masks)


def _hgrn_kernel(*refs, rb, c, nlev, chained, aliased, sub):
    q_ref, f_ref, i_ref, g_ref, s0_ref, lb_ref, nw_ref, w3_ref, mk_ref = refs[:9]
    o_ref, so_ref, st_ref = refs[9 + (1 if aliased else 0):]
    cidx = pl.program_id(1)
    hdim = HGRN_DIM

    if chained:
        @pl.when(cidx == 0)
        def _():
            for h in range(HGRN_HEADS):
                st_ref[h] = s0_ref[0, h].T

    w3 = w3_ref[...]
    row_i = lax.broadcasted_iota(jnp.int32, (c, c), 0)
    col_i = lax.broadcasted_iota(jnp.int32, (c, c), 1)
    eye = (row_i == col_i).astype(F32)

    heads = range(HGRN_HEADS)

    def lanes(h):
        return slice(h * hdim, (h + 1) * hdim)

    def block(it, carry):
        rows_s = [pl.ds(pl.multiple_of((it * sub + s) * c, c), c) for s in range(sub)]
        items = [(s, h) for s in range(sub) for h in heads]
        fg = {k: lb_ref[:, lanes(k[1])] + (1.0 - lb_ref[:, lanes(k[1])]) * _sigmoid(f_ref[rows_s[k[0]], lanes(k[1])])
              for k in items}
        kk = {k: 1.0 - fg[k] for k in items}
        qs = {}
        for k in items:
            q = q_ref[rows_s[k[0]], lanes(k[1])]
            qs[k] = q * _sigmoid(q)
        ex = {k: jnp.exp(_dot(w3, jnp.concatenate(_split3(jnp.log(fg[k])), axis=0))) for k in items}
        vb = {k: i_ref[rows_s[k[0]], lanes(k[1])].astype(BF16) for k in items}
        lev = {(k, lv): _dot_nt((qs[k] * ex[k][(2 + lv) * c:(3 + lv) * c]).astype(BF16),
                                (kk[k] * ex[k][(2 + lv) * c:(3 + lv) * c]).astype(BF16))
               for k in items for lv in range(nlev)}
        att = {}
        for k in items:
            a = eye * jnp.sum(qs[k] * kk[k], axis=-1, keepdims=True)
            for lv in range(nlev):
                a = a + jnp.where(mk_ref[lv] > 0.0, lev[(k, lv)], 0.0)
            att[k] = a.astype(BF16)
        intra = {k: _dot(att[k], vb[k]) for k in items}
        kv = {k: _dot_tn(vb[k], (kk[k] * ex[k][c:2 * c]).astype(BF16)) for k in items}
        qe = {k: (qs[k] * ex[k][0:c]).astype(BF16) for k in items}
        st = {h: st_ref[h] for h in heads} if chained else None
        for s in range(sub):
            if not chained:
                st = {h: s0_ref[it * sub + s, h].T for h in heads}
            inter = {h: _dot_nt(qe[(s, h)], st[h].astype(BF16)) for h in heads}
            st = {h: st[h] * ex[(s, h)][c - 1:c, :] + kv[(s, h)] for h in heads}
            if not chained:
                for h in heads:
                    so_ref[it * sub + s, h] = st[h].T
            for h in heads:
                o = inter[h] + intra[(s, h)]
                o = o * lax.rsqrt(jnp.mean(o * o, axis=-1, keepdims=True) + HGRN_EPS)
                gg = g_ref[rows_s[s], lanes(h)]
                o_ref[rows_s[s], lanes(h)] = o * nw_ref[:, lanes(h)] * (gg * _sigmoid(gg))
        if chained:
            for h in heads:
                st_ref[h] = st[h]
        return carry

    lax.fori_loop(0, rb // (c * sub), block, 0)

    if chained:
        @pl.when(cidx == pl.num_programs(1) - 1)
        def _():
            for h in range(HGRN_HEADS):
                so_ref[0, h] = st_ref[h].T


def _hgrn(p3, s0_all, l, so_prev, lb, nw, nseq, t):
    rows = nseq * t
    depth = s0_all.shape[0]
    c = math.gcd(t, 64)
    chained = t > c
    w3, masks, nlev = _hgrn_tables(c)
    w = HGRN_WIDTH
    if chained:
        rb = 256
        nblk = t // rb
        grid = (nseq, nblk)
        col = lambda cb: (lambda n, k: (n * nblk + k, cb))
        out_map = lambda n, k: (n * nblk + k, 0)
        nsb = 1
    else:
        nsb = 16
        rb = nsb * c
        grid = (rows // rb, 1)
        col = lambda cb: (lambda n, k: (n, cb))
        out_map = lambda n, k: (n, 0)
    st_spec = pl.BlockSpec((None, nsb, HGRN_HEADS, HGRN_DIM, HGRN_DIM), lambda n, k: (l, n, 0, 0, 0))
    whole2 = lambda n, k: (0, 0)
    whole3 = lambda n, k: (0, 0, 0)
    aliased = so_prev is not None
    in_specs = [pl.BlockSpec((rb, w), col(4)), pl.BlockSpec((rb, w), col(5)), pl.BlockSpec((rb, w), col(6)),
                pl.BlockSpec((rb, w), col(7)), st_spec,
                pl.BlockSpec((1, w), whole2), pl.BlockSpec((1, w), whole2),
                pl.BlockSpec(w3.shape, whole2), pl.BlockSpec(masks.shape, whole3)]
    args = [p3, p3, p3, p3, s0_all, lb, nw, w3, masks]
    aliases = {}
    if aliased:
        in_specs.append(pl.BlockSpec(memory_space=pl.ANY))
        args.append(so_prev)
        aliases = {9: 1}
    return pl.pallas_call(
        functools.partial(_hgrn_kernel, rb=rb, c=c, nlev=nlev, chained=chained, aliased=aliased,
                          sub=2 if chained else 4),
        grid=grid,
        in_specs=in_specs,
        out_specs=[pl.BlockSpec((rb, w), out_map), st_spec],
        out_shape=[jax.ShapeDtypeStruct((rows, w), F32),
                   jax.ShapeDtypeStruct((depth, nseq, HGRN_HEADS, HGRN_DIM, HGRN_DIM), F32)],
        scratch_shapes=[pltpu.VMEM((HGRN_HEADS, HGRN_DIM, HGRN_DIM), F32)],
        input_output_aliases=aliases,
        compiler_params=_cparams(("parallel", "arbitrary")),
        name="hgrn_chain" if chained else "hgrn_tile",
    )(*args)


def _run(x, nseq, t, states, lw, ffw, norm_final):
    s5_re0, s5_im0, shift0, wkv0, hgrn0 = states
    depth = len(lw)
    s5_re0 = s5_re0.reshape(depth, nseq, S5_LANES)
    s5_im0 = s5_im0.reshape(depth, nseq, S5_LANES)
    s5_re_l, s5_im_l, shift_l = [], [], []
    wkv_out = None
    hgrn_out = None
    cast = not isinstance(ffw["f1g"], list)
    ffb = {k: [] for k in ffw}

    def ffn(x, nw, names, l, fnw=None):
        if not cast:
            return _ffn(x, nw, *(ffw[k][l][None] for k in names), 0, fnw=fnw)
        res = _ffn(x, nw, *(ffw[k] for k in names), l, cast=True, fnw=fnw)
        for k, wb in zip(names, res[1:]):
            ffb[k].append(wb)
        return res[0]

    for l in range(depth):
        w = lw[l]
        x = ffn(x, w["norm_ffn1"], ("f1g", "f1u", "f1d"), l)
        if cast:
            p3, wb = _inproj(x, w["norm_mix"], ffw["w_in"], l, cast=True)
            ffb["w_in"].append(wb)
        else:
            p3 = _inproj(x, w["norm_mix"], ffw["w_in"][l][None], 0)
        y_s5, s_re, s_im = _s5(p3, s5_re0, s5_im0, l, w["s5"], nseq, t)
        (y_rw, bonus, gate), wkv_out = _rwkv(p3, shift0[l], wkv0, l, wkv_out, w["rwkv"], nseq, t)
        y_hg, hgrn_out = _hgrn(p3, hgrn0, l, hgrn_out, w["hgrn_lb"], w["hgrn_nw"], nseq, t)
        x = _outproj(x, y_s5, y_rw, bonus, gate, y_hg, w["w_out"], l, w["rwkv"])
        x = ffn(x, w["norm_ffn2"], ("f2g", "f2u", "f2d"), l,
                fnw=norm_final.reshape(1, -1) if l == depth - 1 else None)
        s5_re_l.append(s_re)
        s5_im_l.append(s_im)
        shift_l.append(p3.reshape(nseq, t, N_IN)[:, t - 1, S5_WIDTH:S5_WIDTH + RWKV_SHIFT_WIDTH])
    return (x,jnp.stack(s5_re_l), jnp.stack(s5_im_l), jnp.stack(shift_l), wkv_out, hgrn_out), ffb


def kernel(x_prompt, x_sample, state_s5_re, state_s5_im, state_rwkv_shift, state_rwkv_wkv, state_hgrn, norm_ffn1, ffn1_w_gate, ffn1_w_up, ffn1_w_down, norm_mix, w_in, s5_a_re, s5_a_im, s5_log_dt, s5_b_re, s5_b_im, s5_c_re, s5_c_im, s5_d, s5_w_glu, s5_b_glu, rwkv_mu, rwkv_w0, rwkv_w2, rwkv_a0, rwkv_a2, rwkv_g2, rwkv_k_k, rwkv_k_a, rwkv_r_k, rwkv_ln_w, rwkv_ln_b, hgrn_lb_raw, hgrn_norm_w, w_out, norm_ffn2, ffn2_w_gate, ffn2_w_up, ffn2_w_down, norm_final):
    depth = w_in.shape[0]
    nb, seq, d = x_prompt.shape
    ns, dseq, _ = x_sample.shape

    p_lb = jax.nn.softmax(hgrn_lb_raw.astype(F32), axis=0)
    lower_bounds = jnp.cumsum(p_lb, axis=0) - p_lb[0]

    w_out_b = w_out.astype(BF16)
    s5_all = jax.vmap(_s5_consts)(s5_a_re, s5_a_im, s5_log_dt, s5_b_re, s5_b_im, s5_c_re, s5_c_im,
                                  s5_d, s5_w_glu, s5_b_glu)
    lw = []
    for l in range(depth):
        lw.append(dict(
            w_out=w_out_b,
            norm_ffn1=norm_ffn1[l].reshape(1, -1), norm_mix=norm_mix[l].reshape(1, -1),
            norm_ffn2=norm_ffn2[l].reshape(1, -1),
            s5=s5_all,
            rwkv=_rwkv_consts(rwkv_mu[l], rwkv_w0[l], rwkv_w2[l], rwkv_a0[l], rwkv_a2[l], rwkv_g2[l],
                              rwkv_k_k[l], rwkv_k_a[l], rwkv_r_k[l], rwkv_ln_w[l], rwkv_ln_b[l]),
            hgrn_lb=lower_bounds[l].reshape(1, -1), hgrn_nw=hgrn_norm_w[l].reshape(1, -1),
        ))

    def zeros_like_state(s):
        return jnp.zeros((depth, nb) + s.shape[2:], F32)

    p_states = tuple(zeros_like_state(s) for s in
                     (state_s5_re, state_s5_im, state_rwkv_shift, state_rwkv_wkv, state_hgrn))
    s_states = (state_s5_re, state_s5_im, state_rwkv_shift, state_rwkv_wkv, state_hgrn)
    ffw = dict(f1g=ffn1_w_gate, f1u=ffn1_w_up, f1d=ffn1_w_down, f2g=ffn2_w_gate, f2u=ffn2_w_up, f2d=ffn2_w_down,
               w_in=w_in)
    (y_s, s5re_s, s5im_s, shift_s, wkv_s, hgrn_s), ffb = _run(
        x_sample.reshape(ns * dseq, d), ns, dseq, s_states, lw, ffw, norm_final)
    (y_p, s5re_p, s5im_p, shift_p, wkv_p, hgrn_p), _ = _run(
        x_prompt.reshape(nb * seq, d), nb, seq, p_states, lw, ffb, norm_final)
    return (y_p.reshape(nb, seq, d), y_s.reshape(ns, dseq, d), s5re_p, s5im_p, shift_p, wkv_p, hgrn_p,
            s5re_s, s5im_s, shift_s, wkv_s, hgrn_s)
```

```python
import functools
import math

import numpy as np
import jax
import jax.numpy as jnp
from jax import lax
from jax.experimental import pallas as pl
from jax.experimental.pallas import tpu as pltpu

F32 = jnp.float32
BF16 = jnp.bfloat16

NORM_EPS = 1e-6
RWKV_GN_EPS = 64e-5
HGRN_EPS = 1e-5

D_MODEL = 2048
S5_WIDTH = 512
S5_GROUP = 16
S5_GROUPS = 32
S5_STATE = 64
S5_LANES = S5_GROUPS * S5_STATE
S5_GB = 4
S5_TR = 512
S5_SEG = S5_TR // 8
RWKV_WIDTH = 768
RWKV_HEAD = 64
RWKV_HEADS = 12
RWKV_PAIRS = 6
RWKV_SHIFT_WIDTH = 2560
RWKV_ZBLOCKS = RWKV_SHIFT_WIDTH // S5_WIDTH
RWKV_BLOCK = 16
RWKV_TM = 128
SEG_CHUNK = 256
HGRN_WIDTH = 768
HGRN_HEADS = 6
HGRN_DIM = 128
N_IN = 6144

SUBLANES = 8
LANES = 128
VMEM_LIMIT = 60 * 1024 * 1024


def _cparams(sem):
    return pltpu.CompilerParams(dimension_semantics=sem, vmem_limit_bytes=VMEM_LIMIT)


def _split2(x):
    hi = x.astype(BF16)
    lo = (x - hi.astype(F32)).astype(BF16)
    return hi, lo


def _split3(x):
    p1 = x.astype(BF16)
    r1 = x - p1.astype(F32)
    p2 = r1.astype(BF16)
    p3 = (r1 - p2.astype(F32)).astype(BF16)
    return p1, p2, p3


def _dot(a, b):
    return jnp.dot(a, b, preferred_element_type=F32)


def _dot_nt(a, b):
    return lax.dot_general(a, b, (((1,), (1,)), ((), ())), preferred_element_type=F32)


def _dot_tn(a, b):
    return lax.dot_general(a, b, (((0,), (0,)), ((), ())), preferred_element_type=F32)


def _sigmoid(x):
    return 1.0 / (1.0 + jnp.exp(-x))


def _ffn_kernel(*refs, cast, final):
    x_ref, nw_ref, wg_ref, wu_ref, wd_ref = refs[:5]
    refs = refs[5:]
    if final:
        fnw_ref = refs[0]
        refs = refs[1:]
    if cast:
        o_ref, wgb_ref, wub_ref, wdb_ref, h_ref = refs
    else:
        o_ref, h_ref = refs
    j = pl.program_id(1)

    @pl.when(j == 0)
    def _():
        rows = o_ref.shape[0]
        step = min(rows, 256)
        for r0 in range(0, rows, step):
            rs = slice(r0, r0 + step)
            x = x_ref[rs, :]
            ms = jnp.mean(x * x, axis=-1, keepdims=True)
            h_ref[rs, :] = (x * lax.rsqrt(ms + NORM_EPS) * nw_ref[...]).astype(BF16)
        o_ref[...] = jnp.zeros_like(o_ref)

    wg = wg_ref[...].astype(BF16)
    wu = wu_ref[...].astype(BF16)
    wd = wd_ref[...].astype(BF16)
    if cast:
        wgb_ref[...] = wg
        wub_ref[...] = wu
        wdb_ref[...] = wd
    h = h_ref[...]
    g = _dot(h, wg)
    u = _dot(h, wu)
    o_ref[...] += _dot((g * _sigmoid(g) * u).astype(BF16), wd)

    @pl.when(j == pl.num_programs(1) - 1)
    def _():
        rows = o_ref.shape[0]
        step = min(rows, 256)
        for r0 in range(0, rows, step):
            rs = slice(r0, r0 + step)
            o = x_ref[rs, :] + 0.5 * o_ref[rs, :]
            if final:
                o = o * lax.rsqrt(jnp.mean(o * o, axis=-1, keepdims=True) + NORM_EPS) * fnw_ref[...]
            o_ref[rs, :] = o


def _ffn(x, nw, wg, wu, wd, l, cast=False, fnw=None, tm=1024):
    m, d = x.shape
    ff = wg.shape[2]
    tf = 256 if cast else 512
    final = fnw is not None
    if final and not cast:
        tm = tm // 2
    out_specs = [pl.BlockSpec((tm, d), lambda i, j: (i, 0))]
    out_shape = [jax.ShapeDtypeStruct((m, d), F32)]
    if cast:
        assert m == tm
        out_specs += [pl.BlockSpec((d, tf), lambda i, j: (0, j)), pl.BlockSpec((d, tf), lambda i, j: (0, j)),
                      pl.BlockSpec((tf, d), lambda i, j: (j, 0))]
        out_shape += [jax.ShapeDtypeStruct((d, ff), BF16), jax.ShapeDtypeStruct((d, ff), BF16),
                      jax.ShapeDtypeStruct((ff, d), BF16)]
    vec = pl.BlockSpec((1, d), lambda i, j: (0, 0))
    outs = pl.pallas_call(
        functools.partial(_ffn_kernel, cast=cast, final=final),
        grid=(m // tm, ff // tf),
        in_specs=[
            pl.BlockSpec((tm, d), lambda i, j: (i, 0)),
            vec,
            pl.BlockSpec((None, d, tf), lambda i, j: (l, 0, j)),
            pl.BlockSpec((None, d, tf), lambda i, j: (l, 0, j)),
            pl.BlockSpec((None, tf, d), lambda i, j: (l, j, 0)),
        ] + ([vec] if final else []),
        out_specs=out_specs,
        out_shape=out_shape,
        scratch_shapes=[pltpu.VMEM((tm, d), BF16)],
        compiler_params=_cparams(("parallel", "arbitrary")),
        name="ffn_cast" if cast else "ffn",
    )(x, nw, wg, wu, wd, *((fnw,) if final else ()))
    return outs if cast else outs[0]


def _inproj_kernel(*refs, cast):
    if cast:
        x_ref, nw_ref, w_ref, o_ref, wb_ref, h_ref = refs
    else:
        x_ref, nw_ref, w_ref, o_ref, h_ref = refs

    @pl.when(pl.program_id(1) == 0)
    def _():
        x = x_ref[...]
        ms = jnp.mean(x * x, axis=-1, keepdims=True)
        h_ref[...] = (x * lax.rsqrt(ms + NORM_EPS) * nw_ref[...]).astype(BF16)

    w = w_ref[...].astype(BF16)
    if cast:
        wb_ref[...] = w
    o_ref[...] = _dot(h_ref[...], w)


def _inproj(x, nw, w, l, cast=False, tm=1024):
    m, d = x.shape
    n = w.shape[2]
    tn = 512 if cast else 1024
    out_specs = [pl.BlockSpec((tm, tn), lambda i, j: (i, j))]
    out_shape = [jax.ShapeDtypeStruct((m, n), F32)]
    if cast:
        assert m == tm
        out_specs.append(pl.BlockSpec((d, tn), lambda i, j: (0, j)))
        out_shape.append(jax.ShapeDtypeStruct((d, n), BF16))
    outs = pl.pallas_call(
        functools.partial(_inproj_kernel, cast=cast),
        grid=(m // tm, n // tn),
        in_specs=[
            pl.BlockSpec((tm, d), lambda i, j: (i, 0)),
            pl.BlockSpec((1, d), lambda i, j: (0, 0)),
            pl.BlockSpec((None, d, tn), lambda i, j: (l, 0, j)),
        ],
        out_specs=out_specs,
        out_shape=out_shape,
        scratch_shapes=[pltpu.VMEM((tm, d), BF16)],
        compiler_params=_cparams(("parallel", "arbitrary")),
        name="inproj_cast" if cast else "inproj",
    )(x, nw, w)
    return outs if cast else outs[0]


def _outproj_kernel(x_ref, a_ref, y_ref, bonus_ref, g_ref, c_ref, w_ref, lnw_ref, lnb_ref, e2_ref, o_ref):
    y = y_ref[...]
    e2 = e2_ref[...]
    mean = _seg_sum(y, e2) * (1.0 / RWKV_HEAD)
    yc = y - mean
    var = _seg_sum(yc * yc, e2) * (1.0 / RWKV_HEAD)
    b = (yc * lax.rsqrt(var + RWKV_GN_EPS) * lnw_ref[...] + lnb_ref[...] + bonus_ref[...]) * g_ref[...]
    o1 = S5_WIDTH
    o2 = S5_WIDTH + RWKV_WIDTH
    acc = _dot(a_ref[...].astype(BF16), w_ref[0:o1, :])
    acc += _dot(b.astype(BF16), w_ref[o1:o2, :])
    acc += _dot(c_ref[...].astype(BF16), w_ref[o2:, :])
    o_ref[...] = x_ref[...] + acc


def _outproj(x, ya, y_rw, bonus, g, yc, w_all, l, cs, tm=512):
    m, d = x.shape
    row = lambda i: (i, 0)
    whole = lambda i: (0, 0)
    wide = lambda a: pl.BlockSpec((tm, a.shape[1]), row)
    vec = pl.BlockSpec((1, RWKV_WIDTH), whole)
    return pl.pallas_call(
        _outproj_kernel,
        grid=(m // tm,),
        in_specs=[
            pl.BlockSpec((tm, d), row), wide(ya), wide(y_rw), wide(bonus), wide(g), wide(yc),
            pl.BlockSpec((None,) + w_all.shape[1:], lambda i: (l, 0, 0)),
            vec, vec, pl.BlockSpec((2 * SEG_CHUNK, SEG_CHUNK), whole),
        ],
        out_specs=pl.BlockSpec((tm, d), row),
        out_shape=jax.ShapeDtypeStruct((m, d), F32),
        compiler_params=_cparams(("parallel",)),
        name="outproj",
    )(x, ya, y_rw, bonus, g, yc, w_all, cs["ln_w"], cs["ln_b"], cs["e2w"])


def _s5_kernel(u_ref, x0r_ref, x0i_ref, b3r_ref, b3i_ref, cr_ref, ci_ref, cst_ref, d_ref, wglu_ref,
               bglu_ref, y_ref, xlr_ref, xli_ref, xr_ref, xi_ref, cr_s, ci_s, pm_ref, *, tr, chained):
    c = pl.program_id(1)
    nlt = S5_LANES // LANES
    per_gb = nlt // S5_GB

    def load_tile(ref, rows):
        return jnp.concatenate([ref[k, rows, :] for k in range(nlt)], axis=1)

    def store_tile(ref, rows, val):
        for k in range(nlt):
            ref[k, rows, :] = val[:, k * LANES:(k + 1) * LANES]

    u = u_ref[...]
    nut = S5_WIDTH // LANES
    seg = tr // SUBLANES
    if chained:
        for k in range(nut):
            pm_ref[k] = u[:, k * LANES:(k + 1) * LANES]
        u = jnp.concatenate(
            [jnp.concatenate([pm_ref[k, pl.ds(tau, SUBLANES, stride=seg), :] for tau in range(seg)], axis=0)
             for k in range(nut)], axis=1)
    uh, ul = _split2(u)
    for gb in range(S5_GB):
        sl = slice(gb * LANES, (gb + 1) * LANES)
        if chained:
            br = _dot(uh[:, sl], b3r_ref[gb, 0:LANES, :])
            bi = _dot(uh[:, sl], b3i_ref[gb, 0:LANES, :])
        else:
            lhs = jnp.concatenate([uh[:, sl], ul[:, sl], uh[:, sl]], axis=1)
            br = _dot(lhs, b3r_ref[gb])
            bi = _dot(lhs, b3i_ref[gb])
        for k in range(per_gb):
            xr_ref[gb * per_gb + k] = br[:, k * LANES:(k + 1) * LANES]
            xi_ref[gb * per_gb + k] = bi[:, k * LANES:(k + 1) * LANES]

    if chained:
        @pl.when(c == 0)
        def _():
            cr_s[...] = x0r_ref[0]
            ci_s[...] = x0i_ref[0]

    a1r, a1i, a2r, a2i, a4r, a4i, pr, pi = [cst_ref[i] for i in range(8)]

    def tile_scan(xr, xi, x0r, x0i, lv1, lv2, lv4, pw):
        for k, (ar, ai) in ((1, lv1), (2, lv2), (4, lv4)):
            sr = pltpu.roll(xr, k, 0)
            si = pltpu.roll(xi, k, 0)
            xr, xi = xr + ar * sr - ai * si, xi + ar * si + ai * sr
        x0r = jnp.broadcast_to(x0r, xr.shape)
        x0i = jnp.broadcast_to(x0i, xr.shape)
        return xr + pw[0] * x0r - pw[1] * x0i, xi + pw[0] * x0i + pw[1] * x0r

    if chained:
        seg = tr // SUBLANES
        b1r, b1i, b2r, b2i, b4r, b4i, qr64, qi64 = [cst_ref[8 + i] for i in range(8)]
        lt = lambda x, k: x[:, k * LANES:(k + 1) * LANES]
        ar = [jnp.broadcast_to(lt(pr, k)[0:1, :], (SUBLANES, LANES)) for k in range(nlt)]
        ai = [jnp.broadcast_to(lt(pi, k)[0:1, :], (SUBLANES, LANES)) for k in range(nlt)]

        def sweep1(tau, carry):
            rows = pl.ds(pl.multiple_of(tau * SUBLANES, SUBLANES), SUBLANES)
            out = []
            for k in range(nlt):
                xr, xi = carry[2 * k], carry[2 * k + 1]
                xr, xi = (ar[k] * xr - ai[k] * xi + xr_ref[k, rows, :], ar[k] * xi + ai[k] * xr + xi_ref[k, rows, :])
                xr_ref[k, rows, :] = xr
                xi_ref[k, rows, :] = xi
                out += [xr, xi]
            return tuple(out)

        zero = jnp.zeros((SUBLANES, LANES), F32)
        ends = lax.fori_loop(0, seg, sweep1, (zero,) * (2 * nlt))
        er = jnp.concatenate(ends[0::2], axis=1)
        ei = jnp.concatenate(ends[1::2], axis=1)
        x0r = cr_s[...]
        x0i = ci_s[...]
        yr, yi = tile_scan(er, ei, x0r, x0i, (b1r, b1i), (b2r, b2i), (b4r, b4i), (qr64, qi64))
        cr_s[...] = yr[SUBLANES - 1:SUBLANES, :]
        ci_s[...] = yi[SUBLANES - 1:SUBLANES, :]
        rid = lax.broadcasted_iota(jnp.int32, pr.shape, 0)
        sr0 = jnp.where(rid == 0, jnp.broadcast_to(x0r, pr.shape), pltpu.roll(yr, 1, 0))
        si0 = jnp.where(rid == 0, jnp.broadcast_to(x0i, pr.shape), pltpu.roll(yi, 1, 0))

        def sweep2(tau, carry):
            rows = pl.ds(pl.multiple_of(tau * SUBLANES, SUBLANES), SUBLANES)
            out = []
            for k in range(nlt):
                qr, qi = carry[2 * k], carry[2 * k + 1]
                qr, qi = ar[k] * qr - ai[k] * qi, ar[k] * qi + ai[k] * qr
                xr_ref[k, rows, :] = xr_ref[k, rows, :] + qr
                xi_ref[k, rows, :] = xi_ref[k, rows, :] + qi
                out += [qr, qi]
            return tuple(out)

        start = []
        for k in range(nlt):
            start += [lt(sr0, k), lt(si0, k)]
        lax.fori_loop(0, seg, sweep2, tuple(start))

        @pl.when(c == pl.num_programs(1) - 1)
        def _():
            xlr_ref[0] = cr_s[...]
            xli_ref[0] = ci_s[...]
    else:
        def tile(j, carry):
            rows = pl.ds(pl.multiple_of(j * SUBLANES, SUBLANES), SUBLANES)
            xr, xi = tile_scan(load_tile(xr_ref, rows), load_tile(xi_ref, rows),
                               x0r_ref[pl.ds(j, 1), :], x0i_ref[pl.ds(j, 1), :],
                               (a1r, a1i), (a2r, a2i), (a4r, a4i), (pr, pi))
            store_tile(xr_ref, rows, xr)
            store_tile(xi_ref, rows, xi)
            xlr_ref[pl.ds(j, 1), :] = xr[SUBLANES - 1:SUBLANES, :]
            xli_ref[pl.ds(j, 1), :] = xi[SUBLANES - 1:SUBLANES, :]
            return carry

        lax.fori_loop(0, tr // SUBLANES, tile, 0)

    ys = []
    for gb in range(S5_GB):
        xr = jnp.concatenate([xr_ref[gb * per_gb + k] for k in range(per_gb)], axis=1)
        xi = jnp.concatenate([xi_ref[gb * per_gb + k] for k in range(per_gb)], axis=1)
        ys.append(_dot(xr.astype(BF16), cr_ref[gb]) + _dot(xi.astype(BF16), ci_ref[gb]))
    y = jnp.concatenate(ys, axis=1) + d_ref[...] * u
    y = 0.5 * y * (1.0 + jnp.tanh(math.sqrt(2.0 / math.pi) * (y + 0.044715 * (y * y * y))))
    z = _dot(y.astype(BF16), wglu_ref[...]) + bglu_ref[...]
    y = y * _sigmoid(z)
    if chained:
        for k in range(nut):
            pm_ref[k] = y[:, k * LANES:(k + 1) * LANES]
        for m in range(tr // SUBLANES):
            start = (m % SUBLANES) * SUBLANES * SUBLANES + m // SUBLANES
            for k in range(nut):
                y_ref[m * SUBLANES:(m + 1) * SUBLANES, k * LANES:(k + 1) * LANES] = (
                    pm_ref[k, pl.ds(start, SUBLANES, stride=SUBLANES), :])
    else:
        y_ref[...] = y


def _s5(p3, x0r, x0i, l, cs, nseq, t, tr=S5_TR):
    rows = nseq * t
    depth = x0r.shape[0]
    chained = t > SUBLANES
    if chained:
        grid = (nseq, t // tr)
        row_map = lambda n, c: (n * (t // tr) + c, 0)
        out_map = lambda n, c: (n * (t // tr) + c, 0)
        x0r = x0r.reshape(depth, nseq, 1, S5_LANES)
        x0i = x0i.reshape(depth, nseq, 1, S5_LANES)
        in_st = pl.BlockSpec((None, 1, 1, S5_LANES), lambda n, c: (l, n, 0, 0))
        st_spec = pl.BlockSpec((1, 1, S5_LANES), lambda n, c: (n, 0, 0))
        st_shape = jax.ShapeDtypeStruct((nseq, 1, S5_LANES), F32)
    else:
        assert t == SUBLANES
        grid = (rows // tr, 1)
        row_map = lambda n, c: (n, 0)
        out_map = lambda n, c: (n, 0)
        in_st = pl.BlockSpec((None, tr // SUBLANES, S5_LANES), lambda n, c: (l, n, 0))
        st_spec = pl.BlockSpec((tr // SUBLANES, S5_LANES), lambda n, c: (n, 0))
        st_shape = jax.ShapeDtypeStruct((nseq, S5_LANES), F32)
    names = ("b3r", "b3i", "cr", "ci", "scan", "d", "wglu", "bglu")
    layer = lambda a: pl.BlockSpec((None,) + a.shape[1:], lambda n, c: (l,) + (0,) * (a.ndim - 1))
    y, xlr, xli = pl.pallas_call(
        functools.partial(_s5_kernel, tr=tr, chained=chained),
        grid=grid,
        in_specs=[pl.BlockSpec((tr, S5_WIDTH), row_map), in_st, in_st] + [layer(cs[k]) for k in names],
        out_specs=[pl.BlockSpec((tr, S5_WIDTH), out_map), st_spec, st_spec],
        out_shape=[jax.ShapeDtypeStruct((rows, S5_WIDTH), F32), st_shape, st_shape],
        scratch_shapes=[pltpu.VMEM((S5_LANES // LANES, tr, LANES), F32), pltpu.VMEM((S5_LANES // LANES, tr, LANES), F32),
                        pltpu.VMEM((1, S5_LANES), F32), pltpu.VMEM((1, S5_LANES), F32),
                        pltpu.VMEM((S5_WIDTH // LANES, tr, LANES), F32)],
        compiler_params=_cparams(("parallel", "arbitrary")),
        name="s5_chain" if chained else "s5_tile",
    )(p3, x0r, x0i, *(cs[k] for k in names))
    return y, xlr.reshape(nseq, S5_GROUPS, S5_STATE), xli.reshape(nseq, S5_GROUPS, S5_STATE)


def _s5_consts(a_re, a_im, log_dt, b_re, b_im, c_re, c_im, d, w_glu, b_glu):
    g, p, h = S5_GROUPS, S5_STATE, S5_GROUP
    dt = jnp.exp(log_dt)[:, None]
    mag = jnp.exp(a_re * dt)
    abr = mag * jnp.cos(a_im * dt)
    abi = mag * jnp.sin(a_im * dt)
    den = a_re * a_re + a_im * a_im
    qr = ((abr - 1.0) * a_re + abi * a_im) / den
    qi = (abi * a_re - (abr - 1.0) * a_im) / den
    bbr = qr[..., None] * b_re - qi[..., None] * b_im
    bbi = qr[..., None] * b_im + qi[..., None] * b_re
    row = jnp.arange(SUBLANES)[:, None]

    def scan_tables(mr, mi):
        pr, pi = [mr], [mi]
        for _ in range(SUBLANES - 1):
            pr, pi = pr + [pr[-1] * mr - pi[-1] * mi], pi + [pr[-1] * mi + pi[-1] * mr]
        pr = jnp.stack(pr).reshape(SUBLANES, g * p)
        pi = jnp.stack(pi).reshape(SUBLANES, g * p)

        def lvl(k):
            return [jnp.where(row >= k, pr[k - 1][None, :], 0.0), jnp.where(row >= k, pi[k - 1][None, :], 0.0)]

        return lvl(1) + lvl(2) + lvl(4) + [pr, pi]

    sr, si = abr, abi
    for _ in range(S5_SEG.bit_length() - 1):
        sr, si = sr * sr - si * si, 2.0 * sr * si
    scan = jnp.stack(scan_tables(abr, abi) + scan_tables(sr, si)).astype(F32)
    eye = jnp.eye(SUBLANES, dtype=F32)

    def in_blocks(x):
        xx = x.reshape(S5_GB, SUBLANES, p, h).transpose(0, 1, 3, 2)
        blk = (xx[:, :, :, None, :] * eye[None, :, None, :, None]).reshape(S5_GB, SUBLANES * h, SUBLANES * p)
        hi = blk.astype(BF16)
        lo = (blk - hi.astype(F32)).astype(BF16)
        return jnp.concatenate([hi, hi, lo], axis=1)

    def out_blocks(x):
        xx = x.reshape(S5_GB, SUBLANES, h, p).transpose(0, 1, 3, 2)
        return (xx[:, :, :, None, :] * eye[None, :, None, :, None]).reshape(
            S5_GB, SUBLANES * p, SUBLANES * h).astype(BF16)

    return dict(b3r=in_blocks(bbr), b3i=in_blocks(bbi),
                cr=out_blocks(c_re), ci=out_blocks(-c_im), scan=scan,
                d=d.reshape(1, -1), wglu=w_glu.astype(BF16), bglu=b_glu.reshape(1, -1))


def _seg_sum(x, e2):
    hi, lo = _split2(x)
    cw = e2.shape[1]
    return jnp.concatenate(
        [_dot(jnp.concatenate([hi[:, c:c + cw], lo[:, c:c + cw]], axis=1), e2) for c in range(0, x.shape[1], cw)],
        axis=1)


def _rwkv_prep_kernel(*refs, nlev, chained, tiles_per_seq):
    nz = RWKV_ZBLOCKS
    z = jnp.concatenate([r[...] for r in refs[:nz]], axis=1)
    if chained:
        prev8 = jnp.concatenate([r[...] for r in refs[nz:2 * nz]], axis=1)
        sh_ref = refs[2 * nz]
        refs = refs[2 * nz + 1:]
    else:
        zp_ref = refs[nz]
        refs = refs[nz + 1:]
    (mu_ref, w0_ref, a0_ref, w2_ref, a2_ref, g2_ref, kk_ref, ka_ref, rk_ref, e2_ref, tril_ref, mk_ref, hm_ref,
     ap_o, rp_o, bt_o, kp_o, cc_o, y0_o, v_o, gm_o, g_o, bonus_o) = refs
    tm = RWKV_TM
    if chained:
        first = (pl.program_id(0) % tiles_per_seq) == 0
        prev_row = jnp.where(first, sh_ref[0], prev8[SUBLANES - 1:SUBLANES, :])
        rid = lax.broadcasted_iota(jnp.int32, z.shape, 0)
        zp = jnp.where(rid == 0, prev_row, pltpu.roll(z, 1, 0))
    else:
        zp = zp_ref[...]
    zm = z + (zp - z) * mu_ref[...]
    w = RWKV_WIDTH
    r = zm[:, 0:w]
    k = zm[:, w:2 * w]
    v = zm[:, 2 * w:3 * w]
    wa = zm[:, 3 * w:3 * w + LANES]
    gi = zm[:, 3 * w + LANES:3 * w + 2 * LANES]
    e2 = e2_ref[...]
    wl = w0_ref[...] + _dot(jnp.tanh(wa).astype(BF16), w2_ref[...])
    sp = jnp.maximum(-wl, 0.0) + jnp.log(1.0 + jnp.exp(-jnp.abs(wl)))
    ld = -jnp.exp(-sp - 0.5)
    a = _sigmoid(a0_ref[...] + _dot(wa.astype(BF16), a2_ref[...]))
    g_o[...] = _dot(_sigmoid(gi).astype(BF16), g2_ref[...])
    kk = k * kk_ref[...]
    kk = kk * lax.rsqrt(jnp.maximum(_seg_sum(kk * kk, e2), 1e-24))
    k2 = k * (1.0 + (a - 1.0) * ka_ref[...])
    b = kk * a
    bonus_o[...] = _seg_sum(r * k2 * rk_ref[...], e2) * v
    v_o[...] = v

    p1, p2, p3 = _split3(ld)
    cum = _dot(tril_ref[...], jnp.concatenate([p1, p2, p3], axis=0))
    gam = jnp.exp(cum)
    ginv = jnp.exp(-cum)
    gm_o[...] = gam
    alpha = kk * jnp.exp(cum - ld)
    beta = b * ginv
    kap = k2 * ginv
    rho = r * gam
    bt_o[...] = beta
    kp_o[...] = kap

    strict = mk_ref[0]
    incl = mk_ref[1]
    eye = mk_ref[2]
    heads = [(j, e) for j in range(RWKV_PAIRS) for e in range(2)]
    nh = len(heads)

    def lanes(j):
        return slice(j * LANES, (j + 1) * LANES)

    al = [alpha[:, lanes(j)] * hm_ref[e] for j, e in heads]
    rh = [rho[:, lanes(j)] * hm_ref[e] for j, e in heads]
    vh = [(v[:, lanes(j)] * hm_ref[e]).astype(BF16) for j, e in heads]
    yk = [jnp.concatenate([beta[:, lanes(j)], kap[:, lanes(j)]], axis=0).astype(BF16) for j in range(RWKV_PAIRS)]
    wu = [_dot_nt(jnp.concatenate(_split2(al[i]), axis=1), jnp.concatenate([yk[j], yk[j]], axis=1))
          for i, (j, e) in enumerate(heads)]
    wl = [_dot_nt(rh[i].astype(BF16), yk[j]) for i, (j, e) in enumerate(heads)]
    mm = [wu[i][:, :tm] * strict for i in range(nh)]
    nn = [wu[i][:, tm:] * strict for i in range(nh)]
    pp = [(wl[i][:, :tm] * incl).astype(BF16) for i in range(nh)]
    qq = [(wl[i][:, tm:] * incl).astype(BF16) for i in range(nh)]
    tinv = [eye - mm[i] * mk_ref[3] for i in range(nh)]
    for lv in range(1, nlev):
        th = [tinv[i].astype(BF16) for i in range(nh)]
        tmp = [_dot(th[i], (mm[i] * mk_ref[3 + lv]).astype(BF16)) for i in range(nh)]
        tinv = [tinv[i] - _dot(tmp[i].astype(BF16), th[i]) for i in range(nh)]
    th = [tinv[i].astype(BF16) for i in range(nh)]
    nv = [_dot(nn[i].astype(BF16), vh[i]) for i in range(nh)]
    qv = [_dot(qq[i], vh[i]) for i in range(nh)]
    tac = [_dot(th[i], jnp.concatenate([al[i], nv[i]], axis=1).astype(BF16)) for i in range(nh)]
    pac = [_dot(pp[i], tac[i].astype(BF16)) for i in range(nh)]
    ap = [tac[i][:, :LANES] for i in range(nh)]
    cc = [tac[i][:, LANES:] for i in range(nh)]
    rp = [rh[i] - pac[i][:, :LANES] for i in range(nh)]
    y0 = [qv[i] - pac[i][:, LANES:] for i in range(nh)]
    for j in range(RWKV_PAIRS):
        ap_o[:, lanes(j)] = ap[2 * j] + ap[2 * j + 1]
        rp_o[:, lanes(j)] = rp[2 * j] + rp[2 * j + 1]
        cc_o[:, lanes(j)] = cc[2 * j] + cc[2 * j + 1]
        y0_o[:, lanes(j)] = y0[2 * j] + y0[2 * j + 1]


def _rwkv_tables(tm, blk):
    t = np.arange(tm)
    same = (t[:, None] // blk) == (t[None, :] // blk)
    tril = (same & (t[None, :] <= t[:, None])).astype(np.float32)
    masks = [same & (t[None, :] < t[:, None]), same & (t[None, :] <= t[:, None]), np.eye(tm, dtype=bool)]
    n = 2
    while n <= blk:
        m = n // 2
        masks.append(((t[:, None] // n) == (t[None, :] // n)) & ((t[:, None] % n) >= m) & ((t[None, :] % n) < m))
        n *= 2
    nlev = len(masks) - 3
    hm = np.stack([(np.arange(LANES) < RWKV_HEAD), (np.arange(LANES) >= RWKV_HEAD)]).astype(np.float32)
    return (jnp.asarray(np.concatenate([tril, tril, tril], axis=1), BF16),
            jnp.asarray(np.stack(masks).astype(np.float32)), jnp.asarray(hm.reshape(2, 1, LANES)), nlev)


def _rwkv_prep(p3, prev, cs, nseq, t):
    m = nseq * t
    tm = RWKV_TM
    blk = min(RWKV_BLOCK, t)
    chained = t >= tm
    tril3, masks, hm, nlev = _rwkv_tables(tm, blk)
    w = RWKV_WIDTH
    row = lambda i: (i, 0)
    whole = lambda i: (0, 0)
    whole3 = lambda i: (0, 0, 0)
    vec = pl.BlockSpec((1, w), whole)
    zspecs = [pl.BlockSpec((tm, S5_WIDTH), lambda i, c=c: (i, 1 + c)) for c in range(RWKV_ZBLOCKS)]
    if chained:
        tps = t // tm
        lead = zspecs + [pl.BlockSpec((SUBLANES, S5_WIDTH),
                                      lambda i, c=c: (jnp.maximum(i * (tm // SUBLANES) - 1, 0), 1 + c))
                         for c in range(RWKV_ZBLOCKS)]
        lead.append(pl.BlockSpec((1, 1, RWKV_SHIFT_WIDTH), lambda i: (i // tps, 0, 0)))
        args = (p3,) * (2 * RWKV_ZBLOCKS) + (prev,)
    else:
        tps = 1
        lead = zspecs + [pl.BlockSpec((tm, RWKV_SHIFT_WIDTH), row)]
        args = (p3,) * RWKV_ZBLOCKS + (prev,)
    return pl.pallas_call(
        functools.partial(_rwkv_prep_kernel, nlev=nlev, chained=chained, tiles_per_seq=tps),
        grid=(m // tm,),
        in_specs=lead + [
            pl.BlockSpec((1, RWKV_SHIFT_WIDTH), whole),
            vec, vec,
            pl.BlockSpec((LANES, w), whole), pl.BlockSpec((LANES, w), whole), pl.BlockSpec((LANES, w), whole),
            vec, vec, vec,
            pl.BlockSpec((2 * SEG_CHUNK, SEG_CHUNK), whole),
            pl.BlockSpec(tril3.shape, whole), pl.BlockSpec(masks.shape, whole3), pl.BlockSpec(hm.shape, whole3),
        ],
        out_specs=[pl.BlockSpec((tm, w), row)] * 10,
        out_shape=[jax.ShapeDtypeStruct((m, w), F32)] * 10,
        compiler_params=_cparams(("parallel",)),
        name="rwkv_prep",
    )(*args, cs["mu"], cs["w0"], cs["a0"], cs["w2"], cs["a2"], cs["g2"], cs["k_k"], cs["k_a"], cs["r_k"],
      cs["e2w"], tril3, masks, hm)


def _rwkv_rec_kernel(*refs, nbp, tc, blk, aliased):
    ap_ref, rp_ref, bt_ref, kp_ref, cc_ref, y0_ref, v_ref, gm_ref, s0_ref, bd_ref = refs[:10]
    y_ref, so_ref, s_ref = refs[10 + (1 if aliased else 0):]
    c = pl.program_id(1)
    hd = RWKV_HEAD
    pairs = [(n, j) for n in range(nbp) for j in range(RWKV_PAIRS)]

    @pl.when(c == 0)
    def _():
        zero = jnp.zeros((hd, hd), F32)
        for idx, (n, j) in enumerate(pairs):
            top = jnp.concatenate([s0_ref[n, 2 * j], zero], axis=1)
            bot = jnp.concatenate([zero, s0_ref[n, 2 * j + 1]], axis=1)
            s_ref[idx] = jnp.concatenate([top, bot], axis=0)

    bd = bd_ref[...]

    def block(bi, carry):
        rows = pl.ds(pl.multiple_of(bi * blk, blk), blk)
        gmats = []
        for idx, (n, j) in enumerate(pairs):
            ls = slice(j * LANES, (j + 1) * LANES)
            sh, sl = _split2(s_ref[idx])
            xh = jnp.concatenate([ap_ref[n, rows, ls], rp_ref[n, rows, ls]], axis=0).astype(BF16)
            gmats.append(_dot_nt(jnp.concatenate([xh, xh], axis=1), jnp.concatenate([sh, sl], axis=1)))
        upds = []
        for idx, (n, j) in enumerate(pairs):
            ls = slice(j * LANES, (j + 1) * LANES)
            gmat = gmats[idx]
            e = -gmat[:blk] - cc_ref[n, rows, ls]
            y_ref[n, rows, ls] = gmat[blk:] + y0_ref[n, rows, ls]
            evh, evl = _split2(jnp.concatenate([e, v_ref[n, rows, ls]], axis=0))
            bkh, bkl = _split2(jnp.concatenate([bt_ref[n, rows, ls], kp_ref[n, rows, ls]], axis=0))
            upds.append(_dot_tn(jnp.concatenate([evh, evh, evl], axis=0), jnp.concatenate([bkh, bkl, bkh], axis=0)))
        for idx, (n, j) in enumerate(pairs):
            ls = slice(j * LANES, (j + 1) * LANES)
            gl = gm_ref[n, rows, ls][blk - 1:blk, :]
            s_ref[idx] = (s_ref[idx] + upds[idx] * bd) * gl
        return carry

    lax.fori_loop(0, tc // blk, block, 0)

    @pl.when(c == pl.num_programs(1) - 1)
    def _():
        for idx, (n, j) in enumerate(pairs):
            s = s_ref[idx]
            so_ref[n, 2 * j] = s[:hd, :hd]
            so_ref[n, 2 * j + 1] = s[hd:, hd:]


def _rwkv_rec(arrs, s0_all, l, so_prev, cs, nseq, t, nbp=4, tc=128):
    w = RWKV_WIDTH
    tc = min(tc, t)
    blk = min(RWKV_BLOCK, t)
    depth = s0_all.shape[0]
    seq_map = lambda i, c: (i, c, 0)
    st_map = lambda i, c: (l, i, 0, 0, 0)
    blkspec = pl.BlockSpec((nbp, tc, w), seq_map)
    st = pl.BlockSpec((None, nbp, RWKV_HEADS, RWKV_HEAD, RWKV_HEAD), st_map)
    aliased = so_prev is not None
    in_specs = [blkspec] * 8 + [st, pl.BlockSpec((LANES, LANES), lambda i, c: (0, 0))]
    args = list(arrs) + [s0_all, cs["bd"]]
    aliases = {}
    if aliased:
        in_specs.append(pl.BlockSpec(memory_space=pl.ANY))
        args.append(so_prev)
        aliases = {10: 1}
    return pl.pallas_call(
        functools.partial(_rwkv_rec_kernel, nbp=nbp, tc=tc, blk=blk, aliased=aliased),
        grid=(nseq // nbp, t // tc),
        in_specs=in_specs,
        out_specs=[blkspec, st],
        out_shape=[jax.ShapeDtypeStruct((nseq, t, w), F32),
                   jax.ShapeDtypeStruct((depth, nseq, RWKV_HEADS, RWKV_HEAD, RWKV_HEAD), F32)],
        scratch_shapes=[pltpu.VMEM((nbp * RWKV_PAIRS, LANES, LANES), F32)],
        input_output_aliases=aliases,
        compiler_params=_cparams(("parallel", "arbitrary")),
        name="rwkv_rec",
    )(*args)


def _rwkv_consts(mu, w0, w2, a0, a2, g2, k_k, k_a, r_k, ln_w, ln_b):
    w = RWKV_WIDTH
    seg = np.arange(SEG_CHUNK) // RWKV_HEAD
    e_w = (seg[:, None] == seg[None, :]).astype(np.float32)
    seg = np.arange(LANES) // RWKV_HEAD
    bd = (seg[:, None] == seg[None, :]).astype(np.float32)
    z64 = jnp.zeros((RWKV_HEAD, w), F32)
    return dict(
        mu=mu.reshape(1, -1), w0=w0.reshape(1, -1), a0=a0.reshape(1, -1),
        w2=jnp.concatenate([w2, z64], axis=0).astype(BF16),
        a2=jnp.concatenate([z64, a2], axis=0).astype(BF16),
        g2=g2.astype(BF16), k_k=k_k.reshape(1, -1), k_a=k_a.reshape(1, -1), r_k=r_k.reshape(1, -1),
        ln_w=ln_w.reshape(1, -1), ln_b=ln_b.reshape(1, -1),
        e2w=jnp.asarray(np.concatenate([e_w, e_w], axis=0), BF16),
        bd=jnp.asarray(bd, F32))


def _rwkv(p3, shift_l, wkv_all, l, so_prev, cs, nseq, t):
    if t >= RWKV_TM:
        prev = shift_l.reshape(nseq, 1, RWKV_SHIFT_WIDTH)
    else:
        z3 = p3.reshape(nseq, t, N_IN)[:, :, S5_WIDTH:S5_WIDTH + RWKV_SHIFT_WIDTH]
        prev = jnp.concatenate([shift_l[:, None, :], z3[:, :-1]], axis=1).reshape(nseq * t, RWKV_SHIFT_WIDTH)
    outs = _rwkv_prep(p3, prev, cs, nseq, t)
    arrs = [a.reshape(nseq, t, RWKV_WIDTH) for a in outs[:8]]
    y, so = _rwkv_rec(arrs, wkv_all, l, so_prev, cs, nseq, t)
    return (y.reshape(nseq * t, RWKV_WIDTH), outs[9], outs[8]), so


def _hgrn_tables(c):
    t = np.arange(c)
    j = np.arange(c)
    mats = [(j[None, :] <= t[:, None]), (j[None, :] > t[:, None])]
    masks = []
    n = 2
    while n <= c:
        m = n // 2
        bs = (t // n) * n
        hi_half = (t % n) >= m
        mats.append((hi_half[:, None] & (j[None, :] >= (bs + m)[:, None]) & (j[None, :] <= t[:, None]))
                    | ((~hi_half)[:, None] & (j[None, :] > t[:, None]) & (j[None, :] <= (bs + m - 1)[:, None])))
        masks.append(((t[:, None] // n) == (t[None, :] // n)) & hi_half[:, None] & (~hi_half)[None, :])
        n *= 2
    w = np.concatenate(mats, axis=0).astype(np.float32)
    w3 = np.concatenate([w, w, w], axis=1)
    return jnp.asarray(w3, BF16), jnp.asarray(np.stack(masks).astype(np.float32)), len(masks)


def _hgrn_kernel(*refs, rb, c, nlev, chained, aliased, sub):
    q_ref, f_ref, i_ref, g_ref, s0_ref, lb_ref, nw_ref, w3_ref, mk_ref = refs[:9]
    o_ref, so_ref, st_ref = refs[9 + (1 if aliased else 0):]
    cidx = pl.program_id(1)
    hdim = HGRN_DIM

    if chained:
        @pl.when(cidx == 0)
        def _():
            for h in range(HGRN_HEADS):
                st_ref[h] = s0_ref[0, h].T

    w3 = w3_ref[...]
    row_i = lax.broadcasted_iota(jnp.int32, (c, c), 0)
    col_i = lax.broadcasted_iota(jnp.int32, (c, c), 1)
    eye = (row_i == col_i).astype(F32)

    heads = range(HGRN_HEADS)

    def lanes(h):
        return slice(h * hdim, (h + 1) * hdim)

    def block(it, carry):
        rows_s = [pl.ds(pl.multiple_of((it * sub + s) * c, c), c) for s in range(sub)]
        items = [(s, h) for s in range(sub) for h in heads]
        fg = {k: lb_ref[:, lanes(k[1])] + (1.0 - lb_ref[:, lanes(k[1])]) * _sigmoid(f_ref[rows_s[k[0]], lanes(k[1])])
              for k in items}
        kk = {k: 1.0 - fg[k] for k in items}
        qs = {}
        for k in items:
            q = q_ref[rows_s[k[0]], lanes(k[1])]
            qs[k] = q * _sigmoid(q)
        ex = {k: jnp.exp(_dot(w3, jnp.concatenate(_split3(jnp.log(fg[k])), axis=0))) for k in items}
        vb = {k: i_ref[rows_s[k[0]], lanes(k[1])].astype(BF16) for k in items}
        lev = {(k, lv): _dot_nt((qs[k] * ex[k][(2 + lv) * c:(3 + lv) * c]).astype(BF16),
                                (kk[k] * ex[k][(2 + lv) * c:(3 + lv) * c]).astype(BF16))
               for k in items for lv in range(nlev)}
        att = {}
        for k in items:
            a = eye * jnp.sum(qs[k] * kk[k], axis=-1, keepdims=True)
            for lv in range(nlev):
                a = a + jnp.where(mk_ref[lv] > 0.0, lev[(k, lv)], 0.0)
            att[k] = a.astype(BF16)
        intra = {k: _dot(att[k], vb[k]) for k in items}
        kv = {k: _dot_tn(vb[k], (kk[k] * ex[k][c:2 * c]).astype(BF16)) for k in items}
        qe = {k: (qs[k] * ex[k][0:c]).astype(BF16) for k in items}
        st = {h: st_ref[h] for h in heads} if chained else None
        for s in range(sub):
            if not chained:
                st = {h: s0_ref[it * sub + s, h].T for h in heads}
            inter = {h: _dot_nt(qe[(s, h)], st[h].astype(BF16)) for h in heads}
            st = {h: st[h] * ex[(s, h)][c - 1:c, :] + kv[(s, h)] for h in heads}
            if not chained:
                for h in heads:
                    so_ref[it * sub + s, h] = st[h].T
            for h in heads:
                o = inter[h] + intra[(s, h)]
                o = o * lax.rsqrt(jnp.mean(o * o, axis=-1, keepdims=True) + HGRN_EPS)
                gg = g_ref[rows_s[s], lanes(h)]
                o_ref[rows_s[s], lanes(h)] = o * nw_ref[:, lanes(h)] * (gg * _sigmoid(gg))
        if chained:
            for h in heads:
                st_ref[h] = st[h]
        return carry

    lax.fori_loop(0, rb // (c * sub), block, 0)

    if chained:
        @pl.when(cidx == pl.num_programs(1) - 1)
        def _():
            for h in range(HGRN_HEADS):
                so_ref[0, h] = st_ref[h].T


def _hgrn(p3, s0_all, l, so_prev, lb, nw, nseq, t):
    rows = nseq * t
    depth = s0_all.shape[0]
    c = math.gcd(t, 64)
    chained = t > c
    w3, masks, nlev = _hgrn_tables(c)
    w = HGRN_WIDTH
    if chained:
        rb = 256
        nblk = t // rb
        grid = (nseq, nblk)
        col = lambda cb: (lambda n, k: (n * nblk + k, cb))
        out_map = lambda n, k: (n * nblk + k, 0)
        nsb = 1
    else:
        nsb = 16
        rb = nsb * c
        grid = (rows // rb, 1)
        col = lambda cb: (lambda n, k: (n, cb))
        out_map = lambda n, k: (n, 0)
    st_spec = pl.BlockSpec((None, nsb, HGRN_HEADS, HGRN_DIM, HGRN_DIM), lambda n, k: (l, n, 0, 0, 0))
    whole2 = lambda n, k: (0, 0)
    whole3 = lambda n, k: (0, 0, 0)
    aliased = so_prev is not None
    in_specs = [pl.BlockSpec((rb, w), col(4)), pl.BlockSpec((rb, w), col(5)), pl.BlockSpec((rb, w), col(6)),
                pl.BlockSpec((rb, w), col(7)), st_spec,
                pl.BlockSpec((1, w), whole2), pl.BlockSpec((1, w), whole2),
                pl.BlockSpec(w3.shape, whole2), pl.BlockSpec(masks.shape, whole3)]
    args = [p3, p3, p3, p3, s0_all, lb, nw, w3, masks]
    aliases = {}
    if aliased:
        in_specs.append(pl.BlockSpec(memory_space=pl.ANY))
        args.append(so_prev)
        aliases = {9: 1}
    return pl.pallas_call(
        functools.partial(_hgrn_kernel, rb=rb, c=c, nlev=nlev, chained=chained, aliased=aliased,
                          sub=2 if chained else 4),
        grid=grid,
        in_specs=in_specs,
        out_specs=[pl.BlockSpec((rb, w), out_map), st_spec],
        out_shape=[jax.ShapeDtypeStruct((rows, w), F32),
                   jax.ShapeDtypeStruct((depth, nseq, HGRN_HEADS, HGRN_DIM, HGRN_DIM), F32)],
        scratch_shapes=[pltpu.VMEM((HGRN_HEADS, HGRN_DIM, HGRN_DIM), F32)],
        input_output_aliases=aliases,
        compiler_params=_cparams(("parallel", "arbitrary")),
        name="hgrn_chain" if chained else "hgrn_tile",
    )(*args)


def _run(x, nseq, t, states, lw, ffw, norm_final):
    s5_re0, s5_im0, shift0, wkv0, hgrn0 = states
    depth = len(lw)
    s5_re0 = s5_re0.reshape(depth, nseq, S5_LANES)
    s5_im0 = s5_im0.reshape(depth, nseq, S5_LANES)
    s5_re_l, s5_im_l, shift_l = [], [], []
    wkv_out = jnp.zeros(wkv0.shape, F32)
    hgrn_out = jnp.zeros(hgrn0.shape, F32)
    cast = not isinstance(ffw["f1g"], list)
    ffb = {k: [] for k in ffw}

    def ffn(x, nw, names, l, fnw=None):
        if not cast:
            return _ffn(x, nw, *(ffw[k][l][None] for k in names), 0, fnw=fnw)
        res = _ffn(x, nw, *(ffw[k] for k in names), l, cast=True, fnw=fnw)
        for k, wb in zip(names, res[1:]):
            ffb[k].append(wb)
        return res[0]

    for l in range(depth):
        w = lw[l]
        x = ffn(x, w["norm_ffn1"], ("f1g", "f1u", "f1d"), l)
        if cast:
            p3, wb = _inproj(x, w["norm_mix"], ffw["w_in"], l, cast=True)
            ffb["w_in"].append(wb)
        else:
            p3 = _inproj(x, w["norm_mix"], ffw["w_in"][l][None], 0)
        y_s5, s_re, s_im = _s5(p3, s5_re0, s5_im0, l, w["s5"], nseq, t)
        (y_rw, bonus, gate), wkv_out = _rwkv(p3, shift0[l], wkv0, l, wkv_out, w["rwkv"], nseq, t)
        y_hg, hgrn_out = _hgrn(p3, hgrn0, l, hgrn_out, w["hgrn_lb"], w["hgrn_nw"], nseq, t)
        x = _outproj(x, y_s5, y_rw, bonus, gate, y_hg, w["w_out"], l, w["rwkv"])
        x = ffn(x, w["norm_ffn2"], ("f2g", "f2u", "f2d"), l,
                fnw=norm_final.reshape(1, -1) if l == depth - 1 else None)
        s5_re_l.append(s_re)
        s5_im_l.append(s_im)
        shift_l.append(p3.reshape(nseq, t, N_IN)[:, t - 1, S5_WIDTH:S5_WIDTH + RWKV_SHIFT_WIDTH])
    return (x,jnp.stack(s5_re_l), jnp.stack(s5_im_l), jnp.stack(shift_l), wkv_out, hgrn_out), ffb


def kernel(x_prompt, x_sample, state_s5_re, state_s5_im, state_rwkv_shift, state_rwkv_wkv, state_hgrn, norm_ffn1, ffn1_w_gate, ffn1_w_up, ffn1_w_down, norm_mix, w_in, s5_a_re, s5_a_im, s5_log_dt, s5_b_re, s5_b_im, s5_c_re, s5_c_im, s5_d, s5_w_glu, s5_b_glu, rwkv_mu, rwkv_w0, rwkv_w2, rwkv_a0, rwkv_a2, rwkv_g2, rwkv_k_k, rwkv_k_a, rwkv_r_k, rwkv_ln_w, rwkv_ln_b, hgrn_lb_raw, hgrn_norm_w, w_out, norm_ffn2, ffn2_w_gate, ffn2_w_up, ffn2_w_down, norm_final):
    depth = w_in.shape[0]
    nb, seq, d = x_prompt.shape
    ns, dseq, _ = x_sample.shape

    p_lb = jax.nn.softmax(hgrn_lb_raw.astype(F32), axis=0)
    lower_bounds = jnp.cumsum(p_lb, axis=0) - p_lb[0]

    w_out_b = w_out.astype(BF16)
    s5_all = jax.vmap(_s5_consts)(s5_a_re, s5_a_im, s5_log_dt, s5_b_re, s5_b_im, s5_c_re, s5_c_im,
                                  s5_d, s5_w_glu, s5_b_glu)
    lw = []
    for l in range(depth):
        lw.append(dict(
            w_out=w_out_b,
            norm_ffn1=norm_ffn1[l].reshape(1, -1), norm_mix=norm_mix[l].reshape(1, -1),
            norm_ffn2=norm_ffn2[l].reshape(1, -1),
            s5=s5_all,
            rwkv=_rwkv_consts(rwkv_mu[l], rwkv_w0[l], rwkv_w2[l], rwkv_a0[l], rwkv_a2[l], rwkv_g2[l],
                              rwkv_k_k[l], rwkv_k_a[l], rwkv_r_k[l], rwkv_ln_w[l], rwkv_ln_b[l]),
            hgrn_lb=lower_bounds[l].reshape(1, -1), hgrn_nw=hgrn_norm_w[l].reshape(1, -1),
        ))

    def zeros_like_state(s):
        return jnp.zeros((depth, nb) + s.shape[2:], F32)

    p_states = tuple(zeros_like_state(s) for s in
                     (state_s5_re, state_s5_im, state_rwkv_shift, state_rwkv_wkv, state_hgrn))
    s_states = (state_s5_re, state_s5_im, state_rwkv_shift, state_rwkv_wkv, state_hgrn)
    ffw = dict(f1g=ffn1_w_gate, f1u=ffn1_w_up, f1d=ffn1_w_down, f2g=ffn2_w_gate, f2u=ffn2_w_up, f2d=ffn2_w_down,
               w_in=w_in)
    (y_s, s5re_s, s5im_s, shift_s, wkv_s, hgrn_s), ffb = _run(
        x_sample.reshape(ns * dseq, d), ns, dseq, s_states, lw, ffw, norm_final)
    (y_p, s5re_p, s5im_p, shift_p, wkv_p, hgrn_p), _ = _run(
        x_prompt.reshape(nb * seq, d), nb, seq, p_states, lw, ffb, norm_final)
    return (y_p.reshape(nb, seq, d), y_s.reshape(ns, dseq, d), s5re_p, s5im_p, shift_p, wkv_p, hgrn_p,
            s5re_s, s5im_s, shift_s, wkv_s, hgrn_s)
```
